```python
import math
import jax, jax.numpy as jnp
from jax import lax
import numpy as np

D_MODEL = 1024
BATCH = 8
SEQ = 2048
DEPTH = 2

MEM_LEN = 256
EPS = 1e-6
Q_BLOCK = 128

A_HEADS = 8
A_QK_DIM = 64
A_V_DIM = 2 * A_QK_DIM
A_WIDTH = A_HEADS * A_V_DIM
B_WIDTH = D_MODEL // 2
CONV_W = 3
C_HEADS = 12
C_DIM = 128
C_WIDTH = C_HEADS * C_DIM
X_HEADS = 4
X_DIM = 128
X_WIDTH = X_HEADS * X_DIM

EVEN_MIX = A_WIDTH + B_WIDTH + X_WIDTH
ODD_MIX = C_WIDTH + X_WIDTH
EVEN_IN = 4 * A_WIDTH + 4 * B_WIDTH + 2 * X_WIDTH
ODD_IN = 4 * C_WIDTH + C_HEADS + 2 * X_WIDTH

kernel_name = "hybrid_diffattn_shortconv_fox_memxattn"


def rms_norm(x, g):
    xf = x.astype(jnp.float32)
    y = xf * lax.rsqrt(jnp.mean(xf * xf, axis=-1, keepdims=True) + EPS)
    return (y * g.astype(jnp.float32)).astype(x.dtype)


def alibi_slopes(n_heads):
    return jnp.asarray(np.array([2.0 ** (-8.0 * (i + 1) / n_heads) for i in range(n_heads)], dtype=np.float32))


def sweep_query_blocks(block_fn, *q_arrays):
    seq = q_arrays[0].shape[2]
    nb = seq // Q_BLOCK

    def split(a):
        a = a.reshape(a.shape[:2] + (nb, Q_BLOCK) + a.shape[3:])
        return jnp.moveaxis(a, 2, 0)

    starts = jnp.arange(nb, dtype=jnp.int32) * Q_BLOCK
    out = lax.map(lambda args: block_fn(args[0], *args[1:]), (starts,) + tuple(split(a) for a in q_arrays))
    out = jnp.moveaxis(out, 0, 2)
    return out.reshape(out.shape[:2] + (seq,) + out.shape[4:])


def differential_attention(q, k, v, q_g, k_g, lam_params, out_g, lam_init):
    bsz, seq, _ = q.shape
    q = rms_norm(q.reshape(bsz, seq, A_HEADS, 2, A_QK_DIM), q_g).transpose(0, 2, 3, 1, 4)
    k = rms_norm(k.reshape(bsz, seq, A_HEADS, 2, A_QK_DIM), k_g).transpose(0, 2, 3, 1, 4)
    v = v.reshape(bsz, seq, A_HEADS, A_V_DIM).transpose(0, 2, 1, 3)
    lp = lam_params.astype(jnp.float32)
    lam = jnp.exp(jnp.sum(lp[0] * lp[1])) - jnp.exp(jnp.sum(lp[2] * lp[3])) + lam_init
    slopes = alibi_slopes(A_HEADS)
    scale = A_QK_DIM ** -0.5
    k1, k2 = k[:, :, 0], k[:, :, 1]
    k_pos = jnp.arange(seq, dtype=jnp.int32)

    def block(start, q1b, q2b):
        q_pos = start + jnp.arange(Q_BLOCK, dtype=jnp.int32)
        dist = q_pos[:, None] - k_pos[None, :]
        causal = dist >= 0
        bias = -slopes[:, None, None] * dist.astype(jnp.float32)
        s1 = jnp.einsum('bhqd,bhkd->bhqk', q1b, k1).astype(jnp.float32) * scale + bias
        s2 = jnp.einsum('bhqd,bhkd->bhqk', q2b, k2).astype(jnp.float32) * scale + bias
        p1 = jax.nn.softmax(jnp.where(causal, s1, -jnp.inf), axis=-1)
        p2 = jax.nn.softmax(jnp.where(causal, s2, -jnp.inf), axis=-1)
        attn = (p1 - lam * p2).astype(v.dtype)
        return jnp.einsum('bhqk,bhkd->bhqd', attn, v)

    o = sweep_query_blocks(block, q[:, :, 0], q[:, :, 1])
    o = rms_norm(o, out_g) * (1.0 - lam_init)
    return o.transpose(0, 2, 1, 3).reshape(bsz, seq, A_WIDTH)


def causal_depthwise_conv(u, w, b):
    c = u.shape[-1]
    y = lax.conv_general_dilated(u, w[:, None, :].astype(u.dtype), window_strides=(1,),
                                 padding=[(CONV_W - 1, 0)],
                                 dimension_numbers=('NWC', 'WIO', 'NWC'),
                                 feature_group_count=c)
    return y + b.astype(u.dtype)


def short_gated_conv(h, c_gate, b_gate, w, b):
    return b_gate * causal_depthwise_conv(c_gate * h, w, b)


def forgetting_attention(q, k, v, f_logit, f_bias, q_g, k_g):
    bsz, seq, _ = q.shape
    q = rms_norm(q.reshape(bsz, seq, C_HEADS, C_DIM), q_g).transpose(0, 2, 1, 3)
    k = rms_norm(k.reshape(bsz, seq, C_HEADS, C_DIM), k_g).transpose(0, 2, 1, 3)
    v = v.reshape(bsz, seq, C_HEADS, C_DIM).transpose(0, 2, 1, 3)
    log_f = jax.nn.log_sigmoid(f_logit.astype(jnp.float32) + f_bias.astype(jnp.float32))
    cum = jnp.cumsum(log_f, axis=1).transpose(0, 2, 1)
    scale = C_DIM ** -0.5
    k_pos = jnp.arange(seq, dtype=jnp.int32)

    def block(start, qb, cqb):
        q_pos = start + jnp.arange(Q_BLOCK, dtype=jnp.int32)
        causal = q_pos[:, None] >= k_pos[None, :]
        s = jnp.einsum('bhqd,bhkd->bhqk', qb, k).astype(jnp.float32) * scale
        s = s + (cqb[..., :, None] - cum[:, :, None, :])
        p = jax.nn.softmax(jnp.where(causal, s, -jnp.inf), axis=-1).astype(v.dtype)
        return jnp.einsum('bhqk,bhkd->bhqd', p, v)

    o = sweep_query_blocks(block, q, cum)
    return o.transpose(0, 2, 1, 3).reshape(bsz, seq, C_WIDTH)


def memory_cross_attention(q, mem, mem_g, w_mem_kv, q_g, k_g):
    bsz, seq, _ = q.shape
    m = mem.shape[1]
    mk, mv = jnp.split(rms_norm(mem, mem_g) @ w_mem_kv, 2, axis=-1)
    q = rms_norm(q.reshape(bsz, seq, X_HEADS, X_DIM), q_g)
    mk = rms_norm(mk.reshape(bsz, m, X_HEADS, X_DIM), k_g)
    mv = mv.reshape(bsz, m, X_HEADS, X_DIM)
    s = jnp.einsum('bshd,bmhd->bhsm', q, mk).astype(jnp.float32) * (X_DIM ** -0.5)
    p = jax.nn.softmax(s, axis=-1).astype(mv.dtype)
    o = jnp.einsum('bhsm,bmhd->bshd', p, mv)
    return o.reshape(bsz, seq, X_WIDTH)


def even_layer(h, mem, layer, norm_g, w_in, w_out, a_qn, a_kn, a_lam, a_on,
               b_cw, b_cb, x_qn, x_kn, mem_g, w_mem_kv):
    u = rms_norm(h, norm_g)
    proj = u @ w_in
    cuts = np.cumsum([A_WIDTH] * 4 + [B_WIDTH] * 4 + [X_WIDTH] * 2)[:-1].tolist()
    aq, ak, av, az, bh, bc, bb, bz, xq, xz = jnp.split(proj, cuts, axis=-1)
    lam_init = 0.8 - 0.6 * math.exp(-0.3 * layer)
    ya = differential_attention(aq, ak, av, a_qn, a_kn, a_lam, a_on, lam_init)
    yb = short_gated_conv(bh, bc, bb, b_cw, b_cb)
    yx = memory_cross_attention(xq, mem, mem_g, w_mem_kv, x_qn, x_kn)
    y = jnp.concatenate([ya * jax.nn.silu(az), yb * jax.nn.silu(bz), yx * jax.nn.silu(xz)], axis=-1)
    return h + y @ w_out


def odd_layer(h, mem, norm_g, w_in, w_out, c_qn, c_kn, c_fb, x_qn, x_kn, mem_g, w_mem_kv):
    u = rms_norm(h, norm_g)
    proj = u @ w_in
    cuts = np.cumsum([C_WIDTH] * 4 + [C_HEADS] + [X_WIDTH] * 2)[:-1].tolist()
    cq, ck, cv, cz, cf, xq, xz = jnp.split(proj, cuts, axis=-1)
    yc = forgetting_attention(cq, ck, cv, cf, c_fb, c_qn, c_kn)
    yx = memory_cross_attention(xq, mem, mem_g, w_mem_kv, x_qn, x_kn)
    y = jnp.concatenate([yc * jax.nn.silu(cz), yx * jax.nn.silu(xz)], axis=-1)
    return h + y @ w_out


def setup_inputs(seed: int = 0) -> dict:
    key = jax.random.key(seed)
    ks = jax.random.split(key, 32)
    it = iter(range(32))
    ne = (DEPTH + 1) // 2
    no = DEPTH // 2

    def nrm(shape, scale):
        return scale * jax.random.normal(ks[next(it)], shape, jnp.float32)

    def gain(shape):
        return 1.0 + 0.02 * jax.random.normal(ks[next(it)], shape, jnp.float32)

    return {
        "x": nrm((BATCH, SEQ, D_MODEL), 1.0),
        "mem": nrm((BATCH, MEM_LEN, D_MODEL), 1.0),
        "e_norm_g": gain((ne, D_MODEL)),
        "e_w_in": nrm((ne, D_MODEL, EVEN_IN), D_MODEL ** -0.5),
        "e_w_out": nrm((ne, EVEN_MIX, D_MODEL), EVEN_MIX ** -0.5),
        "e_a_q_norm_g": gain((ne, A_QK_DIM)),
        "e_a_k_norm_g": gain((ne, A_QK_DIM)),
        "e_a_lambda": nrm((ne, 4, A_QK_DIM), 0.1),
        "e_a_out_norm_g": gain((ne, A_V_DIM)),
        "e_b_conv_w": nrm((ne, CONV_W, B_WIDTH), CONV_W ** -0.5),
        "e_b_conv_b": nrm((ne, B_WIDTH), 0.02),
        "e_x_q_norm_g": gain((ne, X_DIM)),
        "e_x_k_norm_g": gain((ne, X_DIM)),
        "e_mem_norm_g": gain((ne, D_MODEL)),
        "e_w_mem_kv": nrm((ne, D_MODEL, 2 * X_WIDTH), D_MODEL ** -0.5),
        "o_norm_g": gain((no, D_MODEL)),
        "o_w_in": nrm((no, D_MODEL, ODD_IN), D_MODEL ** -0.5),
        "o_w_out": nrm((no, ODD_MIX, D_MODEL), ODD_MIX ** -0.5),
        "o_c_q_norm_g": gain((no, C_DIM)),
        "o_c_k_norm_g": gain((no, C_DIM)),
        "o_c_forget_b": 3.0 + nrm((no, C_HEADS), 0.5),
        "o_x_q_norm_g": gain((no, X_DIM)),
        "o_x_k_norm_g": gain((no, X_DIM)),
        "o_mem_norm_g": gain((no, D_MODEL)),
        "o_w_mem_kv": nrm((no, D_MODEL, 2 * X_WIDTH), D_MODEL ** -0.5),
    }


def reference(x, mem, e_norm_g, e_w_in, e_w_out, e_a_q_norm_g, e_a_k_norm_g, e_a_lambda,
              e_a_out_norm_g, e_b_conv_w, e_b_conv_b, e_x_q_norm_g, e_x_k_norm_g,
              e_mem_norm_g, e_w_mem_kv, o_norm_g, o_w_in, o_w_out, o_c_q_norm_g,
              o_c_k_norm_g, o_c_forget_b, o_x_q_norm_g, o_x_k_norm_g, o_mem_norm_g,
              o_w_mem_kv):
    h = x
    for layer in range(DEPTH):
        i = layer // 2
        if layer % 2 == 0:
            h = even_layer(h, mem, layer, e_norm_g[i], e_w_in[i], e_w_out[i],
                           e_a_q_norm_g[i], e_a_k_norm_g[i], e_a_lambda[i], e_a_out_norm_g[i],
                           e_b_conv_w[i], e_b_conv_b[i], e_x_q_norm_g[i], e_x_k_norm_g[i],
                           e_mem_norm_g[i], e_w_mem_kv[i])
        else:
            h = odd_layer(h, mem, o_norm_g[i], o_w_in[i], o_w_out[i],
                          o_c_q_norm_g[i], o_c_k_norm_g[i], o_c_forget_b[i],
                          o_x_q_norm_g[i], o_x_k_norm_g[i], o_mem_norm_g[i], o_w_mem_kv[i])
    return h
```

```python
import functools
import math

import numpy as np
import jax
import jax.numpy as jnp
from jax import lax
from jax.experimental import pallas as pl
from jax.experimental.pallas import tpu as pltpu

F32 = jnp.float32
BF16 = jnp.bfloat16

D_MODEL = 1024
MEM_LEN = 256
EPS = 1e-6
NEG = -1e30

A_HEADS = 8
A_QK_DIM = 64
A_V_DIM = 128
A_WIDTH = 1024
B_WIDTH = 512
CONV_W = 3
C_HEADS = 12
C_DIM = 128
C_WIDTH = 1536
X_HEADS = 4
X_DIM = 128
X_WIDTH = 512
MIX = 2048

LANES = 128
VMEM_CAP = 56 * 1024 * 1024

PROJ_TM = 1024
PROJ_TN = 1024
OUT_TM = 512
ATT_T = 256
X_TQ = 512
CONV_TS = 512


def _vmem_limit(block_bytes, scratch_bytes=0, temp_bytes=0):
    need = 2 * block_bytes + scratch_bytes + temp_bytes + (4 << 20)
    return int(min(max(need, 16 << 20), VMEM_CAP))


def _silu(z):
    return z * jax.nn.sigmoid(z)


def _nt_dot(a, b):
    return lax.dot_general(a, b, (((1,), (1,)), ((), ())), preferred_element_type=F32)


def _proj_kernel(x_ref, g_ref, w_ref, o_ref, xn_ref):
    @pl.when(pl.program_id(1) == 0)
    def _():
        x = x_ref[...]
        ms = jnp.mean(x * x, axis=-1, keepdims=True)
        xn_ref[...] = (x * lax.rsqrt(ms + EPS) * g_ref[...]).astype(BF16)

    o_ref[...] = jnp.dot(xn_ref[...], w_ref[...], preferred_element_type=F32).astype(o_ref.dtype)


def _proj_side_kernel(x_ref, g_ref, w_ref, ws_ref, o_ref, side_ref, xn_ref):
    @pl.when(pl.program_id(1) == 0)
    def _():
        x = x_ref[...]
        ms = jnp.mean(x * x, axis=-1, keepdims=True)
        xn = (x * lax.rsqrt(ms + EPS) * g_ref[...]).astype(BF16)
        xn_ref[...] = xn
        side_ref[...] = jnp.dot(xn, ws_ref[...], preferred_element_type=F32)

    o_ref[...] = jnp.dot(xn_ref[...], w_ref[...], preferred_element_type=F32).astype(o_ref.dtype)


def _norm_proj(h, g, w, w_side=None):
    t, d = h.shape
    n = w.shape[1]
    tm, tn = PROJ_TM, PROJ_TN
    grid = (t // tm, n // tn)
    blocks = tm * d * 4 + d * tn * 2 + tm * tn * 2
    in_specs = [
        pl.BlockSpec((tm, d), lambda i, j: (i, 0)),
        pl.BlockSpec((1, d), lambda i, j: (0, 0)),
        pl.BlockSpec((d, tn), lambda i, j: (0, j)),
    ]
    out_specs = pl.BlockSpec((tm, tn), lambda i, j: (i, j))
    out_shape = jax.ShapeDtypeStruct((t, n), BF16)
    args = [h, g.reshape(1, d), w]
    body = _proj_kernel
    if w_side is not None:
        ns = w_side.shape[1]
        in_specs.append(pl.BlockSpec((d, ns), lambda i, j: (0, 0)))
        out_specs = [out_specs, pl.BlockSpec((tm, ns), lambda i, j: (i, 0))]
        out_shape = [out_shape, jax.ShapeDtypeStruct((t, ns), F32)]
        args.append(w_side)
        body = _proj_side_kernel
        blocks += d * ns * 2 + tm * ns * 4
    return pl.pallas_call(
        body,
        grid=grid,
        in_specs=in_specs,
        out_specs=out_specs,
        out_shape=out_shape,
        scratch_shapes=[pltpu.VMEM((tm, d), BF16)],
        compiler_params=pltpu.CompilerParams(
            dimension_semantics=("parallel", "arbitrary"),
            vmem_limit_bytes=_vmem_limit(blocks, tm * d * 2, tm * tn * 4 + tm * d * 4)),
        name="norm_proj",
    )(*args)


def _out_kernel(*refs, widths):
    ys = refs[:len(widths)]
    w_ref, h_ref, o_ref = refs[len(widths):]
    acc = h_ref[...]
    off = 0
    for y_ref, width in zip(ys, widths):
        acc = acc + jnp.dot(y_ref[...], w_ref[off:off + width, :], preferred_element_type=F32)
        off += width
    o_ref[...] = acc


def _out_proj(ys, w, h):
    t, d = h.shape
    tm = OUT_TM
    widths = tuple(int(y.shape[1]) for y in ys)
    blocks = sum(tm * wd * 2 for wd in widths) + w.size * 2 + 2 * tm * d * 4
    in_specs = [pl.BlockSpec((tm, wd), lambda i: (i, 0)) for wd in widths]
    in_specs += [pl.BlockSpec(w.shape, lambda i: (0, 0)), pl.BlockSpec((tm, d), lambda i: (i, 0))]
    return pl.pallas_call(
        functools.partial(_out_kernel, widths=widths),
        grid=(t // tm,),
        in_specs=in_specs,
        out_specs=pl.BlockSpec((tm, d), lambda i: (i, 0)),
        out_shape=jax.ShapeDtypeStruct((t, d), F32),
        compiler_params=pltpu.CompilerParams(
            dimension_semantics=("parallel",),
            vmem_limit_bytes=_vmem_limit(blocks, 0, 2 * tm * d * 4)),
        name="out_proj",
    )(*ys, w, h)


def _softmax_step(s, vb, m, l, acc):
    m_new = jnp.maximum(m, jnp.max(s, axis=-1, keepdims=True))
    alpha = jnp.exp(m - m_new)
    p = jnp.exp(s - m_new)
    l_new = alpha * l + jnp.sum(p, axis=-1, keepdims=True)
    acc_new = alpha * acc + jnp.dot(p.astype(BF16), vb, preferred_element_type=F32)
    return m_new, l_new, acc_new


def _causal_mask(t):
    row = lax.broadcasted_iota(jnp.int32, (t, t), 0)
    col = lax.broadcasted_iota(jnp.int32, (t, t), 1)
    return row >= col


def _half_rms(x, lo):
    x2 = x * x
    ms_lo = jnp.sum(jnp.where(lo, x2, 0.0), axis=-1, keepdims=True) * (1.0 / A_QK_DIM)
    ms_hi = jnp.sum(jnp.where(lo, 0.0, x2), axis=-1, keepdims=True) * (1.0 / A_QK_DIM)
    return lax.rsqrt(jnp.where(lo, ms_lo, ms_hi) + EPS)


def _diff_kernel(slopes_ref, q_ref, k_ref, v_ref, z_ref, qg_ref, kg_ref, lam_ref, og_ref,
                 o_ref, kn_ref, *, lam_init):
    t = ATT_T
    h = pl.program_id(1)
    qi = pl.program_id(2)

    @pl.when(qi == 0)
    def _():
        k = k_ref[...].astype(F32)
        lo_k = lax.broadcasted_iota(jnp.int32, k.shape, 1) < A_QK_DIM
        kn_ref[...] = (k * _half_rms(k, lo_k) * kg_ref[...]).astype(BF16)

    q = q_ref[...].astype(F32)
    lo = lax.broadcasted_iota(jnp.int32, q.shape, 1) < A_QK_DIM
    qn = q * _half_rms(q, lo) * qg_ref[...] * (A_QK_DIM ** -0.5)
    q1 = jnp.where(lo, qn, 0.0).astype(BF16)
    q2 = jnp.where(lo, 0.0, qn).astype(BF16)

    slope = slopes_ref[h]
    row = lax.broadcasted_iota(jnp.int32, (t, t), 0)
    col = lax.broadcasted_iota(jnp.int32, (t, t), 1)
    rel_bias = slope * (col - row).astype(F32)

    def step(j, carry, masked):
        m1, l1, a1, m2, l2, a2 = carry
        start = pl.multiple_of(j * t, t)
        kb = kn_ref[pl.ds(start, t), :]
        vb = v_ref[pl.ds(start, t), :]
        bias = rel_bias + slope * ((j - qi) * t).astype(F32)
        s1 = _nt_dot(q1, kb) + bias
        s2 = _nt_dot(q2, kb) + bias
        if masked:
            s1 = jnp.where(row >= col, s1, NEG)
            s2 = jnp.where(row >= col, s2, NEG)
        m1, l1, a1 = _softmax_step(s1, vb, m1, l1, a1)
        m2, l2, a2 = _softmax_step(s2, vb, m2, l2, a2)
        return m1, l1, a1, m2, l2, a2

    m0 = jnp.full((t, 1), NEG, F32)
    l0 = jnp.zeros((t, 1), F32)
    a0 = jnp.zeros((t, A_V_DIM), F32)
    carry = lax.fori_loop(0, qi, lambda j, c: step(j, c, False), (m0, l0, a0, m0, l0, a0))
    _, l1, a1, _, l2, a2 = step(qi, carry, True)

    lp = lam_ref[...]
    lam = (jnp.exp(jnp.sum(lp[0:1, :] * lp[1:2, :], axis=-1, keepdims=True))
           - jnp.exp(jnp.sum(lp[2:3, :] * lp[3:4, :], axis=-1, keepdims=True)) + lam_init)
    o = a1 / l1 - lam * (a2 / l2)
    ms = jnp.mean(o * o, axis=-1, keepdims=True)
    on = o * lax.rsqrt(ms + EPS) * og_ref[...] * (1.0 - lam_init)
    o_ref[...] = (on * _silu(z_ref[...].astype(F32))).astype(o_ref.dtype)


def _diff_attention(proj, q_g, k_g, lam_params, out_g, lam_init, bsz, seq):
    t = ATT_T
    nq = seq // t
    hw = A_V_DIM
    slopes = jnp.asarray(np.array([2.0 ** (-8.0 * (i + 1) / A_HEADS) for i in range(A_HEADS)],
                                  dtype=np.float32))
    qg2 = jnp.concatenate([q_g, q_g]).reshape(1, hw)
    kg2 = jnp.concatenate([k_g, k_g]).reshape(1, hw)
    blocks = 2 * t * hw * 2 + 2 * seq * hw * 2 + t * hw * 2
    small = lambda shape: pl.BlockSpec(shape, lambda b, h, i: (0, 0))
    return pl.pallas_call(
        functools.partial(_diff_kernel, lam_init=lam_init),
        grid=(bsz, A_HEADS, nq),
        in_specs=[
            pl.BlockSpec(memory_space=pltpu.SMEM),
            pl.BlockSpec((t, hw), lambda b, h, i: (b * nq + i, h)),
            pl.BlockSpec((seq, hw), lambda b, h, i: (b, A_HEADS + h)),
            pl.BlockSpec((seq, hw), lambda b, h, i: (b, 2 * A_HEADS + h)),
            pl.BlockSpec((t, hw), lambda b, h, i: (b * nq + i, 3 * A_HEADS + h)),
            small((1, hw)), small((1, hw)), small((4, A_QK_DIM)), small((1, hw)),
        ],
        out_specs=pl.BlockSpec((t, hw), lambda b, h, i: (b * nq + i, h)),
        out_shape=jax.ShapeDtypeStruct((bsz * seq, A_WIDTH), BF16),
        scratch_shapes=[pltpu.VMEM((seq, hw), BF16)],
        compiler_params=pltpu.CompilerParams(
            dimension_semantics=("parallel", "parallel", "arbitrary"),
            vmem_limit_bytes=_vmem_limit(blocks, seq * hw * 2, 12 * t * t * 4)),
        name="diff_attn",
    )(slopes, proj, proj, proj, proj, qg2, kg2, lam_params, out_g.reshape(1, hw))


def _conv_kernel(h_ref, c_ref, hp_ref, cp_ref, b_ref, z_ref, w_ref, bias_ref, o_ref):
    ts = h_ref.shape[0]
    first = pl.program_id(1) == 0
    u = c_ref[...].astype(F32) * h_ref[...].astype(F32)
    up = cp_ref[...].astype(F32) * hp_ref[...].astype(F32)
    up = jnp.where(first, 0.0, up)
    row = lax.broadcasted_iota(jnp.int32, u.shape, 0)
    u1 = jnp.where(row >= 1, pltpu.roll(u, 1, 0), up[7:8, :])
    u2 = jnp.where(row >= 2, pltpu.roll(u, 2, 0),
                   jnp.where(row == 1, up[7:8, :], up[6:7, :]))
    w = w_ref[...]
    y = w[0:1, :] * u2 + w[1:2, :] * u1 + w[2:3, :] * u + bias_ref[...]
    o_ref[...] = (b_ref[...].astype(F32) * y * _silu(z_ref[...].astype(F32))).astype(o_ref.dtype)


def _gated_conv(proj, w, b, bsz, seq):
    ts = CONV_TS
    ns = seq // ts
    cw = B_WIDTH
    base = 4 * A_WIDTH // cw
    halo = 8
    rows = lambda c: pl.BlockSpec((ts, cw), lambda bi, si: (bi * ns + si, base + c))
    prev = lambda c: pl.BlockSpec(
        (halo, cw), lambda bi, si: (jnp.maximum((bi * ns + si) * (ts // halo) - 1, 0), base + c))
    blocks = 5 * ts * cw * 2
    return pl.pallas_call(
        _conv_kernel,
        grid=(bsz, ns),
        in_specs=[rows(0), rows(1), prev(0), prev(1), rows(2), rows(3),
                  pl.BlockSpec((CONV_W, cw), lambda bi, si: (0, 0)),
                  pl.BlockSpec((1, cw), lambda bi, si: (0, 0))],
        out_specs=pl.BlockSpec((ts, cw), lambda bi, si: (bi * ns + si, 0)),
        out_shape=jax.ShapeDtypeStruct((bsz * seq, cw), BF16),
        compiler_params=pltpu.CompilerParams(
            dimension_semantics=("parallel", "parallel"),
            vmem_limit_bytes=_vmem_limit(blocks, 0, 8 * ts * cw * 4)),
        name="gated_conv",
    )(proj, proj, proj, proj, proj, proj, w, b.reshape(1, cw))


def _mem_kv_kernel(mem_ref, g_ref, w_ref, kg_ref, o_ref):
    x = mem_ref[...]
    ms = jnp.mean(x * x, axis=-1, keepdims=True)
    xn = (x * lax.rsqrt(ms + EPS) * g_ref[...]).astype(BF16)
    kv = jnp.dot(xn, w_ref[...], preferred_element_type=F32)
    for hd in range(X_HEADS):
        k = kv[:, hd * X_DIM:(hd + 1) * X_DIM]
        kms = jnp.mean(k * k, axis=-1, keepdims=True)
        o_ref[:, hd * X_DIM:(hd + 1) * X_DIM] = (k * lax.rsqrt(kms + EPS) * kg_ref[...]).astype(BF16)
    o_ref[:, X_WIDTH:] = kv[:, X_WIDTH:].astype(BF16)


def _mem_kv(mem2d, g, w, k_g, bsz):
    d = mem2d.shape[1]
    blocks = MEM_LEN * d * 4 + w.size * 2 + MEM_LEN * 2 * X_WIDTH * 2
    return pl.pallas_call(
        _mem_kv_kernel,
        grid=(bsz,),
        in_specs=[pl.BlockSpec((MEM_LEN, d), lambda b: (b, 0)),
                  pl.BlockSpec((1, d), lambda b: (0, 0)),
                  pl.BlockSpec(w.shape, lambda b: (0, 0)),
                  pl.BlockSpec((1, X_DIM), lambda b: (0, 0))],
        out_specs=pl.BlockSpec((MEM_LEN, 2 * X_WIDTH), lambda b: (b, 0)),
        out_shape=jax.ShapeDtypeStruct((bsz * MEM_LEN, 2 * X_WIDTH), BF16),
        compiler_params=pltpu.CompilerParams(
            dimension_semantics=("parallel",),
            vmem_limit_bytes=_vmem_limit(blocks, 0, 4 * MEM_LEN * d * 4)),
        name="mem_kv",
    )(mem2d, g.reshape(1, d), w, k_g.reshape(1, X_DIM))


def _xattn_kernel(q_ref, z_ref, kv_ref, qg_ref, o_ref):
    for hd in range(X_HEADS):
        cs = slice(hd * X_DIM, (hd + 1) * X_DIM)
        q = q_ref[:, cs].astype(F32)
        ms = jnp.mean(q * q, axis=-1, keepdims=True)
        qn = (q * lax.rsqrt(ms + EPS) * qg_ref[...] * (X_DIM ** -0.5)).astype(BF16)
        s = _nt_dot(qn, kv_ref[:, cs])
        m = jnp.max(s, axis=-1, keepdims=True)
        p = jnp.exp(s - m)
        l = jnp.sum(p, axis=-1, keepdims=True)
        vs = slice(X_WIDTH + hd * X_DIM, X_WIDTH + (hd + 1) * X_DIM)
        o = jnp.dot(p.astype(BF16), kv_ref[:, vs], preferred_element_type=F32) / l
        o_ref[:, cs] = (o * _silu(z_ref[:, cs].astype(F32))).astype(o_ref.dtype)


def _mem_xattn(proj, kv, q_g, q_col_block, bsz, seq):
    tq = X_TQ
    nq = seq // tq
    blocks = 3 * tq * X_WIDTH * 2 + MEM_LEN * 2 * X_WIDTH * 2
    return pl.pallas_call(
        _xattn_kernel,
        grid=(bsz, nq),
        in_specs=[pl.BlockSpec((tq, X_WIDTH), lambda b, i: (b * nq + i, q_col_block)),
                  pl.BlockSpec((tq, X_WIDTH), lambda b, i: (b * nq + i, q_col_block + 1)),
                  pl.BlockSpec((MEM_LEN, 2 * X_WIDTH), lambda b, i: (b, 0)),
                  pl.BlockSpec((1, X_DIM), lambda b, i: (0, 0))],
        out_specs=pl.BlockSpec((tq, X_WIDTH), lambda b, i: (b * nq + i, 0)),
        out_shape=jax.ShapeDtypeStruct((bsz * seq, X_WIDTH), BF16),
        compiler_params=pltpu.CompilerParams(
            dimension_semantics=("parallel", "parallel"),
            vmem_limit_bytes=_vmem_limit(blocks, 0, 8 * tq * MEM_LEN * 4)),
        name="mem_xattn",
    )(proj, proj, kv, q_g.reshape(1, X_DIM))


def _forget_cum_kernel(f_ref, b_ref, col_ref, row_ref):
    x = f_ref[...] + b_ref[...]
    c = jnp.minimum(x, 0.0) - jnp.log1p(jnp.exp(-jnp.abs(x)))
    n = c.shape[0]
    row = lax.broadcasted_iota(jnp.int32, c.shape, 0)
    shift = 1
    while shift < n:
        c = c + jnp.where(row >= shift, pltpu.roll(c, shift, 0), 0.0)
        shift *= 2
    col_ref[...] = c
    row_ref[...] = c.T[:row_ref.shape[0], :]


def _forget_cum(f_logits, f_bias, bsz, seq):
    hp = 16
    bias = jnp.zeros((1, LANES), F32).at[0, :C_HEADS].set(f_bias)
    blocks = 2 * seq * LANES * 4 + hp * seq * 4
    return pl.pallas_call(
        _forget_cum_kernel,
        grid=(bsz,),
        in_specs=[pl.BlockSpec((seq, LANES), lambda b: (b, 0)),
                  pl.BlockSpec((1, LANES), lambda b: (0, 0))],
        out_specs=[pl.BlockSpec((seq, LANES), lambda b: (b, 0)),
                   pl.BlockSpec((hp, seq), lambda b: (b, 0))],
        out_shape=[jax.ShapeDtypeStruct((bsz * seq, LANES), F32),
                   jax.ShapeDtypeStruct((bsz * hp, seq), F32)],
        compiler_params=pltpu.CompilerParams(
            dimension_semantics=("parallel",),
            vmem_limit_bytes=_vmem_limit(blocks, 0, 6 * seq * LANES * 4)),
        name="forget_cum",
    )(f_logits, bias)


def _fox_kernel(q_ref, k_ref, v_ref, z_ref, cc_ref, cr_ref, qg_ref, kg_ref, o_ref, kn_ref):
    t = ATT_T
    h = pl.program_id(1)
    qi = pl.program_id(2)

    @pl.when(qi == 0)
    def _():
        k = k_ref[...].astype(F32)
        ms = jnp.mean(k * k, axis=-1, keepdims=True)
        kn_ref[...] = (k * lax.rsqrt(ms + EPS) * kg_ref[...]).astype(BF16)

    q = q_ref[...].astype(F32)
    ms = jnp.mean(q * q, axis=-1, keepdims=True)
    qn = (q * lax.rsqrt(ms + EPS) * qg_ref[...] * (C_DIM ** -0.5)).astype(BF16)
    lane = lax.broadcasted_iota(jnp.int32, (t, LANES), 1)
    cq = jnp.sum(jnp.where(lane == h, cc_ref[...], 0.0), axis=-1, keepdims=True)
    causal = _causal_mask(t)

    def step(j, carry, masked):
        start = pl.multiple_of(j * t, t)
        kb = kn_ref[pl.ds(start, t), :]
        vb = v_ref[pl.ds(start, t), :]
        ck = cr_ref[:, pl.ds(start, t)]
        s = _nt_dot(qn, kb) + (cq - ck)
        if masked:
            s = jnp.where(causal, s, NEG)
        return _softmax_step(s, vb, *carry)

    init = (jnp.full((t, 1), NEG, F32), jnp.zeros((t, 1), F32), jnp.zeros((t, C_DIM), F32))
    carry = lax.fori_loop(0, qi, lambda j, c: step(j, c, False), init)
    _, l, acc = step(qi, carry, True)
    o_ref[...] = ((acc / l) * _silu(z_ref[...].astype(F32))).astype(o_ref.dtype)


def _fox_attention(proj, cum_col, cum_row, q_g, k_g, bsz, seq):
    t = ATT_T
    nq = seq // t
    hw = C_DIM
    hp = cum_row.shape[0] // bsz
    cum_row3 = cum_row.reshape(bsz * hp, 1, seq)
    blocks = 3 * t * hw * 2 + 2 * seq * hw * 2 + t * LANES * 4 + 8 * seq * 4
    small = lambda shape: pl.BlockSpec(shape, lambda b, h, i: (0, 0))
    return pl.pallas_call(
        _fox_kernel,
        grid=(bsz, C_HEADS, nq),
        in_specs=[
            pl.BlockSpec((t, hw), lambda b, h, i: (b * nq + i, h)),
            pl.BlockSpec((seq, hw), lambda b, h, i: (b, C_HEADS + h)),
            pl.BlockSpec((seq, hw), lambda b, h, i: (b, 2 * C_HEADS + h)),
            pl.BlockSpec((t, hw), lambda b, h, i: (b * nq + i, 3 * C_HEADS + h)),
            pl.BlockSpec((t, LANES), lambda b, h, i: (b * nq + i, 0)),
            pl.BlockSpec((None, 1, seq), lambda b, h, i: (b * hp + h, 0, 0)),
            small((1, hw)), small((1, hw)),
        ],
        out_specs=pl.BlockSpec((t, hw), lambda b, h, i: (b * nq + i, h)),
        out_shape=jax.ShapeDtypeStruct((bsz * seq, C_WIDTH), BF16),
        scratch_shapes=[pltpu.VMEM((seq, hw), BF16)],
        compiler_params=pltpu.CompilerParams(
            dimension_semantics=("parallel", "parallel", "arbitrary"),
            vmem_limit_bytes=_vmem_limit(blocks, seq * hw * 2, 8 * t * t * 4)),
        name="fox_attn",
    )(proj, proj, proj, proj, cum_col, cum_row3, q_g.reshape(1, hw), k_g.reshape(1, hw))


def _even_layer(h, mem2d, layer, bsz, seq, norm_g, w_in, w_out, a_qn, a_kn, a_lam, a_on,
                b_cw, b_cb, x_qn, x_kn, mem_g, w_mem_kv):
    lam_init = 0.8 - 0.6 * math.exp(-0.3 * layer)
    proj = _norm_proj(h, norm_g, w_in.astype(BF16))
    ya = _diff_attention(proj, a_qn, a_kn, a_lam, a_on, lam_init, bsz, seq)
    yb = _gated_conv(proj, b_cw, b_cb, bsz, seq)
    kv = _mem_kv(mem2d, mem_g, w_mem_kv.astype(BF16), x_kn, bsz)
    yx = _mem_xattn(proj, kv, x_qn, (4 * A_WIDTH + 4 * B_WIDTH) // X_WIDTH, bsz, seq)
    return _out_proj([ya, yb, yx], w_out.astype(BF16), h)


def _odd_layer(h, mem2d, bsz, seq, norm_g, w_in, w_out, c_qn, c_kn, c_fb, x_qn, x_kn,
               mem_g, w_mem_kv):
    main = 4 * C_WIDTH
    w_main = jnp.concatenate([w_in[:, :main], w_in[:, main + C_HEADS:]], axis=1).astype(BF16)
    w_f = jnp.pad(w_in[:, main:main + C_HEADS], ((0, 0), (0, LANES - C_HEADS))).astype(BF16)
    proj, f_logits = _norm_proj(h, norm_g, w_main, w_f)
    cum_col, cum_row = _forget_cum(f_logits, c_fb, bsz, seq)
    yc = _fox_attention(proj, cum_col, cum_row, c_qn, c_kn, bsz, seq)
    kv = _mem_kv(mem2d, mem_g, w_mem_kv.astype(BF16), x_kn, bsz)
    yx = _mem_xattn(proj, kv, x_qn, main // X_WIDTH, bsz, seq)
    return _out_proj([yc, yx], w_out.astype(BF16), h)


def kernel(x, mem, e_norm_g, e_w_in, e_w_out, e_a_q_norm_g, e_a_k_norm_g, e_a_lambda,
           e_a_out_norm_g, e_b_conv_w, e_b_conv_b, e_x_q_norm_g, e_x_k_norm_g, e_mem_norm_g,
           e_w_mem_kv, o_norm_g, o_w_in, o_w_out, o_c_q_norm_g, o_c_k_norm_g, o_c_forget_b,
           o_x_q_norm_g, o_x_k_norm_g, o_mem_norm_g, o_w_mem_kv):
    bsz, seq, d = x.shape
    h = x.reshape(bsz * seq, d)
    mem2d = mem.reshape(bsz * MEM_LEN, d)
    depth = e_w_in.shape[0] + o_w_in.shape[0]
    for layer in range(depth):
        i = layer // 2
        if layer % 2 == 0:
            h = _even_layer(h, mem2d, layer, bsz, seq, e_norm_g[i], e_w_in[i], e_w_out[i],
                            e_a_q_norm_g[i], e_a_k_norm_g[i], e_a_lambda[i], e_a_out_norm_g[i],
                            e_b_conv_w[i], e_b_conv_b[i], e_x_q_norm_g[i], e_x_k_norm_g[i],
                            e_mem_norm_g[i], e_w_mem_kv[i])
        else:
            h = _odd_layer(h, mem2d, bsz, seq, o_norm_g[i], o_w_in[i], o_w_out[i],
                           o_c_q_norm_g[i], o_c_k_norm_g[i], o_c_forget_b[i],
                           o_x_q_norm_g[i], o_x_k_norm_g[i], o_mem_norm_g[i], o_w_mem_kv[i])
    return h.reshape(bsz, seq, d)
```

```python
import functools
import math

import numpy as np
import jax
import jax.numpy as jnp
from jax import lax
from jax.experimental import pallas as pl
from jax.experimental.pallas import tpu as pltpu

F32 = jnp.float32
BF16 = jnp.bfloat16

D_MODEL = 1024
MEM_LEN = 256
EPS = 1e-6
NEG = -1e30

A_HEADS = 8
A_QK_DIM = 64
A_V_DIM = 128
A_WIDTH = 1024
B_WIDTH = 512
CONV_W = 3
C_HEADS = 12
C_DIM = 128
C_WIDTH = 1536
X_HEADS = 4
X_DIM = 128
X_WIDTH = 512
MIX = 2048

LANES = 128
VMEM_CAP = 56 * 1024 * 1024

PROJ_TM = 1024
PROJ_TN = 1024
OUT_TM = 512
ATT_T = 256
A_GROUP = 2
C_GROUP = 4
X_TQ = 512
CONV_TS = 512


def _vmem_limit(block_bytes, scratch_bytes=0, temp_bytes=0):
    need = 2 * block_bytes + scratch_bytes + temp_bytes + (4 << 20)
    return int(min(max(need, 16 << 20), VMEM_CAP))


def _silu(z):
    return z * jax.nn.sigmoid(z)


def _nt_dot(a, b):
    return lax.dot_general(a, b, (((1,), (1,)), ((), ())), preferred_element_type=F32)


def _proj_kernel(x_ref, g_ref, w_ref, o_ref, xn_ref):
    @pl.when(pl.program_id(1) == 0)
    def _():
        x = x_ref[...]
        ms = jnp.mean(x * x, axis=-1, keepdims=True)
        xn_ref[...] = (x * lax.rsqrt(ms + EPS) * g_ref[...]).astype(BF16)

    o_ref[...] = jnp.dot(xn_ref[...], w_ref[...], preferred_element_type=F32).astype(o_ref.dtype)


def _proj_side_kernel(x_ref, g_ref, w_ref, ws_ref, o_ref, side_ref, xn_ref):
    @pl.when(pl.program_id(1) == 0)
    def _():
        x = x_ref[...]
        ms = jnp.mean(x * x, axis=-1, keepdims=True)
        xn = (x * lax.rsqrt(ms + EPS) * g_ref[...]).astype(BF16)
        xn_ref[...] = xn
        side_ref[...] = jnp.dot(xn, ws_ref[...], preferred_element_type=F32)

    o_ref[...] = jnp.dot(xn_ref[...], w_ref[...], preferred_element_type=F32).astype(o_ref.dtype)


def _norm_proj(h, g, w, w_side=None):
    t, d = h.shape
    n = w.shape[1]
    tm, tn = PROJ_TM, PROJ_TN
    grid = (t // tm, n // tn)
    blocks = tm * d * 4 + d * tn * 2 + tm * tn * 2
    in_specs = [
        pl.BlockSpec((tm, d), lambda i, j: (i, 0)),
        pl.BlockSpec((1, d), lambda i, j: (0, 0)),
        pl.BlockSpec((d, tn), lambda i, j: (0, j)),
    ]
    out_specs = pl.BlockSpec((tm, tn), lambda i, j: (i, j))
    out_shape = jax.ShapeDtypeStruct((t, n), BF16)
    args = [h, g.reshape(1, d), w]
    body = _proj_kernel
    if w_side is not None:
        ns = w_side.shape[1]
        in_specs.append(pl.BlockSpec((d, ns), lambda i, j: (0, 0)))
        out_specs = [out_specs, pl.BlockSpec((tm, ns), lambda i, j: (i, 0))]
        out_shape = [out_shape, jax.ShapeDtypeStruct((t, ns), F32)]
        args.append(w_side)
        body = _proj_side_kernel
        blocks += d * ns * 2 + tm * ns * 4
    return pl.pallas_call(
        body,
        grid=grid,
        in_specs=in_specs,
        out_specs=out_specs,
        out_shape=out_shape,
        scratch_shapes=[pltpu.VMEM((tm, d), BF16)],
        compiler_params=pltpu.CompilerParams(
            dimension_semantics=("parallel", "arbitrary"),
            vmem_limit_bytes=_vmem_limit(blocks, tm * d * 2, tm * tn * 4 + tm * d * 4)),
        name="norm_proj",
    )(*args)


def _out_kernel(*refs, widths):
    ys = refs[:len(widths)]
    w_ref, h_ref, o_ref = refs[len(widths):]
    acc = h_ref[...]
    off = 0
    for y_ref, width in zip(ys, widths):
        acc = acc + jnp.dot(y_ref[...], w_ref[off:off + width, :], preferred_element_type=F32)
        off += width
    o_ref[...] = acc


def _out_proj(ys, w, h):
    t, d = h.shape
    tm = OUT_TM
    widths = tuple(int(y.shape[1]) for y in ys)
    blocks = sum(tm * wd * 2 for wd in widths) + w.size * 2 + 2 * tm * d * 4
    in_specs = [pl.BlockSpec((tm, wd), lambda i: (i, 0)) for wd in widths]
    in_specs += [pl.BlockSpec(w.shape, lambda i: (0, 0)), pl.BlockSpec((tm, d), lambda i: (i, 0))]
    return pl.pallas_call(
        functools.partial(_out_kernel, widths=widths),
        grid=(t // tm,),
        in_specs=in_specs,
        out_specs=pl.BlockSpec((tm, d), lambda i: (i, 0)),
        out_shape=jax.ShapeDtypeStruct((t, d), F32),
        compiler_params=pltpu.CompilerParams(
            dimension_semantics=("parallel",),
            vmem_limit_bytes=_vmem_limit(blocks, 0, 2 * tm * d * 4)),
        name="out_proj",
    )(*ys, w, h)


def _split3(c):
    c1 = c.astype(BF16).astype(F32)
    r = c - c1
    c2 = r.astype(BF16).astype(F32)
    c3 = (r - c2).astype(BF16).astype(F32)
    return c1, c2, c3


def _bias_lanes(term, key_side):
    lane = lax.broadcasted_iota(jnp.int32, (term.shape[0], LANES), 1)
    c1, c2, c3 = _split3(term)
    own = 0 if key_side else 3
    one = 3 if key_side else 0
    out = jnp.where(lane == own, c1, jnp.where(lane == own + 1, c2, jnp.where(lane == own + 2, c3, 0.0)))
    return jnp.where(lane < one, out, jnp.where(lane < one + 3, 1.0, out))


def _flash_transposed(chains, qi, t):
    krow = lax.broadcasted_iota(jnp.int32, (t, t), 0)
    qcol = lax.broadcasted_iota(jnp.int32, (t, t), 1)

    def step(j, carry, masked):
        start = pl.multiple_of(j * t, t)
        new = []
        for (qa, ka_ref, vt_ref), (m, l, acc) in zip(chains, carry):
            s = _nt_dot(ka_ref[pl.ds(start, t), :], qa)
            if masked:
                s = jnp.where(krow <= qcol, s, NEG)
            m_new = jnp.maximum(m, jnp.max(s, axis=0, keepdims=True))
            alpha = jnp.exp(m - m_new)
            p = jnp.exp(s - m_new)
            l_new = alpha * l + jnp.sum(p, axis=0, keepdims=True)
            pv = jnp.dot(vt_ref[:, pl.ds(start, t)], p.astype(BF16), preferred_element_type=F32)
            new.append((m_new, l_new, alpha * acc + pv))
        return tuple(new)

    init = tuple((jnp.full((1, t), NEG, F32), jnp.zeros((1, t), F32),
                  jnp.zeros((vt_ref.shape[0], t), F32)) for _, _, vt_ref in chains)
    carry = lax.fori_loop(0, qi, lambda j, c: step(j, c, False), init)
    return [(acc, l) for _, l, acc in step(qi, carry, True)]


def _half_rms(x, lo):
    x2 = x * x
    ms_lo = jnp.sum(jnp.where(lo, x2, 0.0), axis=-1, keepdims=True) * (1.0 / A_QK_DIM)
    ms_hi = jnp.sum(jnp.where(lo, 0.0, x2), axis=-1, keepdims=True) * (1.0 / A_QK_DIM)
    return lax.rsqrt(jnp.where(lo, ms_lo, ms_hi) + EPS)


def _diff_kernel(slopes_ref, q_ref, k_ref, v_ref, z_ref, qg_ref, kg_ref, lam_ref, og_ref,
                 o_ref, ka_ref, vt_ref, *, lam_init, group):
    t = ATT_T
    hw = A_V_DIM
    seq = k_ref.shape[0]
    hg = pl.program_id(1)
    qi = pl.program_id(2)

    @pl.when(qi == 0)
    def _():
        lo_k = lax.broadcasted_iota(jnp.int32, (seq, hw), 1) < A_QK_DIM
        k_pos = lax.broadcasted_iota(jnp.int32, (seq, 1), 0).astype(F32)
        for g in range(group):
            cs = slice(g * hw, (g + 1) * hw)
            k = k_ref[:, cs].astype(F32)
            ka_ref[g, :, :hw] = (k * _half_rms(k, lo_k) * kg_ref[...]).astype(BF16)
            ka_ref[g, :, hw:] = _bias_lanes(slopes_ref[hg * group + g] * k_pos, True).astype(BF16)
            vt_ref[g] = v_ref[:, cs].astype(F32).T.astype(BF16)

    lo = lax.broadcasted_iota(jnp.int32, (t, hw), 1) < A_QK_DIM
    q_pos = (qi * t + lax.broadcasted_iota(jnp.int32, (t, 1), 0)).astype(F32)
    chains = []
    for g in range(group):
        cs = slice(g * hw, (g + 1) * hw)
        q = q_ref[:, cs].astype(F32)
        qn = q * _half_rms(q, lo) * qg_ref[...] * (A_QK_DIM ** -0.5)
        qb = _bias_lanes(-slopes_ref[hg * group + g] * q_pos, False)
        q1 = jnp.concatenate([jnp.where(lo, qn, 0.0), qb], axis=1).astype(BF16)
        q2 = jnp.concatenate([jnp.where(lo, 0.0, qn), qb], axis=1).astype(BF16)
        chains += [(q1, ka_ref.at[g], vt_ref.at[g]), (q2, ka_ref.at[g], vt_ref.at[g])]

    outs = _flash_transposed(chains, qi, t)

    lp = lam_ref[...]
    lam = (jnp.exp(jnp.sum(lp[0:1, :] * lp[1:2, :], axis=-1, keepdims=True))
           - jnp.exp(jnp.sum(lp[2:3, :] * lp[3:4, :], axis=-1, keepdims=True)) + lam_init)
    for g in range(group):
        cs = slice(g * hw, (g + 1) * hw)
        (a1, l1), (a2, l2) = outs[2 * g], outs[2 * g + 1]
        o = (a1 / l1 - lam * (a2 / l2)).T
        ms = jnp.mean(o * o, axis=-1, keepdims=True)
        on = o * lax.rsqrt(ms + EPS) * og_ref[...] * (1.0 - lam_init)
        o_ref[:, cs] = (on * _silu(z_ref[:, cs].astype(F32))).astype(o_ref.dtype)


def _diff_attention(proj, q_g, k_g, lam_params, out_g, lam_init, bsz, seq):
    t = ATT_T
    nq = seq // t
    hw = A_V_DIM
    grp = A_GROUP
    gw = grp * hw
    ng = A_HEADS // grp
    slopes = jnp.asarray(np.array([2.0 ** (-8.0 * (i + 1) / A_HEADS) for i in range(A_HEADS)],
                                  dtype=np.float32))
    qg2 = jnp.concatenate([q_g, q_g]).reshape(1, hw)
    kg2 = jnp.concatenate([k_g, k_g]).reshape(1, hw)
    blocks = 3 * t * gw * 2 + 2 * seq * gw * 2
    scratch = grp * seq * 3 * hw * 2
    small = lambda shape: pl.BlockSpec(shape, lambda b, h, i: (0, 0))
    return pl.pallas_call(
        functools.partial(_diff_kernel, lam_init=lam_init, group=grp),
        grid=(bsz, ng, nq),
        in_specs=[
            pl.BlockSpec(memory_space=pltpu.SMEM),
            pl.BlockSpec((t, gw), lambda b, h, i: (b * nq + i, h)),
            pl.BlockSpec((seq, gw), lambda b, h, i: (b, ng + h)),
            pl.BlockSpec((seq, gw), lambda b, h, i: (b, 2 * ng + h)),
            pl.BlockSpec((t, gw), lambda b, h, i: (b * nq + i, 3 * ng + h)),
            small((1, hw)), small((1, hw)), small((4, A_QK_DIM)), small((1, hw)),
        ],
        out_specs=pl.BlockSpec((t, gw), lambda b, h, i: (b * nq + i, h)),
        out_shape=jax.ShapeDtypeStruct((bsz * seq, A_WIDTH), BF16),
        scratch_shapes=[pltpu.VMEM((grp, seq, 2 * hw), BF16), pltpu.VMEM((grp, hw, seq), BF16)],
        compiler_params=pltpu.CompilerParams(
            dimension_semantics=("parallel", "parallel", "arbitrary"),
            vmem_limit_bytes=_vmem_limit(blocks, scratch, 8 * seq * hw * 4)),
        name="diff_attn",
    )(slopes, proj, proj, proj, proj, qg2, kg2, lam_params, out_g.reshape(1, hw))


def _conv_kernel(h_ref, c_ref, hp_ref, cp_ref, b_ref, z_ref, w_ref, bias_ref, o_ref):
    first = pl.program_id(1) == 0
    u = c_ref[...].astype(F32) * h_ref[...].astype(F32)
    up = cp_ref[...].astype(F32) * hp_ref[...].astype(F32)
    up = jnp.where(first, 0.0, up)
    row = lax.broadcasted_iota(jnp.int32, u.shape, 0)
    u1 = jnp.where(row >= 1, pltpu.roll(u, 1, 0), up[7:8, :])
    u2 = jnp.where(row >= 2, pltpu.roll(u, 2, 0),
                   jnp.where(row == 1, up[7:8, :], up[6:7, :]))
    w = w_ref[...]
    y = w[0:1, :] * u2 + w[1:2, :] * u1 + w[2:3, :] * u + bias_ref[...]
    o_ref[...] = (b_ref[...].astype(F32) * y * _silu(z_ref[...].astype(F32))).astype(o_ref.dtype)


def _gated_conv(proj, w, b, bsz, seq):
    ts = CONV_TS
    ns = seq // ts
    cw = B_WIDTH
    base = 4 * A_WIDTH // cw
    halo = 8
    rows = lambda c: pl.BlockSpec((ts, cw), lambda bi, si: (bi * ns + si, base + c))
    prev = lambda c: pl.BlockSpec(
        (halo, cw), lambda bi, si: (jnp.maximum((bi * ns + si) * (ts // halo) - 1, 0), base + c))
    blocks = 5 * ts * cw * 2
    return pl.pallas_call(
        _conv_kernel,
        grid=(bsz, ns),
        in_specs=[rows(0), rows(1), prev(0), prev(1), rows(2), rows(3),
                  pl.BlockSpec((CONV_W, cw), lambda bi, si: (0, 0)),
                  pl.BlockSpec((1, cw), lambda bi, si: (0, 0))],
        out_specs=pl.BlockSpec((ts, cw), lambda bi, si: (bi * ns + si, 0)),
        out_shape=jax.ShapeDtypeStruct((bsz * seq, cw), BF16),
        compiler_params=pltpu.CompilerParams(
            dimension_semantics=("parallel", "parallel"),
            vmem_limit_bytes=_vmem_limit(blocks, 0, 8 * ts * cw * 4)),
        name="gated_conv",
    )(proj, proj, proj, proj, proj, proj, w, b.reshape(1, cw))


def _mem_kv_kernel(mem_ref, g_ref, w_ref, kg_ref, o_ref):
    x = mem_ref[...]
    ms = jnp.mean(x * x, axis=-1, keepdims=True)
    xn = (x * lax.rsqrt(ms + EPS) * g_ref[...]).astype(BF16)
    kv = jnp.dot(xn, w_ref[...], preferred_element_type=F32)
    for hd in range(X_HEADS):
        k = kv[:, hd * X_DIM:(hd + 1) * X_DIM]
        kms = jnp.mean(k * k, axis=-1, keepdims=True)
        o_ref[:, hd * X_DIM:(hd + 1) * X_DIM] = (k * lax.rsqrt(kms + EPS) * kg_ref[...]).astype(BF16)
    o_ref[:, X_WIDTH:] = kv[:, X_WIDTH:].astype(BF16)


def _mem_kv(mem2d, g, w, k_g, bsz):
    d = mem2d.shape[1]
    blocks = MEM_LEN * d * 4 + w.size * 2 + MEM_LEN * 2 * X_WIDTH * 2
    return pl.pallas_call(
        _mem_kv_kernel,
        grid=(bsz,),
        in_specs=[pl.BlockSpec((MEM_LEN, d), lambda b: (b, 0)),
                  pl.BlockSpec((1, d), lambda b: (0, 0)),
                  pl.BlockSpec(w.shape, lambda b: (0, 0)),
                  pl.BlockSpec((1, X_DIM), lambda b: (0, 0))],
        out_specs=pl.BlockSpec((MEM_LEN, 2 * X_WIDTH), lambda b: (b, 0)),
        out_shape=jax.ShapeDtypeStruct((bsz * MEM_LEN, 2 * X_WIDTH), BF16),
        compiler_params=pltpu.CompilerParams(
            dimension_semantics=("parallel",),
            vmem_limit_bytes=_vmem_limit(blocks, 0, 4 * MEM_LEN * d * 4)),
        name="mem_kv",
    )(mem2d, g.reshape(1, d), w, k_g.reshape(1, X_DIM))


def _xattn_kernel(q_ref, z_ref, kv_ref, qg_ref, o_ref):
    for hd in range(X_HEADS):
        cs = slice(hd * X_DIM, (hd + 1) * X_DIM)
        q = q_ref[:, cs].astype(F32)
        ms = jnp.mean(q * q, axis=-1, keepdims=True)
        qn = (q * lax.rsqrt(ms + EPS) * qg_ref[...] * (X_DIM ** -0.5)).astype(BF16)
        s = _nt_dot(qn, kv_ref[:, cs])
        m = jnp.max(s, axis=-1, keepdims=True)
        p = jnp.exp(s - m)
        l = jnp.sum(p, axis=-1, keepdims=True)
        vs = slice(X_WIDTH + hd * X_DIM, X_WIDTH + (hd + 1) * X_DIM)
        o = jnp.dot(p.astype(BF16), kv_ref[:, vs], preferred_element_type=F32) / l
        o_ref[:, cs] = (o * _silu(z_ref[:, cs].astype(F32))).astype(o_ref.dtype)


def _mem_xattn(proj, kv, q_g, q_col_block, bsz, seq):
    tq = X_TQ
    nq = seq // tq
    blocks = 3 * tq * X_WIDTH * 2 + MEM_LEN * 2 * X_WIDTH * 2
    return pl.pallas_call(
        _xattn_kernel,
        grid=(bsz, nq),
        in_specs=[pl.BlockSpec((tq, X_WIDTH), lambda b, i: (b * nq + i, q_col_block)),
                  pl.BlockSpec((tq, X_WIDTH), lambda b, i: (b * nq + i, q_col_block + 1)),
                  pl.BlockSpec((MEM_LEN, 2 * X_WIDTH), lambda b, i: (b, 0)),
                  pl.BlockSpec((1, X_DIM), lambda b, i: (0, 0))],
        out_specs=pl.BlockSpec((tq, X_WIDTH), lambda b, i: (b * nq + i, 0)),
        out_shape=jax.ShapeDtypeStruct((bsz * seq, X_WIDTH), BF16),
        compiler_params=pltpu.CompilerParams(
            dimension_semantics=("parallel", "parallel"),
            vmem_limit_bytes=_vmem_limit(blocks, 0, 8 * tq * MEM_LEN * 4)),
        name="mem_xattn",
    )(proj, proj, kv, q_g.reshape(1, X_DIM))


def _forget_cum_kernel(f_ref, b_ref, cum_ref):
    x = f_ref[...] + b_ref[...]
    c = jnp.minimum(x, 0.0) - jnp.log1p(jnp.exp(-jnp.abs(x)))
    n = c.shape[0]
    row = lax.broadcasted_iota(jnp.int32, c.shape, 0)
    shift = 1
    while shift < n:
        c = c + jnp.where(row >= shift, pltpu.roll(c, shift, 0), 0.0)
        shift *= 2
    cum_ref[...] = c


def _forget_cum(f_logits, f_bias, bsz, seq):
    bias = jnp.zeros((1, LANES), F32).at[0, :C_HEADS].set(f_bias)
    blocks = 2 * seq * LANES * 4
    return pl.pallas_call(
        _forget_cum_kernel,
        grid=(bsz,),
        in_specs=[pl.BlockSpec((seq, LANES), lambda b: (b, 0)),
                  pl.BlockSpec((1, LANES), lambda b: (0, 0))],
        out_specs=pl.BlockSpec((seq, LANES), lambda b: (b, 0)),
        out_shape=jax.ShapeDtypeStruct((bsz * seq, LANES), F32),
        compiler_params=pltpu.CompilerParams(
            dimension_semantics=("parallel",),
            vmem_limit_bytes=_vmem_limit(blocks, 0, 6 * seq * LANES * 4)),
        name="forget_cum",
    )(f_logits, bias)


def _fox_kernel(q_ref, k_ref, v_ref, z_ref, cc_ref, qg_ref, kg_ref, o_ref, ka_ref, vt_ref,
                *, group):
    t = ATT_T
    hw = C_DIM
    seq = k_ref.shape[0]
    hg = pl.program_id(1)
    qi = pl.program_id(2)

    def head_column(tile, head):
        lane = lax.broadcasted_iota(jnp.int32, tile.shape, 1)
        return jnp.sum(jnp.where(lane == head, tile, 0.0), axis=-1, keepdims=True)

    @pl.when(qi == 0)
    def _():
        for g in range(group):
            cs = slice(g * hw, (g + 1) * hw)
            k = k_ref[:, cs].astype(F32)
            ms = jnp.mean(k * k, axis=-1, keepdims=True)
            ka_ref[g, :, :hw] = (k * lax.rsqrt(ms + EPS) * kg_ref[...]).astype(BF16)
            ck = head_column(cc_ref[...], hg * group + g)
            ka_ref[g, :, hw:] = _bias_lanes(-ck, True).astype(BF16)
            vt_ref[g] = v_ref[:, cs].astype(F32).T.astype(BF16)

    cc_q = cc_ref[pl.ds(pl.multiple_of(qi * t, t), t), :]
    chains = []
    for g in range(group):
        cs = slice(g * hw, (g + 1) * hw)
        q = q_ref[:, cs].astype(F32)
        ms = jnp.mean(q * q, axis=-1, keepdims=True)
        qn = q * lax.rsqrt(ms + EPS) * qg_ref[...] * (C_DIM ** -0.5)
        qb = _bias_lanes(head_column(cc_q, hg * group + g), False)
        chains.append((jnp.concatenate([qn, qb], axis=1).astype(BF16), ka_ref.at[g], vt_ref.at[g]))

    outs = _flash_transposed(chains, qi, t)
    for g, (acc, l) in enumerate(outs):
        cs = slice(g * hw, (g + 1) * hw)
        o_ref[:, cs] = ((acc / l).T * _silu(z_ref[:, cs].astype(F32))).astype(o_ref.dtype)


def _fox_attention(proj, cum, q_g, k_g, bsz, seq):
    t = ATT_T
    nq = seq // t
    hw = C_DIM
    grp = C_GROUP
    gw = grp * hw
    ng = C_HEADS // grp
    blocks = 3 * t * gw * 2 + 2 * seq * gw * 2 + seq * LANES * 4
    scratch = grp * seq * 3 * hw * 2
    small = lambda shape: pl.BlockSpec(shape, lambda b, h, i: (0, 0))
    return pl.pallas_call(
        functools.partial(_fox_kernel, group=grp),
        grid=(bsz, ng, nq),
        in_specs=[
            pl.BlockSpec((t, gw), lambda b, h, i: (b * nq + i, h)),
            pl.BlockSpec((seq, gw), lambda b, h, i: (b, ng + h)),
            pl.BlockSpec((seq, gw), lambda b, h, i: (b, 2 * ng + h)),
            pl.BlockSpec((t, gw), lambda b, h, i: (b * nq + i, 3 * ng + h)),
            pl.BlockSpec((seq, LANES), lambda b, h, i: (b, 0)),
            small((1, hw)), small((1, hw)),
        ],
        out_specs=pl.BlockSpec((t, gw), lambda b, h, i: (b * nq + i, h)),
        out_shape=jax.ShapeDtypeStruct((bsz * seq, C_WIDTH), BF16),
        scratch_shapes=[pltpu.VMEM((grp, seq, 2 * hw), BF16), pltpu.VMEM((grp, hw, seq), BF16)],
        compiler_params=pltpu.CompilerParams(
            dimension_semantics=("parallel", "parallel", "arbitrary"),
            vmem_limit_bytes=_vmem_limit(blocks, scratch, 8 * seq * hw * 4)),
        name="fox_attn",
    )(proj, proj, proj, proj, cum, q_g.reshape(1, hw), k_g.reshape(1, hw))


def _even_layer(h, mem2d, layer, bsz, seq, norm_g, w_in, w_out, a_qn, a_kn, a_lam, a_on,
                b_cw, b_cb, x_qn, x_kn, mem_g, w_mem_kv):
    lam_init = 0.8 - 0.6 * math.exp(-0.3 * layer)
    proj = _norm_proj(h, norm_g, w_in.astype(BF16))
    ya = _diff_attention(proj, a_qn, a_kn, a_lam, a_on, lam_init, bsz, seq)
    yb = _gated_conv(proj, b_cw, b_cb, bsz, seq)
    kv = _mem_kv(mem2d, mem_g, w_mem_kv.astype(BF16), x_kn, bsz)
    yx = _mem_xattn(proj, kv, x_qn, (4 * A_WIDTH + 4 * B_WIDTH) // X_WIDTH, bsz, seq)
    return _out_proj([ya, yb, yx], w_out.astype(BF16), h)


def _odd_layer(h, mem2d, bsz, seq, norm_g, w_in, w_out, c_qn, c_kn, c_fb, x_qn, x_kn,
               mem_g, w_mem_kv):
    main = 4 * C_WIDTH
    w_main = jnp.concatenate([w_in[:, :main], w_in[:, main + C_HEADS:]], axis=1).astype(BF16)
    w_f = jnp.pad(w_in[:, main:main + C_HEADS], ((0, 0), (0, LANES - C_HEADS))).astype(BF16)
    proj, f_logits = _norm_proj(h, norm_g, w_main, w_f)
    cum = _forget_cum(f_logits, c_fb, bsz, seq)
    yc = _fox_attention(proj, cum, c_qn, c_kn, bsz, seq)
    kv = _mem_kv(mem2d, mem_g, w_mem_kv.astype(BF16), x_kn, bsz)
    yx = _mem_xattn(proj, kv, x_qn, main // X_WIDTH, bsz, seq)
    return _out_proj([yc, yx], w_out.astype(BF16), h)


def kernel(x, mem, e_norm_g, e_w_in, e_w_out, e_a_q_norm_g, e_a_k_norm_g, e_a_lambda,
           e_a_out_norm_g, e_b_conv_w, e_b_conv_b, e_x_q_norm_g, e_x_k_norm_g, e_mem_norm_g,
           e_w_mem_kv, o_norm_g, o_w_in, o_w_out, o_c_q_norm_g, o_c_k_norm_g, o_c_forget_b,
           o_x_q_norm_g, o_x_k_norm_g, o_mem_norm_g, o_w_mem_kv):
    bsz, seq, d = x.shape
    h = x.reshape(bsz * seq, d)
    mem2d = mem.reshape(bsz * MEM_LEN, d)
    depth = e_w_in.shape[0] + o_w_in.shape[0]
    for layer in range(depth):
        i = layer // 2
        if layer % 2 == 0:
            h = _even_layer(h, mem2d, layer, bsz, seq, e_norm_g[i], e_w_in[i], e_w_out[i],
                            e_a_q_norm_g[i], e_a_k_norm_g[i], e_a_lambda[i], e_a_out_norm_g[i],
                            e_b_conv_w[i], e_b_conv_b[i], e_x_q_norm_g[i], e_x_k_norm_g[i],
                            e_mem_norm_g[i], e_w_mem_kv[i])
        else:
            h = _odd_layer(h, mem2d, bsz, seq, o_norm_g[i], o_w_in[i], o_w_out[i],
                           o_c_q_norm_g[i], o_c_k_norm_g[i], o_c_forget_b[i],
                           o_x_q_norm_g[i], o_x_k_norm_g[i], o_mem_norm_g[i], o_w_mem_kv[i])
    return h.reshape(bsz, seq, d)
```

```python
import functools
import math

import numpy as np
import jax
import jax.numpy as jnp
from jax import lax
from jax.experimental import pallas as pl
from jax.experimental.pallas import tpu as pltpu

F32 = jnp.float32
BF16 = jnp.bfloat16

D_MODEL = 1024
MEM_LEN = 256
EPS = 1e-6
NEG = -1e30

A_HEADS = 8
A_QK_DIM = 64
A_V_DIM = 128
A_WIDTH = 1024
B_WIDTH = 512
CONV_W = 3
C_HEADS = 12
C_DIM = 128
C_WIDTH = 1536
X_HEADS = 4
X_DIM = 128
X_WIDTH = 512
MIX = 2048

LANES = 128
VMEM_CAP = 56 * 1024 * 1024

PROJ_TM = 1024
PROJ_TN = 1024
OUT_TM = 512
ATT_T = 256
A_GROUP = 2
C_GROUP = 4
X_TQ = 512
CONV_TS = 512


def _vmem_limit(block_bytes, scratch_bytes=0, temp_bytes=0):
    need = 2 * block_bytes + scratch_bytes + temp_bytes + (4 << 20)
    return int(min(max(need, 16 << 20), VMEM_CAP))


def _silu(z):
    return z * jax.nn.sigmoid(z)


def _nt_dot(a, b):
    return lax.dot_general(a, b, (((1,), (1,)), ((), ())), preferred_element_type=F32)


def _proj_kernel(x_ref, g_ref, w_ref, o_ref, xn_ref):
    @pl.when(pl.program_id(1) == 0)
    def _():
        x = x_ref[...]
        ms = jnp.mean(x * x, axis=-1, keepdims=True)
        xn_ref[...] = (x * lax.rsqrt(ms + EPS) * g_ref[...]).astype(BF16)

    o_ref[...] = jnp.dot(xn_ref[...], w_ref[...], preferred_element_type=F32).astype(o_ref.dtype)


def _proj_side_kernel(x_ref, g_ref, w_ref, ws_ref, o_ref, side_ref, xn_ref):
    @pl.when(pl.program_id(1) == 0)
    def _():
        x = x_ref[...]
        ms = jnp.mean(x * x, axis=-1, keepdims=True)
        xn = (x * lax.rsqrt(ms + EPS) * g_ref[...]).astype(BF16)
        xn_ref[...] = xn
        side_ref[...] = jnp.dot(xn, ws_ref[...], preferred_element_type=F32)

    o_ref[...] = jnp.dot(xn_ref[...], w_ref[...], preferred_element_type=F32).astype(o_ref.dtype)


def _norm_proj(h, g, w, w_side=None):
    t, d = h.shape
    n = w.shape[1]
    tm, tn = PROJ_TM, PROJ_TN
    grid = (t // tm, n // tn)
    blocks = tm * d * 4 + d * tn * 2 + tm * tn * 2
    in_specs = [
        pl.BlockSpec((tm, d), lambda i, j: (i, 0)),
        pl.BlockSpec((1, d), lambda i, j: (0, 0)),
        pl.BlockSpec((d, tn), lambda i, j: (0, j)),
    ]
    out_specs = pl.BlockSpec((tm, tn), lambda i, j: (i, j))
    out_shape = jax.ShapeDtypeStruct((t, n), BF16)
    args = [h, g.reshape(1, d), w]
    body = _proj_kernel
    if w_side is not None:
        ns = w_side.shape[1]
        in_specs.append(pl.BlockSpec((d, ns), lambda i, j: (0, 0)))
        out_specs = [out_specs, pl.BlockSpec((tm, ns), lambda i, j: (i, 0))]
        out_shape = [out_shape, jax.ShapeDtypeStruct((t, ns), F32)]
        args.append(w_side)
        body = _proj_side_kernel
        blocks += d * ns * 2 + tm * ns * 4
    return pl.pallas_call(
        body,
        grid=grid,
        in_specs=in_specs,
        out_specs=out_specs,
        out_shape=out_shape,
        scratch_shapes=[pltpu.VMEM((tm, d), BF16)],
        compiler_params=pltpu.CompilerParams(
            dimension_semantics=("parallel", "arbitrary"),
            vmem_limit_bytes=_vmem_limit(blocks, tm * d * 2, tm * tn * 4 + tm * d * 4)),
        name="norm_proj",
    )(*args)


def _out_kernel(*refs, widths):
    ys = refs[:len(widths)]
    w_ref, h_ref, o_ref = refs[len(widths):]
    acc = h_ref[...]
    off = 0
    for y_ref, width in zip(ys, widths):
        acc = acc + jnp.dot(y_ref[...], w_ref[off:off + width, :], preferred_element_type=F32)
        off += width
    o_ref[...] = acc


def _out_proj(ys, w, h):
    t, d = h.shape
    tm = OUT_TM
    widths = tuple(int(y.shape[1]) for y in ys)
    blocks = sum(tm * wd * 2 for wd in widths) + w.size * 2 + 2 * tm * d * 4
    in_specs = [pl.BlockSpec((tm, wd), lambda i: (i, 0)) for wd in widths]
    in_specs += [pl.BlockSpec(w.shape, lambda i: (0, 0)), pl.BlockSpec((tm, d), lambda i: (i, 0))]
    return pl.pallas_call(
        functools.partial(_out_kernel, widths=widths),
        grid=(t // tm,),
        in_specs=in_specs,
        out_specs=pl.BlockSpec((tm, d), lambda i: (i, 0)),
        out_shape=jax.ShapeDtypeStruct((t, d), F32),
        compiler_params=pltpu.CompilerParams(
            dimension_semantics=("parallel",),
            vmem_limit_bytes=_vmem_limit(blocks, 0, 2 * tm * d * 4)),
        name="out_proj",
    )(*ys, w, h)


def _split3(c):
    c1 = c.astype(BF16).astype(F32)
    r = c - c1
    c2 = r.astype(BF16).astype(F32)
    c3 = (r - c2).astype(BF16).astype(F32)
    return c1, c2, c3


def _bias_lanes(term, key_side):
    lane = lax.broadcasted_iota(jnp.int32, (term.shape[0], LANES), 1)
    c1, c2, c3 = _split3(term)
    own = 0 if key_side else 3
    one = 3 if key_side else 0
    out = jnp.where(lane == own, c1, jnp.where(lane == own + 1, c2, jnp.where(lane == own + 2, c3, 0.0)))
    return jnp.where(lane < one, out, jnp.where(lane < one + 3, 1.0, out))


def _attend_tile(chains, c, t):
    krow = lax.broadcasted_iota(jnp.int32, (t, t), 0)
    qcol = lax.broadcasted_iota(jnp.int32, (t, t), 1)
    lo, hi = c * t, (c + 1) * t
    scores = []
    for qa, ka_ref, _ in chains:
        s_d = jnp.where(krow <= qcol, _nt_dot(ka_ref[lo:hi, :], qa), NEG)
        m = jnp.max(s_d, axis=0, keepdims=True)
        s_f = None
        if c > 0:
            s_f = _nt_dot(ka_ref[0:lo, :], qa)
            m = jnp.maximum(m, jnp.max(s_f, axis=0, keepdims=True))
        scores.append((s_d, s_f, m))
    outs = []
    for (_, _, vt_ref), (s_d, s_f, m) in zip(chains, scores):
        p_d = jnp.exp(s_d - m)
        l = jnp.sum(p_d, axis=0, keepdims=True)
        acc = jnp.dot(vt_ref[:, lo:hi], p_d.astype(BF16), preferred_element_type=F32)
        if c > 0:
            p_f = jnp.exp(s_f - m)
            l = l + jnp.sum(p_f, axis=0, keepdims=True)
            acc = acc + jnp.dot(vt_ref[:, 0:lo], p_f.astype(BF16), preferred_element_type=F32)
        outs.append((acc, l))
    return outs


def _for_each_tile(qi, n_tiles, body):
    for c in range(n_tiles):
        pl.when(qi == c)(functools.partial(body, c))


def _half_rms(x, lo):
    x2 = x * x
    ms_lo = jnp.sum(jnp.where(lo, x2, 0.0), axis=-1, keepdims=True) * (1.0 / A_QK_DIM)
    ms_hi = jnp.sum(jnp.where(lo, 0.0, x2), axis=-1, keepdims=True) * (1.0 / A_QK_DIM)
    return lax.rsqrt(jnp.where(lo, ms_lo, ms_hi) + EPS)


def _diff_kernel(slopes_ref, q_ref, k_ref, v_ref, z_ref, qg_ref, kg_ref, lam_ref, og_ref,
                 o_ref, ka_ref, vt_ref, *, lam_init, group):
    t = ATT_T
    hw = A_V_DIM
    seq = k_ref.shape[0]
    hg = pl.program_id(1)
    qi = pl.program_id(2)

    @pl.when(qi == 0)
    def _():
        lo_k = lax.broadcasted_iota(jnp.int32, (seq, hw), 1) < A_QK_DIM
        k_pos = lax.broadcasted_iota(jnp.int32, (seq, 1), 0).astype(F32)
        for g in range(group):
            cs = slice(g * hw, (g + 1) * hw)
            k = k_ref[:, cs].astype(F32)
            ka_ref[g, :, :hw] = (k * _half_rms(k, lo_k) * kg_ref[...]).astype(BF16)
            ka_ref[g, :, hw:] = _bias_lanes(slopes_ref[hg * group + g] * k_pos, True).astype(BF16)
            vt_ref[g] = v_ref[:, cs].astype(F32).T.astype(BF16)

    lo = lax.broadcasted_iota(jnp.int32, (t, hw), 1) < A_QK_DIM
    q_pos = (qi * t + lax.broadcasted_iota(jnp.int32, (t, 1), 0)).astype(F32)
    chains = []
    for g in range(group):
        cs = slice(g * hw, (g + 1) * hw)
        q = q_ref[:, cs].astype(F32)
        qn = q * _half_rms(q, lo) * qg_ref[...] * (A_QK_DIM ** -0.5)
        qb = _bias_lanes(-slopes_ref[hg * group + g] * q_pos, False)
        q1 = jnp.concatenate([jnp.where(lo, qn, 0.0), qb], axis=1).astype(BF16)
        q2 = jnp.concatenate([jnp.where(lo, 0.0, qn), qb], axis=1).astype(BF16)
        chains += [(q1, ka_ref.at[g], vt_ref.at[g]), (q2, ka_ref.at[g], vt_ref.at[g])]

    lp = lam_ref[...]
    lam = (jnp.exp(jnp.sum(lp[0:1, :] * lp[1:2, :], axis=-1, keepdims=True))
           - jnp.exp(jnp.sum(lp[2:3, :] * lp[3:4, :], axis=-1, keepdims=True)) + lam_init)

    def tile(c):
        outs = _attend_tile(chains, c, t)
        for g in range(group):
            cs = slice(g * hw, (g + 1) * hw)
            (a1, l1), (a2, l2) = outs[2 * g], outs[2 * g + 1]
            o = (a1 / l1 - lam * (a2 / l2)).T
            ms = jnp.mean(o * o, axis=-1, keepdims=True)
            on = o * lax.rsqrt(ms + EPS) * og_ref[...] * (1.0 - lam_init)
            o_ref[:, cs] = (on * _silu(z_ref[:, cs].astype(F32))).astype(o_ref.dtype)

    _for_each_tile(qi, seq // t, tile)


def _diff_attention(proj, q_g, k_g, lam_params, out_g, lam_init, bsz, seq):
    t = ATT_T
    nq = seq // t
    hw = A_V_DIM
    grp = A_GROUP
    gw = grp * hw
    ng = A_HEADS // grp
    slopes = jnp.asarray(np.array([2.0 ** (-8.0 * (i + 1) / A_HEADS) for i in range(A_HEADS)],
                                  dtype=np.float32))
    qg2 = jnp.concatenate([q_g, q_g]).reshape(1, hw)
    kg2 = jnp.concatenate([k_g, k_g]).reshape(1, hw)
    blocks = 3 * t * gw * 2 + 2 * seq * gw * 2
    scratch = grp * seq * 3 * hw * 2
    small = lambda shape: pl.BlockSpec(shape, lambda b, h, i: (0, 0))
    return pl.pallas_call(
        functools.partial(_diff_kernel, lam_init=lam_init, group=grp),
        grid=(bsz, ng, nq),
        in_specs=[
            pl.BlockSpec(memory_space=pltpu.SMEM),
            pl.BlockSpec((t, gw), lambda b, h, i: (b * nq + i, h)),
            pl.BlockSpec((seq, gw), lambda b, h, i: (b, ng + h)),
            pl.BlockSpec((seq, gw), lambda b, h, i: (b, 2 * ng + h)),
            pl.BlockSpec((t, gw), lambda b, h, i: (b * nq + i, 3 * ng + h)),
            small((1, hw)), small((1, hw)), small((4, A_QK_DIM)), small((1, hw)),
        ],
        out_specs=pl.BlockSpec((t, gw), lambda b, h, i: (b * nq + i, h)),
        out_shape=jax.ShapeDtypeStruct((bsz * seq, A_WIDTH), BF16),
        scratch_shapes=[pltpu.VMEM((grp, seq, 2 * hw), BF16), pltpu.VMEM((grp, hw, seq), BF16)],
        compiler_params=pltpu.CompilerParams(
            dimension_semantics=("parallel", "parallel", "arbitrary"),
            vmem_limit_bytes=_vmem_limit(blocks, scratch, 8 * seq * hw * 4)),
        name="diff_attn",
    )(slopes, proj, proj, proj, proj, qg2, kg2, lam_params, out_g.reshape(1, hw))


def _conv_kernel(h_ref, c_ref, hp_ref, cp_ref, b_ref, z_ref, w_ref, bias_ref, o_ref):
    first = pl.program_id(1) == 0
    u = c_ref[...].astype(F32) * h_ref[...].astype(F32)
    up = cp_ref[...].astype(F32) * hp_ref[...].astype(F32)
    up = jnp.where(first, 0.0, up)
    row = lax.broadcasted_iota(jnp.int32, u.shape, 0)
    u1 = jnp.where(row >= 1, pltpu.roll(u, 1, 0), up[7:8, :])
    u2 = jnp.where(row >= 2, pltpu.roll(u, 2, 0),
                   jnp.where(row == 1, up[7:8, :], up[6:7, :]))
    w = w_ref[...]
    y = w[0:1, :] * u2 + w[1:2, :] * u1 + w[2:3, :] * u + bias_ref[...]
    o_ref[...] = (b_ref[...].astype(F32) * y * _silu(z_ref[...].astype(F32))).astype(o_ref.dtype)


def _gated_conv(proj, w, b, bsz, seq):
    ts = CONV_TS
    ns = seq // ts
    cw = B_WIDTH
    base = 4 * A_WIDTH // cw
    halo = 8
    rows = lambda c: pl.BlockSpec((ts, cw), lambda bi, si: (bi * ns + si, base + c))
    prev = lambda c: pl.BlockSpec(
        (halo, cw), lambda bi, si: (jnp.maximum((bi * ns + si) * (ts // halo) - 1, 0), base + c))
    blocks = 5 * ts * cw * 2
    return pl.pallas_call(
        _conv_kernel,
        grid=(bsz, ns),
        in_specs=[rows(0), rows(1), prev(0), prev(1), rows(2), rows(3),
                  pl.BlockSpec((CONV_W, cw), lambda bi, si: (0, 0)),
                  pl.BlockSpec((1, cw), lambda bi, si: (0, 0))],
        out_specs=pl.BlockSpec((ts, cw), lambda bi, si: (bi * ns + si, 0)),
        out_shape=jax.ShapeDtypeStruct((bsz * seq, cw), BF16),
        compiler_params=pltpu.CompilerParams(
            dimension_semantics=("parallel", "parallel"),
            vmem_limit_bytes=_vmem_limit(blocks, 0, 8 * ts * cw * 4)),
        name="gated_conv",
    )(proj, proj, proj, proj, proj, proj, w, b.reshape(1, cw))


def _mem_kv_kernel(mem_ref, g_ref, w_ref, kg_ref, o_ref):
    x = mem_ref[...]
    ms = jnp.mean(x * x, axis=-1, keepdims=True)
    xn = (x * lax.rsqrt(ms + EPS) * g_ref[...]).astype(BF16)
    kv = jnp.dot(xn, w_ref[...], preferred_element_type=F32)
    for hd in range(X_HEADS):
        k = kv[:, hd * X_DIM:(hd + 1) * X_DIM]
        kms = jnp.mean(k * k, axis=-1, keepdims=True)
        o_ref[:, hd * X_DIM:(hd + 1) * X_DIM] = (k * lax.rsqrt(kms + EPS) * kg_ref[...]).astype(BF16)
    o_ref[:, X_WIDTH:] = kv[:, X_WIDTH:].astype(BF16)


def _mem_kv(mem2d, g, w, k_g, bsz):
    d = mem2d.shape[1]
    blocks = MEM_LEN * d * 4 + w.size * 2 + MEM_LEN * 2 * X_WIDTH * 2
    return pl.pallas_call(
        _mem_kv_kernel,
        grid=(bsz,),
        in_specs=[pl.BlockSpec((MEM_LEN, d), lambda b: (b, 0)),
                  pl.BlockSpec((1, d), lambda b: (0, 0)),
                  pl.BlockSpec(w.shape, lambda b: (0, 0)),
                  pl.BlockSpec((1, X_DIM), lambda b: (0, 0))],
        out_specs=pl.BlockSpec((MEM_LEN, 2 * X_WIDTH), lambda b: (b, 0)),
        out_shape=jax.ShapeDtypeStruct((bsz * MEM_LEN, 2 * X_WIDTH), BF16),
        compiler_params=pltpu.CompilerParams(
            dimension_semantics=("parallel",),
            vmem_limit_bytes=_vmem_limit(blocks, 0, 4 * MEM_LEN * d * 4)),
        name="mem_kv",
    )(mem2d, g.reshape(1, d), w, k_g.reshape(1, X_DIM))


def _xattn_kernel(q_ref, z_ref, kv_ref, qg_ref, o_ref):
    for hd in range(X_HEADS):
        cs = slice(hd * X_DIM, (hd + 1) * X_DIM)
        q = q_ref[:, cs].astype(F32)
        ms = jnp.mean(q * q, axis=-1, keepdims=True)
        qn = (q * lax.rsqrt(ms + EPS) * qg_ref[...] * (X_DIM ** -0.5)).astype(BF16)
        s = _nt_dot(qn, kv_ref[:, cs])
        m = jnp.max(s, axis=-1, keepdims=True)
        p = jnp.exp(s - m)
        l = jnp.sum(p, axis=-1, keepdims=True)
        vs = slice(X_WIDTH + hd * X_DIM, X_WIDTH + (hd + 1) * X_DIM)
        o = jnp.dot(p.astype(BF16), kv_ref[:, vs], preferred_element_type=F32) / l
        o_ref[:, cs] = (o * _silu(z_ref[:, cs].astype(F32))).astype(o_ref.dtype)


def _mem_xattn(proj, kv, q_g, q_col_block, bsz, seq):
    tq = X_TQ
    nq = seq // tq
    blocks = 3 * tq * X_WIDTH * 2 + MEM_LEN * 2 * X_WIDTH * 2
    return pl.pallas_call(
        _xattn_kernel,
        grid=(bsz, nq),
        in_specs=[pl.BlockSpec((tq, X_WIDTH), lambda b, i: (b * nq + i, q_col_block)),
                  pl.BlockSpec((tq, X_WIDTH), lambda b, i: (b * nq + i, q_col_block + 1)),
                  pl.BlockSpec((MEM_LEN, 2 * X_WIDTH), lambda b, i: (b, 0)),
                  pl.BlockSpec((1, X_DIM), lambda b, i: (0, 0))],
        out_specs=pl.BlockSpec((tq, X_WIDTH), lambda b, i: (b * nq + i, 0)),
        out_shape=jax.ShapeDtypeStruct((bsz * seq, X_WIDTH), BF16),
        compiler_params=pltpu.CompilerParams(
            dimension_semantics=("parallel", "parallel"),
            vmem_limit_bytes=_vmem_limit(blocks, 0, 8 * tq * MEM_LEN * 4)),
        name="mem_xattn",
    )(proj, proj, kv, q_g.reshape(1, X_DIM))


def _forget_cum_kernel(f_ref, b_ref, cum_ref):
    x = f_ref[...] + b_ref[...]
    c = jnp.minimum(x, 0.0) - jnp.log1p(jnp.exp(-jnp.abs(x)))
    n = c.shape[0]
    row = lax.broadcasted_iota(jnp.int32, c.shape, 0)
    shift = 1
    while shift < n:
        c = c + jnp.where(row >= shift, pltpu.roll(c, shift, 0), 0.0)
        shift *= 2
    cum_ref[...] = c


def _forget_cum(f_logits, f_bias, bsz, seq):
    bias = jnp.zeros((1, LANES), F32).at[0, :C_HEADS].set(f_bias)
    blocks = 2 * seq * LANES * 4
    return pl.pallas_call(
        _forget_cum_kernel,
        grid=(bsz,),
        in_specs=[pl.BlockSpec((seq, LANES), lambda b: (b, 0)),
                  pl.BlockSpec((1, LANES), lambda b: (0, 0))],
        out_specs=pl.BlockSpec((seq, LANES), lambda b: (b, 0)),
        out_shape=jax.ShapeDtypeStruct((bsz * seq, LANES), F32),
        compiler_params=pltpu.CompilerParams(
            dimension_semantics=("parallel",),
            vmem_limit_bytes=_vmem_limit(blocks, 0, 6 * seq * LANES * 4)),
        name="forget_cum",
    )(f_logits, bias)


def _fox_kernel(q_ref, k_ref, v_ref, z_ref, cc_ref, qg_ref, kg_ref, o_ref, ka_ref, vt_ref,
                *, group):
    t = ATT_T
    hw = C_DIM
    seq = k_ref.shape[0]
    hg = pl.program_id(1)
    qi = pl.program_id(2)

    def head_column(tile, head):
        lane = lax.broadcasted_iota(jnp.int32, tile.shape, 1)
        return jnp.sum(jnp.where(lane == head, tile, 0.0), axis=-1, keepdims=True)

    @pl.when(qi == 0)
    def _():
        for g in range(group):
            cs = slice(g * hw, (g + 1) * hw)
            k = k_ref[:, cs].astype(F32)
            ms = jnp.mean(k * k, axis=-1, keepdims=True)
            ka_ref[g, :, :hw] = (k * lax.rsqrt(ms + EPS) * kg_ref[...]).astype(BF16)
            ck = head_column(cc_ref[...], hg * group + g)
            ka_ref[g, :, hw:] = _bias_lanes(-ck, True).astype(BF16)
            vt_ref[g] = v_ref[:, cs].astype(F32).T.astype(BF16)

    cc_q = cc_ref[pl.ds(pl.multiple_of(qi * t, t), t), :]
    chains = []
    for g in range(group):
        cs = slice(g * hw, (g + 1) * hw)
        q = q_ref[:, cs].astype(F32)
        ms = jnp.mean(q * q, axis=-1, keepdims=True)
        qn = q * lax.rsqrt(ms + EPS) * qg_ref[...] * (C_DIM ** -0.5)
        qb = _bias_lanes(head_column(cc_q, hg * group + g), False)
        chains.append((jnp.concatenate([qn, qb], axis=1).astype(BF16), ka_ref.at[g], vt_ref.at[g]))

    def tile(c):
        for g, (acc, l) in enumerate(_attend_tile(chains, c, t)):
            cs = slice(g * hw, (g + 1) * hw)
            o_ref[:, cs] = ((acc / l).T * _silu(z_ref[:, cs].astype(F32))).astype(o_ref.dtype)

    _for_each_tile(qi, seq // t, tile)


def _fox_attention(proj, cum, q_g, k_g, bsz, seq):
    t = ATT_T
    nq = seq // t
    hw = C_DIM
    grp = C_GROUP
    gw = grp * hw
    ng = C_HEADS // grp
    blocks = 3 * t * gw * 2 + 2 * seq * gw * 2 + seq * LANES * 4
    scratch = grp * seq * 3 * hw * 2
    small = lambda shape: pl.BlockSpec(shape, lambda b, h, i: (0, 0))
    return pl.pallas_call(
        functools.partial(_fox_kernel, group=grp),
        grid=(bsz, ng, nq),
        in_specs=[
            pl.BlockSpec((t, gw), lambda b, h, i: (b * nq + i, h)),
            pl.BlockSpec((seq, gw), lambda b, h, i: (b, ng + h)),
            pl.BlockSpec((seq, gw), lambda b, h, i: (b, 2 * ng + h)),
            pl.BlockSpec((t, gw), lambda b, h, i: (b * nq + i, 3 * ng + h)),
            pl.BlockSpec((seq, LANES), lambda b, h, i: (b, 0)),
            small((1, hw)), small((1, hw)),
        ],
        out_specs=pl.BlockSpec((t, gw), lambda b, h, i: (b * nq + i, h)),
        out_shape=jax.ShapeDtypeStruct((bsz * seq, C_WIDTH), BF16),
        scratch_shapes=[pltpu.VMEM((grp, seq, 2 * hw), BF16), pltpu.VMEM((grp, hw, seq), BF16)],
        compiler_params=pltpu.CompilerParams(
            dimension_semantics=("parallel", "parallel", "arbitrary"),
            vmem_limit_bytes=_vmem_limit(blocks, scratch, 8 * seq * hw * 4)),
        name="fox_attn",
    )(proj, proj, proj, proj, cum, q_g.reshape(1, hw), k_g.reshape(1, hw))


def _even_layer(h, mem2d, layer, bsz, seq, norm_g, w_in, w_out, a_qn, a_kn, a_lam, a_on,
                b_cw, b_cb, x_qn, x_kn, mem_g, w_mem_kv):
    lam_init = 0.8 - 0.6 * math.exp(-0.3 * layer)
    proj = _norm_proj(h, norm_g, w_in.astype(BF16))
    ya = _diff_attention(proj, a_qn, a_kn, a_lam, a_on, lam_init, bsz, seq)
    yb = _gated_conv(proj, b_cw, b_cb, bsz, seq)
    kv = _mem_kv(mem2d, mem_g, w_mem_kv.astype(BF16), x_kn, bsz)
    yx = _mem_xattn(proj, kv, x_qn, (4 * A_WIDTH + 4 * B_WIDTH) // X_WIDTH, bsz, seq)
    return _out_proj([ya, yb, yx], w_out.astype(BF16), h)


def _odd_layer(h, mem2d, bsz, seq, norm_g, w_in, w_out, c_qn, c_kn, c_fb, x_qn, x_kn,
               mem_g, w_mem_kv):
    main = 4 * C_WIDTH
    w_main = jnp.concatenate([w_in[:, :main], w_in[:, main + C_HEADS:]], axis=1).astype(BF16)
    w_f = jnp.pad(w_in[:, main:main + C_HEADS], ((0, 0), (0, LANES - C_HEADS))).astype(BF16)
    proj, f_logits = _norm_proj(h, norm_g, w_main, w_f)
    cum = _forget_cum(f_logits, c_fb, bsz, seq)
    yc = _fox_attention(proj, cum, c_qn, c_kn, bsz, seq)
    kv = _mem_kv(mem2d, mem_g, w_mem_kv.astype(BF16), x_kn, bsz)
    yx = _mem_xattn(proj, kv, x_qn, main // X_WIDTH, bsz, seq)
    return _out_proj([yc, yx], w_out.astype(BF16), h)


def kernel(x, mem, e_norm_g, e_w_in, e_w_out, e_a_q_norm_g, e_a_k_norm_g, e_a_lambda,
           e_a_out_norm_g, e_b_conv_w, e_b_conv_b, e_x_q_norm_g, e_x_k_norm_g, e_mem_norm_g,
           e_w_mem_kv, o_norm_g, o_w_in, o_w_out, o_c_q_norm_g, o_c_k_norm_g, o_c_forget_b,
           o_x_q_norm_g, o_x_k_norm_g, o_mem_norm_g, o_w_mem_kv):
    bsz, seq, d = x.shape
    h = x.reshape(bsz * seq, d)
    mem2d = mem.reshape(bsz * MEM_LEN, d)
    depth = e_w_in.shape[0] + o_w_in.shape[0]
    for layer in range(depth):
        i = layer // 2
        if layer % 2 == 0:
            h = _even_layer(h, mem2d, layer, bsz, seq, e_norm_g[i], e_w_in[i], e_w_out[i],
                            e_a_q_norm_g[i], e_a_k_norm_g[i], e_a_lambda[i], e_a_out_norm_g[i],
                            e_b_conv_w[i], e_b_conv_b[i], e_x_q_norm_g[i], e_x_k_norm_g[i],
                            e_mem_norm_g[i], e_w_mem_kv[i])
        else:
            h = _odd_layer(h, mem2d, bsz, seq, o_norm_g[i], o_w_in[i], o_w_out[i],
                           o_c_q_norm_g[i], o_c_k_norm_g[i], o_c_forget_b[i],
                           o_x_q_norm_g[i], o_x_k_norm_g[i], o_mem_norm_g[i], o_w_mem_kv[i])
    return h.reshape(bsz, seq, d)
```

```python
import functools
import math

import numpy as np
import jax
import jax.numpy as jnp
from jax import lax
from jax.experimental import pallas as pl
from jax.experimental.pallas import tpu as pltpu

F32 = jnp.float32
BF16 = jnp.bfloat16

D_MODEL = 1024
MEM_LEN = 256
EPS = 1e-6
NEG = -1e30

A_HEADS = 8
A_QK_DIM = 64
A_V_DIM = 128
A_WIDTH = 1024
B_WIDTH = 512
CONV_W = 3
C_HEADS = 12
C_DIM = 128
C_WIDTH = 1536
X_HEADS = 4
X_DIM = 128
X_WIDTH = 512
MIX = 2048

LANES = 128
ONES_ROWS = 16
VMEM_CAP = 56 * 1024 * 1024

PROJ_TM = 1024
PROJ_TN = 1024
OUT_TM = 512
ATT_T = 256
A_GROUP = 2
C_GROUP = 4
X_TQ = 512
CONV_TS = 512


def _vmem_limit(block_bytes, scratch_bytes=0, temp_bytes=0):
    need = 2 * block_bytes + scratch_bytes + temp_bytes + (4 << 20)
    return int(min(max(need, 16 << 20), VMEM_CAP))


def _silu(z):
    return z * jax.nn.sigmoid(z)


def _nt_dot(a, b):
    return lax.dot_general(a, b, (((1,), (1,)), ((), ())), preferred_element_type=F32)


def _proj_kernel(x_ref, g_ref, w_ref, o_ref, xn_ref):
    @pl.when(pl.program_id(1) == 0)
    def _():
        x = x_ref[...]
        ms = jnp.mean(x * x, axis=-1, keepdims=True)
        xn_ref[...] = (x * lax.rsqrt(ms + EPS) * g_ref[...]).astype(BF16)

    o_ref[...] = jnp.dot(xn_ref[...], w_ref[...], preferred_element_type=F32).astype(o_ref.dtype)


def _proj_side_kernel(x_ref, g_ref, w_ref, ws_ref, o_ref, side_ref, xn_ref):
    @pl.when(pl.program_id(1) == 0)
    def _():
        x = x_ref[...]
        ms = jnp.mean(x * x, axis=-1, keepdims=True)
        xn = (x * lax.rsqrt(ms + EPS) * g_ref[...]).astype(BF16)
        xn_ref[...] = xn
        side_ref[...] = jnp.dot(xn, ws_ref[...], preferred_element_type=F32)

    o_ref[...] = jnp.dot(xn_ref[...], w_ref[...], preferred_element_type=F32).astype(o_ref.dtype)


def _norm_proj(h, g, w, w_side=None):
    t, d = h.shape
    n = w.shape[1]
    tm, tn = PROJ_TM, PROJ_TN
    grid = (t // tm, n // tn)
    blocks = tm * d * 4 + d * tn * 2 + tm * tn * 2
    in_specs = [
        pl.BlockSpec((tm, d), lambda i, j: (i, 0)),
        pl.BlockSpec((1, d), lambda i, j: (0, 0)),
        pl.BlockSpec((d, tn), lambda i, j: (0, j)),
    ]
    out_specs = pl.BlockSpec((tm, tn), lambda i, j: (i, j))
    out_shape = jax.ShapeDtypeStruct((t, n), BF16)
    args = [h, g.reshape(1, d), w]
    body = _proj_kernel
    if w_side is not None:
        ns = w_side.shape[1]
        in_specs.append(pl.BlockSpec((d, ns), lambda i, j: (0, 0)))
        out_specs = [out_specs, pl.BlockSpec((tm, ns), lambda i, j: (i, 0))]
        out_shape = [out_shape, jax.ShapeDtypeStruct((t, ns), F32)]
        args.append(w_side)
        body = _proj_side_kernel
        blocks += d * ns * 2 + tm * ns * 4
    return pl.pallas_call(
        body,
        grid=grid,
        in_specs=in_specs,
        out_specs=out_specs,
        out_shape=out_shape,
        scratch_shapes=[pltpu.VMEM((tm, d), BF16)],
        compiler_params=pltpu.CompilerParams(
            dimension_semantics=("parallel", "arbitrary"),
            vmem_limit_bytes=_vmem_limit(blocks, tm * d * 2, tm * tn * 4 + tm * d * 4)),
        name="norm_proj",
    )(*args)


def _out_kernel(*refs, widths):
    ys = refs[:len(widths)]
    w_ref, h_ref, o_ref = refs[len(widths):]
    acc = h_ref[...]
    off = 0
    for y_ref, width in zip(ys, widths):
        acc = acc + jnp.dot(y_ref[...], w_ref[off:off + width, :], preferred_element_type=F32)
        off += width
    o_ref[...] = acc


def _out_proj(ys, w, h):
    t, d = h.shape
    tm = OUT_TM
    widths = tuple(int(y.shape[1]) for y in ys)
    blocks = sum(tm * wd * 2 for wd in widths) + w.size * 2 + 2 * tm * d * 4
    in_specs = [pl.BlockSpec((tm, wd), lambda i: (i, 0)) for wd in widths]
    in_specs += [pl.BlockSpec(w.shape, lambda i: (0, 0)), pl.BlockSpec((tm, d), lambda i: (i, 0))]
    return pl.pallas_call(
        functools.partial(_out_kernel, widths=widths),
        grid=(t // tm,),
        in_specs=in_specs,
        out_specs=pl.BlockSpec((tm, d), lambda i: (i, 0)),
        out_shape=jax.ShapeDtypeStruct((t, d), F32),
        compiler_params=pltpu.CompilerParams(
            dimension_semantics=("parallel",),
            vmem_limit_bytes=_vmem_limit(blocks, 0, 2 * tm * d * 4)),
        name="out_proj",
    )(*ys, w, h)


def _split3(c):
    c1 = c.astype(BF16).astype(F32)
    r = c - c1
    c2 = r.astype(BF16).astype(F32)
    c3 = (r - c2).astype(BF16).astype(F32)
    return c1, c2, c3


def _bias_lanes(term, key_side):
    lane = lax.broadcasted_iota(jnp.int32, (term.shape[0], LANES), 1)
    c1, c2, c3 = _split3(term)
    own = 0 if key_side else 3
    one = 3 if key_side else 0
    out = jnp.where(lane == own, c1, jnp.where(lane == own + 1, c2, jnp.where(lane == own + 2, c3, 0.0)))
    return jnp.where(lane < one, out, jnp.where(lane < one + 3, 1.0, out))


def _attend_tile(chains, c, t):
    krow = lax.broadcasted_iota(jnp.int32, (t, t), 0)
    qcol = lax.broadcasted_iota(jnp.int32, (t, t), 1)
    lo, hi = c * t, (c + 1) * t
    scores = []
    for qa, ka_ref, _ in chains:
        s_d = jnp.where(krow <= qcol, _nt_dot(ka_ref[lo:hi, :], qa), NEG)
        m = jnp.max(s_d, axis=0, keepdims=True)
        s_f = None
        if c > 0:
            s_f = _nt_dot(ka_ref[0:lo, :], qa)
            m = jnp.maximum(m, jnp.max(s_f, axis=0, keepdims=True))
        scores.append((s_d, s_f, m))
    outs = []
    for (_, _, vt_ref), (s_d, s_f, m) in zip(chains, scores):
        dv = vt_ref.shape[0] - ONES_ROWS
        p_d = jnp.exp((s_d - m).astype(BF16))
        acc = jnp.dot(vt_ref[:, lo:hi], p_d, preferred_element_type=F32)
        if c > 0:
            p_f = jnp.exp((s_f - m).astype(BF16))
            acc = acc + jnp.dot(vt_ref[:, 0:lo], p_f, preferred_element_type=F32)
        outs.append((acc[:dv], acc[dv:dv + 1]))
    return outs


def _for_each_tile(qi, n_tiles, body):
    for c in range(n_tiles):
        pl.when(qi == c)(functools.partial(body, c))


def _half_rms(x, lo):
    x2 = x * x
    ms_lo = jnp.sum(jnp.where(lo, x2, 0.0), axis=-1, keepdims=True) * (1.0 / A_QK_DIM)
    ms_hi = jnp.sum(jnp.where(lo, 0.0, x2), axis=-1, keepdims=True) * (1.0 / A_QK_DIM)
    return lax.rsqrt(jnp.where(lo, ms_lo, ms_hi) + EPS)


def _diff_kernel(slopes_ref, q_ref, k_ref, v_ref, z_ref, qg_ref, kg_ref, lam_ref, og_ref,
                 o_ref, ka_ref, vt_ref, *, lam_init, group):
    t = ATT_T
    hw = A_V_DIM
    seq = k_ref.shape[0]
    hg = pl.program_id(1)
    qi = pl.program_id(2)

    @pl.when(qi == 0)
    def _():
        lo_k = lax.broadcasted_iota(jnp.int32, (seq, hw), 1) < A_QK_DIM
        k_pos = lax.broadcasted_iota(jnp.int32, (seq, 1), 0).astype(F32)
        for g in range(group):
            cs = slice(g * hw, (g + 1) * hw)
            k = k_ref[:, cs].astype(F32)
            ka_ref[g, :, :hw] = (k * _half_rms(k, lo_k) * kg_ref[...]).astype(BF16)
            ka_ref[g, :, hw:] = _bias_lanes(slopes_ref[hg * group + g] * k_pos, True).astype(BF16)
            vt_ref[g, :hw] = v_ref[:, cs].astype(F32).T.astype(BF16)
            vt_ref[g, hw:] = jnp.ones((ONES_ROWS, seq), BF16)

    lo = lax.broadcasted_iota(jnp.int32, (t, hw), 1) < A_QK_DIM
    q_pos = (qi * t + lax.broadcasted_iota(jnp.int32, (t, 1), 0)).astype(F32)
    chains = []
    for g in range(group):
        cs = slice(g * hw, (g + 1) * hw)
        q = q_ref[:, cs].astype(F32)
        qn = q * _half_rms(q, lo) * qg_ref[...] * (A_QK_DIM ** -0.5)
        qb = _bias_lanes(-slopes_ref[hg * group + g] * q_pos, False)
        q1 = jnp.concatenate([jnp.where(lo, qn, 0.0), qb], axis=1).astype(BF16)
        q2 = jnp.concatenate([jnp.where(lo, 0.0, qn), qb], axis=1).astype(BF16)
        chains += [(q1, ka_ref.at[g], vt_ref.at[g]), (q2, ka_ref.at[g], vt_ref.at[g])]

    lp = lam_ref[...]
    lam = (jnp.exp(jnp.sum(lp[0:1, :] * lp[1:2, :], axis=-1, keepdims=True))
           - jnp.exp(jnp.sum(lp[2:3, :] * lp[3:4, :], axis=-1, keepdims=True)) + lam_init)

    def tile(c):
        outs = _attend_tile(chains, c, t)
        for g in range(group):
            cs = slice(g * hw, (g + 1) * hw)
            (a1, l1), (a2, l2) = outs[2 * g], outs[2 * g + 1]
            o = (a1 / l1 - lam * (a2 / l2)).T
            ms = jnp.mean(o * o, axis=-1, keepdims=True)
            on = o * lax.rsqrt(ms + EPS) * og_ref[...] * (1.0 - lam_init)
            o_ref[:, cs] = (on * _silu(z_ref[:, cs].astype(F32))).astype(o_ref.dtype)

    _for_each_tile(qi, seq // t, tile)


def _diff_attention(proj, q_g, k_g, lam_params, out_g, lam_init, bsz, seq):
    t = ATT_T
    nq = seq // t
    hw = A_V_DIM
    grp = A_GROUP
    gw = grp * hw
    ng = A_HEADS // grp
    slopes = jnp.asarray(np.array([2.0 ** (-8.0 * (i + 1) / A_HEADS) for i in range(A_HEADS)],
                                  dtype=np.float32))
    qg2 = jnp.concatenate([q_g, q_g]).reshape(1, hw)
    kg2 = jnp.concatenate([k_g, k_g]).reshape(1, hw)
    blocks = 3 * t * gw * 2 + 2 * seq * gw * 2
    scratch = grp * seq * 3 * hw * 2
    small = lambda shape: pl.BlockSpec(shape, lambda b, h, i: (0, 0))
    return pl.pallas_call(
        functools.partial(_diff_kernel, lam_init=lam_init, group=grp),
        grid=(bsz, ng, nq),
        in_specs=[
            pl.BlockSpec(memory_space=pltpu.SMEM),
            pl.BlockSpec((t, gw), lambda b, h, i: (b * nq + i, h)),
            pl.BlockSpec((seq, gw), lambda b, h, i: (b, ng + h)),
            pl.BlockSpec((seq, gw), lambda b, h, i: (b, 2 * ng + h)),
            pl.BlockSpec((t, gw), lambda b, h, i: (b * nq + i, 3 * ng + h)),
            small((1, hw)), small((1, hw)), small((4, A_QK_DIM)), small((1, hw)),
        ],
        out_specs=pl.BlockSpec((t, gw), lambda b, h, i: (b * nq + i, h)),
        out_shape=jax.ShapeDtypeStruct((bsz * seq, A_WIDTH), BF16),
        scratch_shapes=[pltpu.VMEM((grp, seq, 2 * hw), BF16), pltpu.VMEM((grp, hw + ONES_ROWS, seq), BF16)],
        compiler_params=pltpu.CompilerParams(
            dimension_semantics=("parallel", "parallel", "arbitrary"),
            vmem_limit_bytes=_vmem_limit(blocks, scratch, 8 * seq * hw * 4)),
        name="diff_attn",
    )(slopes, proj, proj, proj, proj, qg2, kg2, lam_params, out_g.reshape(1, hw))


def _conv_kernel(h_ref, c_ref, hp_ref, cp_ref, b_ref, z_ref, w_ref, bias_ref, o_ref):
    first = pl.program_id(1) == 0
    u = c_ref[...].astype(F32) * h_ref[...].astype(F32)
    up = cp_ref[...].astype(F32) * hp_ref[...].astype(F32)
    up = jnp.where(first, 0.0, up)
    row = lax.broadcasted_iota(jnp.int32, u.shape, 0)
    u1 = jnp.where(row >= 1, pltpu.roll(u, 1, 0), up[7:8, :])
    u2 = jnp.where(row >= 2, pltpu.roll(u, 2, 0),
                   jnp.where(row == 1, up[7:8, :], up[6:7, :]))
    w = w_ref[...]
    y = w[0:1, :] * u2 + w[1:2, :] * u1 + w[2:3, :] * u + bias_ref[...]
    o_ref[...] = (b_ref[...].astype(F32) * y * _silu(z_ref[...].astype(F32))).astype(o_ref.dtype)


def _gated_conv(proj, w, b, bsz, seq):
    ts = CONV_TS
    ns = seq // ts
    cw = B_WIDTH
    base = 4 * A_WIDTH // cw
    halo = 8
    rows = lambda c: pl.BlockSpec((ts, cw), lambda bi, si: (bi * ns + si, base + c))
    prev = lambda c: pl.BlockSpec(
        (halo, cw), lambda bi, si: (jnp.maximum((bi * ns + si) * (ts // halo) - 1, 0), base + c))
    blocks = 5 * ts * cw * 2
    return pl.pallas_call(
        _conv_kernel,
        grid=(bsz, ns),
        in_specs=[rows(0), rows(1), prev(0), prev(1), rows(2), rows(3),
                  pl.BlockSpec((CONV_W, cw), lambda bi, si: (0, 0)),
                  pl.BlockSpec((1, cw), lambda bi, si: (0, 0))],
        out_specs=pl.BlockSpec((ts, cw), lambda bi, si: (bi * ns + si, 0)),
        out_shape=jax.ShapeDtypeStruct((bsz * seq, cw), BF16),
        compiler_params=pltpu.CompilerParams(
            dimension_semantics=("parallel", "parallel"),
            vmem_limit_bytes=_vmem_limit(blocks, 0, 8 * ts * cw * 4)),
        name="gated_conv",
    )(proj, proj, proj, proj, proj, proj, w, b.reshape(1, cw))


def _mem_kv_kernel(mem_ref, g_ref, w_ref, kg_ref, o_ref):
    x = mem_ref[...]
    ms = jnp.mean(x * x, axis=-1, keepdims=True)
    xn = (x * lax.rsqrt(ms + EPS) * g_ref[...]).astype(BF16)
    kv = jnp.dot(xn, w_ref[...], preferred_element_type=F32)
    for hd in range(X_HEADS):
        k = kv[:, hd * X_DIM:(hd + 1) * X_DIM]
        kms = jnp.mean(k * k, axis=-1, keepdims=True)
        o_ref[:, hd * X_DIM:(hd + 1) * X_DIM] = (k * lax.rsqrt(kms + EPS) * kg_ref[...]).astype(BF16)
    o_ref[:, X_WIDTH:] = kv[:, X_WIDTH:].astype(BF16)


def _mem_kv(mem2d, g, w, k_g, bsz):
    d = mem2d.shape[1]
    blocks = MEM_LEN * d * 4 + w.size * 2 + MEM_LEN * 2 * X_WIDTH * 2
    return pl.pallas_call(
        _mem_kv_kernel,
        grid=(bsz,),
        in_specs=[pl.BlockSpec((MEM_LEN, d), lambda b: (b, 0)),
                  pl.BlockSpec((1, d), lambda b: (0, 0)),
                  pl.BlockSpec(w.shape, lambda b: (0, 0)),
                  pl.BlockSpec((1, X_DIM), lambda b: (0, 0))],
        out_specs=pl.BlockSpec((MEM_LEN, 2 * X_WIDTH), lambda b: (b, 0)),
        out_shape=jax.ShapeDtypeStruct((bsz * MEM_LEN, 2 * X_WIDTH), BF16),
        compiler_params=pltpu.CompilerParams(
            dimension_semantics=("parallel",),
            vmem_limit_bytes=_vmem_limit(blocks, 0, 4 * MEM_LEN * d * 4)),
        name="mem_kv",
    )(mem2d, g.reshape(1, d), w, k_g.reshape(1, X_DIM))


def _xattn_kernel(q_ref, z_ref, kv_ref, qg_ref, o_ref):
    for hd in range(X_HEADS):
        cs = slice(hd * X_DIM, (hd + 1) * X_DIM)
        q = q_ref[:, cs].astype(F32)
        ms = jnp.mean(q * q, axis=-1, keepdims=True)
        qn = (q * lax.rsqrt(ms + EPS) * qg_ref[...] * (X_DIM ** -0.5)).astype(BF16)
        s = _nt_dot(qn, kv_ref[:, cs])
        m = jnp.max(s, axis=-1, keepdims=True)
        p = jnp.exp(s - m)
        l = jnp.sum(p, axis=-1, keepdims=True)
        vs = slice(X_WIDTH + hd * X_DIM, X_WIDTH + (hd + 1) * X_DIM)
        o = jnp.dot(p.astype(BF16), kv_ref[:, vs], preferred_element_type=F32) / l
        o_ref[:, cs] = (o * _silu(z_ref[:, cs].astype(F32))).astype(o_ref.dtype)


def _mem_xattn(proj, kv, q_g, q_col_block, bsz, seq):
    tq = X_TQ
    nq = seq // tq
    blocks = 3 * tq * X_WIDTH * 2 + MEM_LEN * 2 * X_WIDTH * 2
    return pl.pallas_call(
        _xattn_kernel,
        grid=(bsz, nq),
        in_specs=[pl.BlockSpec((tq, X_WIDTH), lambda b, i: (b * nq + i, q_col_block)),
                  pl.BlockSpec((tq, X_WIDTH), lambda b, i: (b * nq + i, q_col_block + 1)),
                  pl.BlockSpec((MEM_LEN, 2 * X_WIDTH), lambda b, i: (b, 0)),
                  pl.BlockSpec((1, X_DIM), lambda b, i: (0, 0))],
        out_specs=pl.BlockSpec((tq, X_WIDTH), lambda b, i: (b * nq + i, 0)),
        out_shape=jax.ShapeDtypeStruct((bsz * seq, X_WIDTH), BF16),
        compiler_params=pltpu.CompilerParams(
            dimension_semantics=("parallel", "parallel"),
            vmem_limit_bytes=_vmem_limit(blocks, 0, 8 * tq * MEM_LEN * 4)),
        name="mem_xattn",
    )(proj, proj, kv, q_g.reshape(1, X_DIM))


def _forget_cum_kernel(f_ref, b_ref, cum_ref):
    x = f_ref[...] + b_ref[...]
    c = jnp.minimum(x, 0.0) - jnp.log1p(jnp.exp(-jnp.abs(x)))
    n = c.shape[0]
    row = lax.broadcasted_iota(jnp.int32, c.shape, 0)
    shift = 1
    while shift < n:
        c = c + jnp.where(row >= shift, pltpu.roll(c, shift, 0), 0.0)
        shift *= 2
    cum_ref[...] = c


def _forget_cum(f_logits, f_bias, bsz, seq):
    bias = jnp.zeros((1, LANES), F32).at[0, :C_HEADS].set(f_bias)
    blocks = 2 * seq * LANES * 4
    return pl.pallas_call(
        _forget_cum_kernel,
        grid=(bsz,),
        in_specs=[pl.BlockSpec((seq, LANES), lambda b: (b, 0)),
                  pl.BlockSpec((1, LANES), lambda b: (0, 0))],
        out_specs=pl.BlockSpec((seq, LANES), lambda b: (b, 0)),
        out_shape=jax.ShapeDtypeStruct((bsz * seq, LANES), F32),
        compiler_params=pltpu.CompilerParams(
            dimension_semantics=("parallel",),
            vmem_limit_bytes=_vmem_limit(blocks, 0, 6 * seq * LANES * 4)),
        name="forget_cum",
    )(f_logits, bias)


def _fox_kernel(q_ref, k_ref, v_ref, z_ref, cc_ref, qg_ref, kg_ref, o_ref, ka_ref, vt_ref,
                *, group):
    t = ATT_T
    hw = C_DIM
    seq = k_ref.shape[0]
    hg = pl.program_id(1)
    qi = pl.program_id(2)

    def head_column(tile, head):
        lane = lax.broadcasted_iota(jnp.int32, tile.shape, 1)
        return jnp.sum(jnp.where(lane == head, tile, 0.0), axis=-1, keepdims=True)

    @pl.when(qi == 0)
    def _():
        for g in range(group):
            cs = slice(g * hw, (g + 1) * hw)
            k = k_ref[:, cs].astype(F32)
            ms = jnp.mean(k * k, axis=-1, keepdims=True)
            ka_ref[g, :, :hw] = (k * lax.rsqrt(ms + EPS) * kg_ref[...]).astype(BF16)
            ck = head_column(cc_ref[...], hg * group + g)
            ka_ref[g, :, hw:] = _bias_lanes(-ck, True).astype(BF16)
            vt_ref[g, :hw] = v_ref[:, cs].astype(F32).T.astype(BF16)
            vt_ref[g, hw:] = jnp.ones((ONES_ROWS, seq), BF16)

    cc_q = cc_ref[pl.ds(pl.multiple_of(qi * t, t), t), :]
    chains = []
    for g in range(group):
        cs = slice(g * hw, (g + 1) * hw)
        q = q_ref[:, cs].astype(F32)
        ms = jnp.mean(q * q, axis=-1, keepdims=True)
        qn = q * lax.rsqrt(ms + EPS) * qg_ref[...] * (C_DIM ** -0.5)
        qb = _bias_lanes(head_column(cc_q, hg * group + g), False)
        chains.append((jnp.concatenate([qn, qb], axis=1).astype(BF16), ka_ref.at[g], vt_ref.at[g]))

    def tile(c):
        for g, (acc, l) in enumerate(_attend_tile(chains, c, t)):
            cs = slice(g * hw, (g + 1) * hw)
            o_ref[:, cs] = ((acc / l).T * _silu(z_ref[:, cs].astype(F32))).astype(o_ref.dtype)

    _for_each_tile(qi, seq // t, tile)


def _fox_attention(proj, cum, q_g, k_g, bsz, seq):
    t = ATT_T
    nq = seq // t
    hw = C_DIM
    grp = C_GROUP
    gw = grp * hw
    ng = C_HEADS // grp
    blocks = 3 * t * gw * 2 + 2 * seq * gw * 2 + seq * LANES * 4
    scratch = grp * seq * 3 * hw * 2
    small = lambda shape: pl.BlockSpec(shape, lambda b, h, i: (0, 0))
    return pl.pallas_call(
        functools.partial(_fox_kernel, group=grp),
        grid=(bsz, ng, nq),
        in_specs=[
            pl.BlockSpec((t, gw), lambda b, h, i: (b * nq + i, h)),
            pl.BlockSpec((seq, gw), lambda b, h, i: (b, ng + h)),
            pl.BlockSpec((seq, gw), lambda b, h, i: (b, 2 * ng + h)),
            pl.BlockSpec((t, gw), lambda b, h, i: (b * nq + i, 3 * ng + h)),
            pl.BlockSpec((seq, LANES), lambda b, h, i: (b, 0)),
            small((1, hw)), small((1, hw)),
        ],
        out_specs=pl.BlockSpec((t, gw), lambda b, h, i: (b * nq + i, h)),
        out_shape=jax.ShapeDtypeStruct((bsz * seq, C_WIDTH), BF16),
        scratch_shapes=[pltpu.VMEM((grp, seq, 2 * hw), BF16), pltpu.VMEM((grp, hw + ONES_ROWS, seq), BF16)],
        compiler_params=pltpu.CompilerParams(
            dimension_semantics=("parallel", "parallel", "arbitrary"),
            vmem_limit_bytes=_vmem_limit(blocks, scratch, 8 * seq * hw * 4)),
        name="fox_attn",
    )(proj, proj, proj, proj, cum, q_g.reshape(1, hw), k_g.reshape(1, hw))


def _even_layer(h, mem2d, layer, bsz, seq, norm_g, w_in, w_out, a_qn, a_kn, a_lam, a_on,
                b_cw, b_cb, x_qn, x_kn, mem_g, w_mem_kv):
    lam_init = 0.8 - 0.6 * math.exp(-0.3 * layer)
    proj = _norm_proj(h, norm_g, w_in.astype(BF16))
    ya = _diff_attention(proj, a_qn, a_kn, a_lam, a_on, lam_init, bsz, seq)
    yb = _gated_conv(proj, b_cw, b_cb, bsz, seq)
    kv = _mem_kv(mem2d, mem_g, w_mem_kv.astype(BF16), x_kn, bsz)
    yx = _mem_xattn(proj, kv, x_qn, (4 * A_WIDTH + 4 * B_WIDTH) // X_WIDTH, bsz, seq)
    return _out_proj([ya, yb, yx], w_out.astype(BF16), h)


def _odd_layer(h, mem2d, bsz, seq, norm_g, w_in, w_out, c_qn, c_kn, c_fb, x_qn, x_kn,
               mem_g, w_mem_kv):
    main = 4 * C_WIDTH
    w_main = jnp.concatenate([w_in[:, :main], w_in[:, main + C_HEADS:]], axis=1).astype(BF16)
    w_f = jnp.pad(w_in[:, main:main + C_HEADS], ((0, 0), (0, LANES - C_HEADS))).astype(BF16)
    proj, f_logits = _norm_proj(h, norm_g, w_main, w_f)
    cum = _forget_cum(f_logits, c_fb, bsz, seq)
    yc = _fox_attention(proj, cum, c_qn, c_kn, bsz, seq)
    kv = _mem_kv(mem2d, mem_g, w_mem_kv.astype(BF16), x_kn, bsz)
    yx = _mem_xattn(proj, kv, x_qn, main // X_WIDTH, bsz, seq)
    return _out_proj([yc, yx], w_out.astype(BF16), h)


def kernel(x, mem, e_norm_g, e_w_in, e_w_out, e_a_q_norm_g, e_a_k_norm_g, e_a_lambda,
           e_a_out_norm_g, e_b_conv_w, e_b_conv_b, e_x_q_norm_g, e_x_k_norm_g, e_mem_norm_g,
           e_w_mem_kv, o_norm_g, o_w_in, o_w_out, o_c_q_norm_g, o_c_k_norm_g, o_c_forget_b,
           o_x_q_norm_g, o_x_k_norm_g, o_mem_norm_g, o_w_mem_kv):
    bsz, seq, d = x.shape
    h = x.reshape(bsz * seq, d)
    mem2d = mem.reshape(bsz * MEM_LEN, d)
    depth = e_w_in.shape[0] + o_w_in.shape[0]
    for layer in range(depth):
        i = layer // 2
        if layer % 2 == 0:
            h = _even_layer(h, mem2d, layer, bsz, seq, e_norm_g[i], e_w_in[i], e_w_out[i],
                            e_a_q_norm_g[i], e_a_k_norm_g[i], e_a_lambda[i], e_a_out_norm_g[i],
                            e_b_conv_w[i], e_b_conv_b[i], e_x_q_norm_g[i], e_x_k_norm_g[i],
                            e_mem_norm_g[i], e_w_mem_kv[i])
        else:
            h = _odd_layer(h, mem2d, bsz, seq, o_norm_g[i], o_w_in[i], o_w_out[i],
                           o_c_q_norm_g[i], o_c_k_norm_g[i], o_c_forget_b[i],
                           o_x_q_norm_g[i], o_x_k_norm_g[i], o_mem_norm_g[i], o_w_mem_kv[i])
    return h.reshape(bsz, seq, d)
```

```python
import functools
import math

import numpy as np
import jax
import jax.numpy as jnp
from jax import lax
from jax.experimental import pallas as pl
from jax.experimental.pallas import tpu as pltpu

F32 = jnp.float32
BF16 = jnp.bfloat16

D_MODEL = 1024
MEM_LEN = 256
EPS = 1e-6
NEG = -1e30

A_HEADS = 8
A_QK_DIM = 64
A_V_DIM = 128
A_WIDTH = 1024
B_WIDTH = 512
CONV_W = 3
C_HEADS = 12
C_DIM = 128
C_WIDTH = 1536
X_HEADS = 4
X_DIM = 128
X_WIDTH = 512
MIX = 2048

LANES = 128
ONES_ROWS = 16
CUM_ROWS = 16
VMEM_CAP = 56 * 1024 * 1024

PROJ_TM = 1024
PROJ_TN = 1024
OUT_TM = 512
ATT_T = 256
A_GROUP = 2
C_GROUP = 4
X_TQ = 512
CONV_TS = 512


def _vmem_limit(block_bytes, scratch_bytes=0, temp_bytes=0):
    need = 2 * block_bytes + scratch_bytes + temp_bytes + (4 << 20)
    return int(min(max(need, 16 << 20), VMEM_CAP))


def _silu(z):
    return z * jax.nn.sigmoid(z)


def _nt_dot(a, b):
    return lax.dot_general(a, b, (((1,), (1,)), ((), ())), preferred_element_type=F32)


def _proj_kernel(x_ref, g_ref, w_ref, o_ref, xn_ref):
    @pl.when(pl.program_id(1) == 0)
    def _():
        x = x_ref[...]
        ms = jnp.mean(x * x, axis=-1, keepdims=True)
        xn_ref[...] = (x * lax.rsqrt(ms + EPS) * g_ref[...]).astype(BF16)

    o_ref[...] = jnp.dot(xn_ref[...], w_ref[...], preferred_element_type=F32).astype(o_ref.dtype)


def _proj_side_kernel(x_ref, g_ref, w_ref, ws_ref, o_ref, side_ref, xn_ref):
    @pl.when(pl.program_id(1) == 0)
    def _():
        x = x_ref[...]
        ms = jnp.mean(x * x, axis=-1, keepdims=True)
        xn = (x * lax.rsqrt(ms + EPS) * g_ref[...]).astype(BF16)
        xn_ref[...] = xn
        side_ref[...] = jnp.dot(xn, ws_ref[...], preferred_element_type=F32)

    o_ref[...] = jnp.dot(xn_ref[...], w_ref[...], preferred_element_type=F32).astype(o_ref.dtype)


def _norm_proj(h, g, w, w_side=None):
    t, d = h.shape
    n = w.shape[1]
    tm, tn = PROJ_TM, PROJ_TN
    grid = (t // tm, n // tn)
    blocks = tm * d * 4 + d * tn * 2 + tm * tn * 2
    in_specs = [
        pl.BlockSpec((tm, d), lambda i, j: (i, 0)),
        pl.BlockSpec((1, d), lambda i, j: (0, 0)),
        pl.BlockSpec((d, tn), lambda i, j: (0, j)),
    ]
    out_specs = pl.BlockSpec((tm, tn), lambda i, j: (i, j))
    out_shape = jax.ShapeDtypeStruct((t, n), BF16)
    args = [h, g.reshape(1, d), w]
    body = _proj_kernel
    if w_side is not None:
        ns = w_side.shape[1]
        in_specs.append(pl.BlockSpec((d, ns), lambda i, j: (0, 0)))
        out_specs = [out_specs, pl.BlockSpec((tm, ns), lambda i, j: (i, 0))]
        out_shape = [out_shape, jax.ShapeDtypeStruct((t, ns), F32)]
        args.append(w_side)
        body = _proj_side_kernel
        blocks += d * ns * 2 + tm * ns * 4
    return pl.pallas_call(
        body,
        grid=grid,
        in_specs=in_specs,
        out_specs=out_specs,
        out_shape=out_shape,
        scratch_shapes=[pltpu.VMEM((tm, d), BF16)],
        compiler_params=pltpu.CompilerParams(
            dimension_semantics=("parallel", "arbitrary"),
            vmem_limit_bytes=_vmem_limit(blocks, tm * d * 2, tm * tn * 4 + tm * d * 4)),
        name="norm_proj",
    )(*args)


def _out_kernel(*refs, widths):
    ys = refs[:len(widths)]
    w_ref, h_ref, o_ref = refs[len(widths):]
    acc = h_ref[...]
    off = 0
    for y_ref, width in zip(ys, widths):
        acc = acc + jnp.dot(y_ref[...], w_ref[off:off + width, :], preferred_element_type=F32)
        off += width
    o_ref[...] = acc


def _out_proj(ys, w, h):
    t, d = h.shape
    tm = OUT_TM
    widths = tuple(int(y.shape[1]) for y in ys)
    blocks = sum(tm * wd * 2 for wd in widths) + w.size * 2 + 2 * tm * d * 4
    in_specs = [pl.BlockSpec((tm, wd), lambda i: (i, 0)) for wd in widths]
    in_specs += [pl.BlockSpec(w.shape, lambda i: (0, 0)), pl.BlockSpec((tm, d), lambda i: (i, 0))]
    return pl.pallas_call(
        functools.partial(_out_kernel, widths=widths),
        grid=(t // tm,),
        in_specs=in_specs,
        out_specs=pl.BlockSpec((tm, d), lambda i: (i, 0)),
        out_shape=jax.ShapeDtypeStruct((t, d), F32),
        compiler_params=pltpu.CompilerParams(
            dimension_semantics=("parallel",),
            vmem_limit_bytes=_vmem_limit(blocks, 0, 2 * tm * d * 4)),
        name="out_proj",
    )(*ys, w, h)


def _group_rms_inv(x, group):
    x2 = x * x
    hi = x2.astype(BF16)
    lo = (x2 - hi.astype(F32)).astype(BF16)
    shift = int(math.log2(group))
    gi = lax.broadcasted_iota(jnp.int32, (LANES, LANES), 0) >> shift
    gj = lax.broadcasted_iota(jnp.int32, (LANES, LANES), 1) >> shift
    member = (gi == gj).astype(BF16)
    total = (jnp.dot(hi, member, preferred_element_type=F32)
             + jnp.dot(lo, member, preferred_element_type=F32))
    return lax.rsqrt(total * (1.0 / group) + EPS)


def _split3(c):
    c1 = c.astype(BF16).astype(F32)
    r = c - c1
    c2 = r.astype(BF16).astype(F32)
    c3 = (r - c2).astype(BF16).astype(F32)
    return c1, c2, c3


def _bias_lanes(term, key_side):
    n = term.shape[1]
    parts = jnp.concatenate(_split3(term), axis=0)
    ones = jnp.ones((3, n), F32)
    pad = jnp.zeros((10, n), F32)
    rows = jnp.concatenate([parts, ones, pad] if key_side else [ones, parts, pad], axis=0)
    sel = (lax.broadcasted_iota(jnp.int32, (16, LANES), 0)
           == lax.broadcasted_iota(jnp.int32, (16, LANES), 1)).astype(BF16)
    lanes = lax.dot_general(rows.astype(BF16), sel, (((0,), (0,)), ((), ())),
                            preferred_element_type=F32)
    return lanes.astype(BF16)


def _attend_tile(chains, c, t):
    krow = lax.broadcasted_iota(jnp.int32, (t, t), 0)
    qcol = lax.broadcasted_iota(jnp.int32, (t, t), 1)
    lo, hi = c * t, (c + 1) * t
    scores = []
    for qa, ka_ref, _ in chains:
        s_d = jnp.where(krow <= qcol, _nt_dot(ka_ref[lo:hi, :], qa), NEG)
        m = jnp.max(s_d, axis=0, keepdims=True)
        s_f = None
        if c > 0:
            s_f = _nt_dot(ka_ref[0:lo, :], qa)
            m = jnp.maximum(m, jnp.max(s_f, axis=0, keepdims=True))
        scores.append((s_d, s_f, m))
    outs = []
    for (_, _, vt_ref), (s_d, s_f, m) in zip(chains, scores):
        dv = vt_ref.shape[0] - ONES_ROWS
        p_d = jnp.exp(s_d - m).astype(BF16)
        acc = jnp.dot(vt_ref[:, lo:hi], p_d, preferred_element_type=F32)
        if c > 0:
            p_f = jnp.exp(s_f - m).astype(BF16)
            acc = acc + jnp.dot(vt_ref[:, 0:lo], p_f, preferred_element_type=F32)
        outs.append((acc[:dv], acc[dv:dv + 1]))
    return outs


def _for_each_tile(qi, n_tiles, body):
    for c in range(n_tiles):
        pl.when(qi == c)(functools.partial(body, c))


def _diff_kernel(slopes_ref, q_ref, k_ref, v_ref, z_ref, qg_ref, kg_ref, lam_ref, og_ref,
                 o_ref, ka_ref, vt_ref, *, lam_init, group):
    t = ATT_T
    hw = A_V_DIM
    seq = k_ref.shape[0]
    hg = pl.program_id(1)
    qi = pl.program_id(2)

    @pl.when(qi == 0)
    def _():
        k_pos = lax.broadcasted_iota(jnp.int32, (1, seq), 1).astype(F32)
        for g in range(group):
            cs = slice(g * hw, (g + 1) * hw)
            k = k_ref[:, cs].astype(F32)
            ka_ref[g, :, :hw] = (k * _group_rms_inv(k, A_QK_DIM) * kg_ref[...]).astype(BF16)
            ka_ref[g, :, hw:] = _bias_lanes(slopes_ref[hg * group + g] * k_pos, True)
            vt_ref[g, :hw] = v_ref[:, cs].astype(F32).T.astype(BF16)
            vt_ref[g, hw:] = jnp.ones((ONES_ROWS, seq), BF16)

    lo = lax.broadcasted_iota(jnp.int32, (t, hw), 1) < A_QK_DIM
    q_pos = (qi * t + lax.broadcasted_iota(jnp.int32, (1, t), 1)).astype(F32)
    chains = []
    for g in range(group):
        cs = slice(g * hw, (g + 1) * hw)
        q = q_ref[:, cs].astype(F32)
        qn = q * _group_rms_inv(q, A_QK_DIM) * qg_ref[...] * (A_QK_DIM ** -0.5)
        qb = _bias_lanes(-slopes_ref[hg * group + g] * q_pos, False)
        q1 = jnp.concatenate([jnp.where(lo, qn, 0.0).astype(BF16), qb], axis=1)
        q2 = jnp.concatenate([jnp.where(lo, 0.0, qn).astype(BF16), qb], axis=1)
        chains += [(q1, ka_ref.at[g], vt_ref.at[g]), (q2, ka_ref.at[g], vt_ref.at[g])]

    lp = lam_ref[...]
    lam = (jnp.exp(jnp.sum(lp[0:1, :] * lp[1:2, :], axis=-1, keepdims=True))
           - jnp.exp(jnp.sum(lp[2:3, :] * lp[3:4, :], axis=-1, keepdims=True)) + lam_init)

    def tile(c):
        outs = _attend_tile(chains, c, t)
        for g in range(group):
            cs = slice(g * hw, (g + 1) * hw)
            (a1, l1), (a2, l2) = outs[2 * g], outs[2 * g + 1]
            o = (a1 / l1 - lam * (a2 / l2)).T
            ms = jnp.mean(o * o, axis=-1, keepdims=True)
            on = o * lax.rsqrt(ms + EPS) * og_ref[...] * (1.0 - lam_init)
            o_ref[:, cs] = (on * _silu(z_ref[:, cs].astype(F32))).astype(o_ref.dtype)

    _for_each_tile(qi, seq // t, tile)


def _diff_attention(proj, q_g, k_g, lam_params, out_g, lam_init, bsz, seq):
    t = ATT_T
    nq = seq // t
    hw = A_V_DIM
    grp = A_GROUP
    gw = grp * hw
    ng = A_HEADS // grp
    slopes = jnp.asarray(np.array([2.0 ** (-8.0 * (i + 1) / A_HEADS) for i in range(A_HEADS)],
                                  dtype=np.float32))
    qg2 = jnp.concatenate([q_g, q_g]).reshape(1, hw)
    kg2 = jnp.concatenate([k_g, k_g]).reshape(1, hw)
    blocks = 3 * t * gw * 2 + 2 * seq * gw * 2
    scratch = grp * seq * 3 * hw * 2
    small = lambda shape: pl.BlockSpec(shape, lambda b, h, i: (0, 0))
    return pl.pallas_call(
        functools.partial(_diff_kernel, lam_init=lam_init, group=grp),
        grid=(bsz, ng, nq),
        in_specs=[
            pl.BlockSpec(memory_space=pltpu.SMEM),
            pl.BlockSpec((t, gw), lambda b, h, i: (b * nq + i, h)),
            pl.BlockSpec((seq, gw), lambda b, h, i: (b, ng + h)),
            pl.BlockSpec((seq, gw), lambda b, h, i: (b, 2 * ng + h)),
            pl.BlockSpec((t, gw), lambda b, h, i: (b * nq + i, 3 * ng + h)),
            small((1, hw)), small((1, hw)), small((4, A_QK_DIM)), small((1, hw)),
        ],
        out_specs=pl.BlockSpec((t, gw), lambda b, h, i: (b * nq + i, h)),
        out_shape=jax.ShapeDtypeStruct((bsz * seq, A_WIDTH), BF16),
        scratch_shapes=[pltpu.VMEM((grp, seq, 2 * hw), BF16), pltpu.VMEM((grp, hw + ONES_ROWS, seq), BF16)],
        compiler_params=pltpu.CompilerParams(
            dimension_semantics=("parallel", "parallel", "arbitrary"),
            vmem_limit_bytes=_vmem_limit(blocks, scratch, 8 * seq * hw * 4)),
        name="diff_attn",
    )(slopes, proj, proj, proj, proj, qg2, kg2, lam_params, out_g.reshape(1, hw))


def _conv_kernel(h_ref, c_ref, hp_ref, cp_ref, b_ref, z_ref, w_ref, bias_ref, o_ref):
    first = pl.program_id(1) == 0
    u = c_ref[...].astype(F32) * h_ref[...].astype(F32)
    up = cp_ref[...].astype(F32) * hp_ref[...].astype(F32)
    up = jnp.where(first, 0.0, up)
    row = lax.broadcasted_iota(jnp.int32, u.shape, 0)
    u1 = jnp.where(row >= 1, pltpu.roll(u, 1, 0), up[7:8, :])
    u2 = jnp.where(row >= 2, pltpu.roll(u, 2, 0),
                   jnp.where(row == 1, up[7:8, :], up[6:7, :]))
    w = w_ref[...]
    y = w[0:1, :] * u2 + w[1:2, :] * u1 + w[2:3, :] * u + bias_ref[...]
    o_ref[...] = (b_ref[...].astype(F32) * y * _silu(z_ref[...].astype(F32))).astype(o_ref.dtype)


def _gated_conv(proj, w, b, bsz, seq):
    ts = CONV_TS
    ns = seq // ts
    cw = B_WIDTH
    base = 4 * A_WIDTH // cw
    halo = 8
    rows = lambda c: pl.BlockSpec((ts, cw), lambda bi, si: (bi * ns + si, base + c))
    prev = lambda c: pl.BlockSpec(
        (halo, cw), lambda bi, si: (jnp.maximum((bi * ns + si) * (ts // halo) - 1, 0), base + c))
    blocks = 5 * ts * cw * 2
    return pl.pallas_call(
        _conv_kernel,
        grid=(bsz, ns),
        in_specs=[rows(0), rows(1), prev(0), prev(1), rows(2), rows(3),
                  pl.BlockSpec((CONV_W, cw), lambda bi, si: (0, 0)),
                  pl.BlockSpec((1, cw), lambda bi, si: (0, 0))],
        out_specs=pl.BlockSpec((ts, cw), lambda bi, si: (bi * ns + si, 0)),
        out_shape=jax.ShapeDtypeStruct((bsz * seq, cw), BF16),
        compiler_params=pltpu.CompilerParams(
            dimension_semantics=("parallel", "parallel"),
            vmem_limit_bytes=_vmem_limit(blocks, 0, 8 * ts * cw * 4)),
        name="gated_conv",
    )(proj, proj, proj, proj, proj, proj, w, b.reshape(1, cw))


def _mem_kv_kernel(mem_ref, g_ref, w_ref, kg_ref, o_ref):
    x = mem_ref[...]
    ms = jnp.mean(x * x, axis=-1, keepdims=True)
    xn = (x * lax.rsqrt(ms + EPS) * g_ref[...]).astype(BF16)
    kv = jnp.dot(xn, w_ref[...], preferred_element_type=F32)
    for hd in range(X_HEADS):
        k = kv[:, hd * X_DIM:(hd + 1) * X_DIM]
        kms = jnp.mean(k * k, axis=-1, keepdims=True)
        o_ref[:, hd * X_DIM:(hd + 1) * X_DIM] = (k * lax.rsqrt(kms + EPS) * kg_ref[...]).astype(BF16)
    o_ref[:, X_WIDTH:] = kv[:, X_WIDTH:].astype(BF16)


def _mem_kv(mem2d, g, w, k_g, bsz):
    d = mem2d.shape[1]
    blocks = MEM_LEN * d * 4 + w.size * 2 + MEM_LEN * 2 * X_WIDTH * 2
    return pl.pallas_call(
        _mem_kv_kernel,
        grid=(bsz,),
        in_specs=[pl.BlockSpec((MEM_LEN, d), lambda b: (b, 0)),
                  pl.BlockSpec((1, d), lambda b: (0, 0)),
                  pl.BlockSpec(w.shape, lambda b: (0, 0)),
                  pl.BlockSpec((1, X_DIM), lambda b: (0, 0))],
        out_specs=pl.BlockSpec((MEM_LEN, 2 * X_WIDTH), lambda b: (b, 0)),
        out_shape=jax.ShapeDtypeStruct((bsz * MEM_LEN, 2 * X_WIDTH), BF16),
        compiler_params=pltpu.CompilerParams(
            dimension_semantics=("parallel",),
            vmem_limit_bytes=_vmem_limit(blocks, 0, 4 * MEM_LEN * d * 4)),
        name="mem_kv",
    )(mem2d, g.reshape(1, d), w, k_g.reshape(1, X_DIM))


def _xattn_kernel(q_ref, z_ref, kv_ref, qg_ref, o_ref):
    for hd in range(X_HEADS):
        cs = slice(hd * X_DIM, (hd + 1) * X_DIM)
        q = q_ref[:, cs].astype(F32)
        ms = jnp.mean(q * q, axis=-1, keepdims=True)
        qn = (q * lax.rsqrt(ms + EPS) * qg_ref[...] * (X_DIM ** -0.5)).astype(BF16)
        s = _nt_dot(qn, kv_ref[:, cs])
        m = jnp.max(s, axis=-1, keepdims=True)
        p = jnp.exp(s - m)
        l = jnp.sum(p, axis=-1, keepdims=True)
        vs = slice(X_WIDTH + hd * X_DIM, X_WIDTH + (hd + 1) * X_DIM)
        o = jnp.dot(p.astype(BF16), kv_ref[:, vs], preferred_element_type=F32) / l
        o_ref[:, cs] = (o * _silu(z_ref[:, cs].astype(F32))).astype(o_ref.dtype)


def _mem_xattn(proj, kv, q_g, q_col_block, bsz, seq):
    tq = X_TQ
    nq = seq // tq
    blocks = 3 * tq * X_WIDTH * 2 + MEM_LEN * 2 * X_WIDTH * 2
    return pl.pallas_call(
        _xattn_kernel,
        grid=(bsz, nq),
        in_specs=[pl.BlockSpec((tq, X_WIDTH), lambda b, i: (b * nq + i, q_col_block)),
                  pl.BlockSpec((tq, X_WIDTH), lambda b, i: (b * nq + i, q_col_block + 1)),
                  pl.BlockSpec((MEM_LEN, 2 * X_WIDTH), lambda b, i: (b, 0)),
                  pl.BlockSpec((1, X_DIM), lambda b, i: (0, 0))],
        out_specs=pl.BlockSpec((tq, X_WIDTH), lambda b, i: (b * nq + i, 0)),
        out_shape=jax.ShapeDtypeStruct((bsz * seq, X_WIDTH), BF16),
        compiler_params=pltpu.CompilerParams(
            dimension_semantics=("parallel", "parallel"),
            vmem_limit_bytes=_vmem_limit(blocks, 0, 8 * tq * MEM_LEN * 4)),
        name="mem_xattn",
    )(proj, proj, kv, q_g.reshape(1, X_DIM))


def _forget_cum_kernel(f_ref, b_ref, cum_ref):
    x = f_ref[...] + b_ref[...]
    c = jnp.minimum(x, 0.0) - jnp.log1p(jnp.exp(-jnp.abs(x)))
    n = c.shape[0]
    row = lax.broadcasted_iota(jnp.int32, c.shape, 0)
    shift = 1
    while shift < n:
        c = c + jnp.where(row >= shift, pltpu.roll(c, shift, 0), 0.0)
        shift *= 2
    cum_ref[...] = c.T[:cum_ref.shape[0], :]


def _forget_cum(f_logits, f_bias, bsz, seq):
    bias = jnp.zeros((1, LANES), F32).at[0, :C_HEADS].set(f_bias)
    blocks = seq * LANES * 4 + CUM_ROWS * seq * 4
    return pl.pallas_call(
        _forget_cum_kernel,
        grid=(bsz,),
        in_specs=[pl.BlockSpec((seq, LANES), lambda b: (b, 0)),
                  pl.BlockSpec((1, LANES), lambda b: (0, 0))],
        out_specs=pl.BlockSpec((CUM_ROWS, seq), lambda b: (b, 0)),
        out_shape=jax.ShapeDtypeStruct((bsz * CUM_ROWS, seq), F32),
        compiler_params=pltpu.CompilerParams(
            dimension_semantics=("parallel",),
            vmem_limit_bytes=_vmem_limit(blocks, 0, 6 * seq * LANES * 4)),
        name="forget_cum",
    )(f_logits, bias)


def _fox_kernel(q_ref, k_ref, v_ref, z_ref, cum_ref, qg_ref, kg_ref, o_ref, ka_ref, vt_ref,
                *, group):
    t = ATT_T
    hw = C_DIM
    seq = k_ref.shape[0]
    qi = pl.program_id(2)

    @pl.when(qi == 0)
    def _():
        for g in range(group):
            cs = slice(g * hw, (g + 1) * hw)
            k = k_ref[:, cs].astype(F32)
            ka_ref[g, :, :hw] = (k * _group_rms_inv(k, C_DIM) * kg_ref[...]).astype(BF16)
            ka_ref[g, :, hw:] = _bias_lanes(-cum_ref[g], True)
            vt_ref[g, :hw] = v_ref[:, cs].astype(F32).T.astype(BF16)
            vt_ref[g, hw:] = jnp.ones((ONES_ROWS, seq), BF16)

    q0 = pl.multiple_of(qi * t, t)
    chains = []
    for g in range(group):
        cs = slice(g * hw, (g + 1) * hw)
        q = q_ref[:, cs].astype(F32)
        qn = (q * _group_rms_inv(q, C_DIM) * qg_ref[...] * (C_DIM ** -0.5)).astype(BF16)
        qb = _bias_lanes(cum_ref[g, :, pl.ds(q0, t)], False)
        chains.append((jnp.concatenate([qn, qb], axis=1), ka_ref.at[g], vt_ref.at[g]))

    def tile(c):
        for g, (acc, l) in enumerate(_attend_tile(chains, c, t)):
            cs = slice(g * hw, (g + 1) * hw)
            o_ref[:, cs] = ((acc / l).T * _silu(z_ref[:, cs].astype(F32))).astype(o_ref.dtype)

    _for_each_tile(qi, seq // t, tile)


def _fox_attention(proj, cum, q_g, k_g, bsz, seq):
    t = ATT_T
    nq = seq // t
    hw = C_DIM
    grp = C_GROUP
    gw = grp * hw
    ng = C_HEADS // grp
    blocks = 3 * t * gw * 2 + 2 * seq * gw * 2 + grp * 8 * seq * 4
    scratch = grp * seq * 3 * hw * 2
    cum3 = cum.reshape(bsz * CUM_ROWS, 1, seq)
    small = lambda shape: pl.BlockSpec(shape, lambda b, h, i: (0, 0))
    return pl.pallas_call(
        functools.partial(_fox_kernel, group=grp),
        grid=(bsz, ng, nq),
        in_specs=[
            pl.BlockSpec((t, gw), lambda b, h, i: (b * nq + i, h)),
            pl.BlockSpec((seq, gw), lambda b, h, i: (b, ng + h)),
            pl.BlockSpec((seq, gw), lambda b, h, i: (b, 2 * ng + h)),
            pl.BlockSpec((t, gw), lambda b, h, i: (b * nq + i, 3 * ng + h)),
            pl.BlockSpec((grp, 1, seq), lambda b, h, i: (b * (CUM_ROWS // grp) + h, 0, 0)),
            small((1, hw)), small((1, hw)),
        ],
        out_specs=pl.BlockSpec((t, gw), lambda b, h, i: (b * nq + i, h)),
        out_shape=jax.ShapeDtypeStruct((bsz * seq, C_WIDTH), BF16),
        scratch_shapes=[pltpu.VMEM((grp, seq, 2 * hw), BF16), pltpu.VMEM((grp, hw + ONES_ROWS, seq), BF16)],
        compiler_params=pltpu.CompilerParams(
            dimension_semantics=("parallel", "parallel", "arbitrary"),
            vmem_limit_bytes=_vmem_limit(blocks, scratch, 8 * seq * hw * 4)),
        name="fox_attn",
    )(proj, proj, proj, proj, cum3, q_g.reshape(1, hw), k_g.reshape(1, hw))


def _even_layer(h, mem2d, layer, bsz, seq, norm_g, w_in, w_out, a_qn, a_kn, a_lam, a_on,
                b_cw, b_cb, x_qn, x_kn, mem_g, w_mem_kv):
    lam_init = 0.8 - 0.6 * math.exp(-0.3 * layer)
    proj = _norm_proj(h, norm_g, w_in.astype(BF16))
    ya = _diff_attention(proj, a_qn, a_kn, a_lam, a_on, lam_init, bsz, seq)
    yb = _gated_conv(proj, b_cw, b_cb, bsz, seq)
    kv = _mem_kv(mem2d, mem_g, w_mem_kv.astype(BF16), x_kn, bsz)
    yx = _mem_xattn(proj, kv, x_qn, (4 * A_WIDTH + 4 * B_WIDTH) // X_WIDTH, bsz, seq)
    return _out_proj([ya, yb, yx], w_out.astype(BF16), h)


def _odd_layer(h, mem2d, bsz, seq, norm_g, w_in, w_out, c_qn, c_kn, c_fb, x_qn, x_kn,
               mem_g, w_mem_kv):
    main = 4 * C_WIDTH
    w_main = jnp.concatenate([w_in[:, :main], w_in[:, main + C_HEADS:]], axis=1).astype(BF16)
    w_f = jnp.pad(w_in[:, main:main + C_HEADS], ((0, 0), (0, LANES - C_HEADS))).astype(BF16)
    proj, f_logits = _norm_proj(h, norm_g, w_main, w_f)
    cum = _forget_cum(f_logits, c_fb, bsz, seq)
    yc = _fox_attention(proj, cum, c_qn, c_kn, bsz, seq)
    kv = _mem_kv(mem2d, mem_g, w_mem_kv.astype(BF16), x_kn, bsz)
    yx = _mem_xattn(proj, kv, x_qn, main // X_WIDTH, bsz, seq)
    return _out_proj([yc, yx], w_out.astype(BF16), h)


def kernel(x, mem, e_norm_g, e_w_in, e_w_out, e_a_q_norm_g, e_a_k_norm_g, e_a_lambda,
           e_a_out_norm_g, e_b_conv_w, e_b_conv_b, e_x_q_norm_g, e_x_k_norm_g, e_mem_norm_g,
           e_w_mem_kv, o_norm_g, o_w_in, o_w_out, o_c_q_norm_g, o_c_k_norm_g, o_c_forget_b,
           o_x_q_norm_g, o_x_k_norm_g, o_mem_norm_g, o_w_mem_kv):
    bsz, seq, d = x.shape
    h = x.reshape(bsz * seq, d)
    mem2d = mem.reshape(bsz * MEM_LEN, d)
    depth = e_w_in.shape[0] + o_w_in.shape[0]
    for layer in range(depth):
        i = layer // 2
        if layer % 2 == 0:
            h = _even_layer(h, mem2d, layer, bsz, seq, e_norm_g[i], e_w_in[i], e_w_out[i],
                            e_a_q_norm_g[i], e_a_k_norm_g[i], e_a_lambda[i], e_a_out_norm_g[i],
                            e_b_conv_w[i], e_b_conv_b[i], e_x_q_norm_g[i], e_x_k_norm_g[i],
                            e_mem_norm_g[i], e_w_mem_kv[i])
        else:
            h = _odd_layer(h, mem2d, bsz, seq, o_norm_g[i], o_w_in[i], o_w_out[i],
                           o_c_q_norm_g[i], o_c_k_norm_g[i], o_c_forget_b[i],
                           o_x_q_norm_g[i], o_x_k_norm_g[i], o_mem_norm_g[i], o_w_mem_kv[i])
    return h.reshape(bsz, seq, d)
```

```python
import functools
import math

import numpy as np
import jax
import jax.numpy as jnp
from jax import lax
from jax.experimental import pallas as pl
from jax.experimental.pallas import tpu as pltpu

F32 = jnp.float32
BF16 = jnp.bfloat16

D_MODEL = 1024
MEM_LEN = 256
EPS = 1e-6
NEG = -1e30

A_HEADS = 8
A_QK_DIM = 64
A_V_DIM = 128
A_WIDTH = 1024
B_WIDTH = 512
CONV_W = 3
C_HEADS = 12
C_DIM = 128
C_WIDTH = 1536
X_HEADS = 4
X_DIM = 128
X_WIDTH = 512
MIX = 2048

LANES = 128
ONES_ROWS = 16
CUM_ROWS = 16
VMEM_CAP = 56 * 1024 * 1024

PROJ_TM = 1024
PROJ_TN = 1024
OUT_TM = 512
ATT_T = 256
A_GROUP = 2
C_GROUP = 4
X_TQ = 512
CONV_TS = 512
ATT_TEMP_BYTES = 20 << 20


def _vmem_limit(block_bytes, scratch_bytes=0, temp_bytes=0):
    need = 2 * block_bytes + scratch_bytes + temp_bytes + (4 << 20)
    return int(min(max(need, 16 << 20), VMEM_CAP))


def _silu(z):
    return z * jax.nn.sigmoid(z)


def _nt_dot(a, b):
    return lax.dot_general(a, b, (((1,), (1,)), ((), ())), preferred_element_type=F32)


def _proj_kernel(x_ref, g_ref, w_ref, o_ref, xn_ref):
    @pl.when(pl.program_id(1) == 0)
    def _():
        x = x_ref[...]
        ms = jnp.mean(x * x, axis=-1, keepdims=True)
        xn_ref[...] = (x * lax.rsqrt(ms + EPS) * g_ref[...]).astype(BF16)

    o_ref[...] = jnp.dot(xn_ref[...], w_ref[...], preferred_element_type=F32).astype(o_ref.dtype)


def _proj_side_kernel(x_ref, g_ref, w_ref, ws_ref, o_ref, side_ref, xn_ref):
    @pl.when(pl.program_id(1) == 0)
    def _():
        x = x_ref[...]
        ms = jnp.mean(x * x, axis=-1, keepdims=True)
        xn = (x * lax.rsqrt(ms + EPS) * g_ref[...]).astype(BF16)
        xn_ref[...] = xn
        side_ref[...] = jnp.dot(xn, ws_ref[...], preferred_element_type=F32)

    o_ref[...] = jnp.dot(xn_ref[...], w_ref[...], preferred_element_type=F32).astype(o_ref.dtype)


def _norm_proj(h, g, w, w_side=None):
    t, d = h.shape
    n = w.shape[1]
    tm, tn = PROJ_TM, PROJ_TN
    grid = (t // tm, n // tn)
    blocks = tm * d * 4 + d * tn * 2 + tm * tn * 2
    in_specs = [
        pl.BlockSpec((tm, d), lambda i, j: (i, 0)),
        pl.BlockSpec((1, d), lambda i, j: (0, 0)),
        pl.BlockSpec((d, tn), lambda i, j: (0, j)),
    ]
    out_specs = pl.BlockSpec((tm, tn), lambda i, j: (i, j))
    out_shape = jax.ShapeDtypeStruct((t, n), BF16)
    args = [h, g.reshape(1, d), w]
    body = _proj_kernel
    if w_side is not None:
        ns = w_side.shape[1]
        in_specs.append(pl.BlockSpec((d, ns), lambda i, j: (0, 0)))
        out_specs = [out_specs, pl.BlockSpec((tm, ns), lambda i, j: (i, 0))]
        out_shape = [out_shape, jax.ShapeDtypeStruct((t, ns), F32)]
        args.append(w_side)
        body = _proj_side_kernel
        blocks += d * ns * 2 + tm * ns * 4
    return pl.pallas_call(
        body,
        grid=grid,
        in_specs=in_specs,
        out_specs=out_specs,
        out_shape=out_shape,
        scratch_shapes=[pltpu.VMEM((tm, d), BF16)],
        compiler_params=pltpu.CompilerParams(
            dimension_semantics=("parallel", "arbitrary"),
            vmem_limit_bytes=_vmem_limit(blocks, tm * d * 2, tm * tn * 4 + tm * d * 4)),
        name="norm_proj",
    )(*args)


def _out_kernel(*refs, widths):
    ys = refs[:len(widths)]
    w_ref, h_ref, o_ref = refs[len(widths):]
    acc = h_ref[...]
    off = 0
    for y_ref, width in zip(ys, widths):
        acc = acc + jnp.dot(y_ref[...], w_ref[off:off + width, :], preferred_element_type=F32)
        off += width
    o_ref[...] = acc


def _out_proj(ys, w, h):
    t, d = h.shape
    tm = OUT_TM
    widths = tuple(int(y.shape[1]) for y in ys)
    blocks = sum(tm * wd * 2 for wd in widths) + w.size * 2 + 2 * tm * d * 4
    in_specs = [pl.BlockSpec((tm, wd), lambda i: (i, 0)) for wd in widths]
    in_specs += [pl.BlockSpec(w.shape, lambda i: (0, 0)), pl.BlockSpec((tm, d), lambda i: (i, 0))]
    return pl.pallas_call(
        functools.partial(_out_kernel, widths=widths),
        grid=(t // tm,),
        in_specs=in_specs,
        out_specs=pl.BlockSpec((tm, d), lambda i: (i, 0)),
        out_shape=jax.ShapeDtypeStruct((t, d), F32),
        compiler_params=pltpu.CompilerParams(
            dimension_semantics=("parallel",),
            vmem_limit_bytes=_vmem_limit(blocks, 0, 2 * tm * d * 4)),
        name="out_proj",
    )(*ys, w, h)


def _group_rms_inv(x, group):
    x2 = x * x
    hi = x2.astype(BF16)
    lo = (x2 - hi.astype(F32)).astype(BF16)
    shift = int(math.log2(group))
    gi = lax.broadcasted_iota(jnp.int32, (LANES, LANES), 0) >> shift
    gj = lax.broadcasted_iota(jnp.int32, (LANES, LANES), 1) >> shift
    member = (gi == gj).astype(BF16)
    total = (jnp.dot(hi, member, preferred_element_type=F32)
             + jnp.dot(lo, member, preferred_element_type=F32))
    return lax.rsqrt(total * (1.0 / group) + EPS)


def _split3(c):
    c1 = c.astype(BF16).astype(F32)
    r = c - c1
    c2 = r.astype(BF16).astype(F32)
    c3 = (r - c2).astype(BF16).astype(F32)
    return c1, c2, c3


def _bias_lanes(term, key_side):
    n = term.shape[1]
    parts = jnp.concatenate(_split3(term), axis=0)
    ones = jnp.ones((3, n), F32)
    pad = jnp.zeros((10, n), F32)
    rows = jnp.concatenate([parts, ones, pad] if key_side else [ones, parts, pad], axis=0)
    sel = (lax.broadcasted_iota(jnp.int32, (16, LANES), 0)
           == lax.broadcasted_iota(jnp.int32, (16, LANES), 1)).astype(BF16)
    lanes = lax.dot_general(rows.astype(BF16), sel, (((0,), (0,)), ((), ())),
                            preferred_element_type=F32)
    return lanes.astype(BF16)


def _score_phase(chains, c, t):
    krow = lax.broadcasted_iota(jnp.int32, (t, t), 0)
    qcol = lax.broadcasted_iota(jnp.int32, (t, t), 1)
    lo, hi = c * t, (c + 1) * t
    scores = []
    for qa, ka_ref, _ in chains:
        s_d = jnp.where(krow <= qcol, _nt_dot(ka_ref[lo:hi, :], qa), NEG)
        m = jnp.max(s_d, axis=0, keepdims=True)
        s_f = None
        if c > 0:
            s_f = _nt_dot(ka_ref[0:lo, :], qa)
            m = jnp.maximum(m, jnp.max(s_f, axis=0, keepdims=True))
        scores.append((s_d, s_f, m))
    return scores


def _value_phase(chains, scores, c, t):
    lo, hi = c * t, (c + 1) * t
    outs = []
    for (_, _, vt_ref), (s_d, s_f, m) in zip(chains, scores):
        dv = vt_ref.shape[0] - ONES_ROWS
        p_d = jnp.exp(s_d - m).astype(BF16)
        acc = jnp.dot(vt_ref[:, lo:hi], p_d, preferred_element_type=F32)
        if c > 0:
            p_f = jnp.exp(s_f - m).astype(BF16)
            acc = acc + jnp.dot(vt_ref[:, 0:lo], p_f, preferred_element_type=F32)
        outs.append((acc[:dv], acc[dv:dv + 1]))
    return outs


def _attend_all_tiles(make_chains, finish, n_tiles, t):
    chains = make_chains(0)
    scores = _score_phase(chains, 0, t)
    for c in range(n_tiles):
        if c + 1 < n_tiles:
            next_chains = make_chains(c + 1)
            next_scores = _score_phase(next_chains, c + 1, t)
        finish(c, _value_phase(chains, scores, c, t))
        if c + 1 < n_tiles:
            chains, scores = next_chains, next_scores


def _diff_kernel(slopes_ref, q_ref, k_ref, v_ref, z_ref, qg_ref, kg_ref, lam_ref, og_ref,
                 o_ref, ka_ref, vt_ref, *, lam_init, group):
    t = ATT_T
    hw = A_V_DIM
    seq = k_ref.shape[0]
    hg = pl.program_id(1)

    k_pos = lax.broadcasted_iota(jnp.int32, (1, seq), 1).astype(F32)
    for g in range(group):
        cs = slice(g * hw, (g + 1) * hw)
        k = k_ref[:, cs].astype(F32)
        ka_ref[g, :, :hw] = (k * _group_rms_inv(k, A_QK_DIM) * kg_ref[...]).astype(BF16)
        ka_ref[g, :, hw:] = _bias_lanes(slopes_ref[hg * group + g] * k_pos, True)
        vt_ref[g, :hw] = v_ref[:, cs].astype(F32).T.astype(BF16)
        vt_ref[g, hw:] = jnp.ones((ONES_ROWS, seq), BF16)

    lo = lax.broadcasted_iota(jnp.int32, (t, hw), 1) < A_QK_DIM
    lp = lam_ref[...]
    lam = (jnp.exp(jnp.sum(lp[0:1, :] * lp[1:2, :], axis=-1, keepdims=True))
           - jnp.exp(jnp.sum(lp[2:3, :] * lp[3:4, :], axis=-1, keepdims=True)) + lam_init)

    def make_chains(c):
        rows = slice(c * t, (c + 1) * t)
        q_pos = (c * t + lax.broadcasted_iota(jnp.int32, (1, t), 1)).astype(F32)
        chains = []
        for g in range(group):
            q = q_ref[rows, g * hw:(g + 1) * hw].astype(F32)
            qn = q * _group_rms_inv(q, A_QK_DIM) * qg_ref[...] * (A_QK_DIM ** -0.5)
            qb = _bias_lanes(-slopes_ref[hg * group + g] * q_pos, False)
            q1 = jnp.concatenate([jnp.where(lo, qn, 0.0).astype(BF16), qb], axis=1)
            q2 = jnp.concatenate([jnp.where(lo, 0.0, qn).astype(BF16), qb], axis=1)
            chains += [(q1, ka_ref.at[g], vt_ref.at[g]), (q2, ka_ref.at[g], vt_ref.at[g])]
        return chains

    def finish(c, outs):
        rows = slice(c * t, (c + 1) * t)
        for g in range(group):
            cs = slice(g * hw, (g + 1) * hw)
            (a1, l1), (a2, l2) = outs[2 * g], outs[2 * g + 1]
            o = (a1 / l1 - lam * (a2 / l2)).T
            ms = jnp.mean(o * o, axis=-1, keepdims=True)
            on = o * lax.rsqrt(ms + EPS) * og_ref[...] * (1.0 - lam_init)
            o_ref[rows, cs] = (on * _silu(z_ref[rows, cs].astype(F32))).astype(o_ref.dtype)

    _attend_all_tiles(make_chains, finish, seq // t, t)


def _diff_attention(proj, q_g, k_g, lam_params, out_g, lam_init, bsz, seq):
    hw = A_V_DIM
    grp = A_GROUP
    gw = grp * hw
    ng = A_HEADS // grp
    slopes = jnp.asarray(np.array([2.0 ** (-8.0 * (i + 1) / A_HEADS) for i in range(A_HEADS)],
                                  dtype=np.float32))
    qg2 = jnp.concatenate([q_g, q_g]).reshape(1, hw)
    kg2 = jnp.concatenate([k_g, k_g]).reshape(1, hw)
    blocks = 5 * seq * gw * 2
    scratch = grp * seq * 3 * hw * 2
    small = lambda shape: pl.BlockSpec(shape, lambda b, h: (0, 0))
    cols = lambda first: pl.BlockSpec((seq, gw), lambda b, h: (b, first * ng + h))
    return pl.pallas_call(
        functools.partial(_diff_kernel, lam_init=lam_init, group=grp),
        grid=(bsz, ng),
        in_specs=[
            pl.BlockSpec(memory_space=pltpu.SMEM),
            cols(0), cols(1), cols(2), cols(3),
            small((1, hw)), small((1, hw)), small((4, A_QK_DIM)), small((1, hw)),
        ],
        out_specs=cols(0),
        out_shape=jax.ShapeDtypeStruct((bsz * seq, A_WIDTH), BF16),
        scratch_shapes=[pltpu.VMEM((grp, seq, 2 * hw), BF16), pltpu.VMEM((grp, hw + ONES_ROWS, seq), BF16)],
        compiler_params=pltpu.CompilerParams(
            dimension_semantics=("parallel", "parallel"),
            vmem_limit_bytes=_vmem_limit(blocks, scratch, ATT_TEMP_BYTES)),
        name="diff_attn",
    )(slopes, proj, proj, proj, proj, qg2, kg2, lam_params, out_g.reshape(1, hw))


def _conv_kernel(h_ref, c_ref, hp_ref, cp_ref, b_ref, z_ref, w_ref, bias_ref, o_ref):
    first = pl.program_id(1) == 0
    u = c_ref[...].astype(F32) * h_ref[...].astype(F32)
    up = cp_ref[...].astype(F32) * hp_ref[...].astype(F32)
    up = jnp.where(first, 0.0, up)
    row = lax.broadcasted_iota(jnp.int32, u.shape, 0)
    u1 = jnp.where(row >= 1, pltpu.roll(u, 1, 0), up[7:8, :])
    u2 = jnp.where(row >= 2, pltpu.roll(u, 2, 0),
                   jnp.where(row == 1, up[7:8, :], up[6:7, :]))
    w = w_ref[...]
    y = w[0:1, :] * u2 + w[1:2, :] * u1 + w[2:3, :] * u + bias_ref[...]
    o_ref[...] = (b_ref[...].astype(F32) * y * _silu(z_ref[...].astype(F32))).astype(o_ref.dtype)


def _gated_conv(proj, w, b, bsz, seq):
    ts = CONV_TS
    ns = seq // ts
    cw = B_WIDTH
    base = 4 * A_WIDTH // cw
    halo = 8
    rows = lambda c: pl.BlockSpec((ts, cw), lambda bi, si: (bi * ns + si, base + c))
    prev = lambda c: pl.BlockSpec(
        (halo, cw), lambda bi, si: (jnp.maximum((bi * ns + si) * (ts // halo) - 1, 0), base + c))
    blocks = 5 * ts * cw * 2
    return pl.pallas_call(
        _conv_kernel,
        grid=(bsz, ns),
        in_specs=[rows(0), rows(1), prev(0), prev(1), rows(2), rows(3),
                  pl.BlockSpec((CONV_W, cw), lambda bi, si: (0, 0)),
                  pl.BlockSpec((1, cw), lambda bi, si: (0, 0))],
        out_specs=pl.BlockSpec((ts, cw), lambda bi, si: (bi * ns + si, 0)),
        out_shape=jax.ShapeDtypeStruct((bsz * seq, cw), BF16),
        compiler_params=pltpu.CompilerParams(
            dimension_semantics=("parallel", "parallel"),
            vmem_limit_bytes=_vmem_limit(blocks, 0, 8 * ts * cw * 4)),
        name="gated_conv",
    )(proj, proj, proj, proj, proj, proj, w, b.reshape(1, cw))


def _mem_kv_kernel(mem_ref, g_ref, w_ref, kg_ref, o_ref):
    x = mem_ref[...]
    ms = jnp.mean(x * x, axis=-1, keepdims=True)
    xn = (x * lax.rsqrt(ms + EPS) * g_ref[...]).astype(BF16)
    kv = jnp.dot(xn, w_ref[...], preferred_element_type=F32)
    for hd in range(X_HEADS):
        k = kv[:, hd * X_DIM:(hd + 1) * X_DIM]
        kms = jnp.mean(k * k, axis=-1, keepdims=True)
        o_ref[:, hd * X_DIM:(hd + 1) * X_DIM] = (k * lax.rsqrt(kms + EPS) * kg_ref[...]).astype(BF16)
    o_ref[:, X_WIDTH:] = kv[:, X_WIDTH:].astype(BF16)


def _mem_kv(mem2d, g, w, k_g, bsz):
    d = mem2d.shape[1]
    blocks = MEM_LEN * d * 4 + w.size * 2 + MEM_LEN * 2 * X_WIDTH * 2
    return pl.pallas_call(
        _mem_kv_kernel,
        grid=(bsz,),
        in_specs=[pl.BlockSpec((MEM_LEN, d), lambda b: (b, 0)),
                  pl.BlockSpec((1, d), lambda b: (0, 0)),
                  pl.BlockSpec(w.shape, lambda b: (0, 0)),
                  pl.BlockSpec((1, X_DIM), lambda b: (0, 0))],
        out_specs=pl.BlockSpec((MEM_LEN, 2 * X_WIDTH), lambda b: (b, 0)),
        out_shape=jax.ShapeDtypeStruct((bsz * MEM_LEN, 2 * X_WIDTH), BF16),
        compiler_params=pltpu.CompilerParams(
            dimension_semantics=("parallel",),
            vmem_limit_bytes=_vmem_limit(blocks, 0, 4 * MEM_LEN * d * 4)),
        name="mem_kv",
    )(mem2d, g.reshape(1, d), w, k_g.reshape(1, X_DIM))


def _xattn_kernel(q_ref, z_ref, kv_ref, qg_ref, o_ref):
    for hd in range(X_HEADS):
        cs = slice(hd * X_DIM, (hd + 1) * X_DIM)
        q = q_ref[:, cs].astype(F32)
        ms = jnp.mean(q * q, axis=-1, keepdims=True)
        qn = (q * lax.rsqrt(ms + EPS) * qg_ref[...] * (X_DIM ** -0.5)).astype(BF16)
        s = _nt_dot(qn, kv_ref[:, cs])
        m = jnp.max(s, axis=-1, keepdims=True)
        p = jnp.exp(s - m)
        l = jnp.sum(p, axis=-1, keepdims=True)
        vs = slice(X_WIDTH + hd * X_DIM, X_WIDTH + (hd + 1) * X_DIM)
        o = jnp.dot(p.astype(BF16), kv_ref[:, vs], preferred_element_type=F32) / l
        o_ref[:, cs] = (o * _silu(z_ref[:, cs].astype(F32))).astype(o_ref.dtype)


def _mem_xattn(proj, kv, q_g, q_col_block, bsz, seq):
    tq = X_TQ
    nq = seq // tq
    blocks = 3 * tq * X_WIDTH * 2 + MEM_LEN * 2 * X_WIDTH * 2
    return pl.pallas_call(
        _xattn_kernel,
        grid=(bsz, nq),
        in_specs=[pl.BlockSpec((tq, X_WIDTH), lambda b, i: (b * nq + i, q_col_block)),
                  pl.BlockSpec((tq, X_WIDTH), lambda b, i: (b * nq + i, q_col_block + 1)),
                  pl.BlockSpec((MEM_LEN, 2 * X_WIDTH), lambda b, i: (b, 0)),
                  pl.BlockSpec((1, X_DIM), lambda b, i: (0, 0))],
        out_specs=pl.BlockSpec((tq, X_WIDTH), lambda b, i: (b * nq + i, 0)),
        out_shape=jax.ShapeDtypeStruct((bsz * seq, X_WIDTH), BF16),
        compiler_params=pltpu.CompilerParams(
            dimension_semantics=("parallel", "parallel"),
            vmem_limit_bytes=_vmem_limit(blocks, 0, 8 * tq * MEM_LEN * 4)),
        name="mem_xattn",
    )(proj, proj, kv, q_g.reshape(1, X_DIM))


def _forget_cum_kernel(f_ref, b_ref, cum_ref):
    x = f_ref[...] + b_ref[...]
    c = jnp.minimum(x, 0.0) - jnp.log1p(jnp.exp(-jnp.abs(x)))
    n = c.shape[0]
    row = lax.broadcasted_iota(jnp.int32, c.shape, 0)
    shift = 1
    while shift < n:
        c = c + jnp.where(row >= shift, pltpu.roll(c, shift, 0), 0.0)
        shift *= 2
    cum_ref[...] = c.T[:cum_ref.shape[0], :]


def _forget_cum(f_logits, f_bias, bsz, seq):
    bias = jnp.zeros((1, LANES), F32).at[0, :C_HEADS].set(f_bias)
    blocks = seq * LANES * 4 + CUM_ROWS * seq * 4
    return pl.pallas_call(
        _forget_cum_kernel,
        grid=(bsz,),
        in_specs=[pl.BlockSpec((seq, LANES), lambda b: (b, 0)),
                  pl.BlockSpec((1, LANES), lambda b: (0, 0))],
        out_specs=pl.BlockSpec((CUM_ROWS, seq), lambda b: (b, 0)),
        out_shape=jax.ShapeDtypeStruct((bsz * CUM_ROWS, seq), F32),
        compiler_params=pltpu.CompilerParams(
            dimension_semantics=("parallel",),
            vmem_limit_bytes=_vmem_limit(blocks, 0, 6 * seq * LANES * 4)),
        name="forget_cum",
    )(f_logits, bias)


def _fox_kernel(q_ref, k_ref, v_ref, z_ref, cum_ref, qg_ref, kg_ref, o_ref, ka_ref, vt_ref,
                *, group):
    t = ATT_T
    hw = C_DIM
    seq = k_ref.shape[0]

    for g in range(group):
        cs = slice(g * hw, (g + 1) * hw)
        k = k_ref[:, cs].astype(F32)
        ka_ref[g, :, :hw] = (k * _group_rms_inv(k, C_DIM) * kg_ref[...]).astype(BF16)
        ka_ref[g, :, hw:] = _bias_lanes(-cum_ref[g], True)
        vt_ref[g, :hw] = v_ref[:, cs].astype(F32).T.astype(BF16)
        vt_ref[g, hw:] = jnp.ones((ONES_ROWS, seq), BF16)

    def make_chains(c):
        rows = slice(c * t, (c + 1) * t)
        chains = []
        for g in range(group):
            q = q_ref[rows, g * hw:(g + 1) * hw].astype(F32)
            qn = (q * _group_rms_inv(q, C_DIM) * qg_ref[...] * (C_DIM ** -0.5)).astype(BF16)
            qb = _bias_lanes(cum_ref[g, :, rows], False)
            chains.append((jnp.concatenate([qn, qb], axis=1), ka_ref.at[g], vt_ref.at[g]))
        return chains

    def finish(c, outs):
        rows = slice(c * t, (c + 1) * t)
        for g, (acc, l) in enumerate(outs):
            cs = slice(g * hw, (g + 1) * hw)
            o_ref[rows, cs] = ((acc / l).T * _silu(z_ref[rows, cs].astype(F32))).astype(o_ref.dtype)

    _attend_all_tiles(make_chains, finish, seq // t, t)


def _fox_attention(proj, cum, q_g, k_g, bsz, seq):
    hw = C_DIM
    grp = C_GROUP
    gw = grp * hw
    ng = C_HEADS // grp
    blocks = 5 * seq * gw * 2 + grp * 8 * seq * 4
    scratch = grp * seq * 3 * hw * 2
    cum3 = cum.reshape(bsz * CUM_ROWS, 1, seq)
    small = lambda shape: pl.BlockSpec(shape, lambda b, h: (0, 0))
    cols = lambda first: pl.BlockSpec((seq, gw), lambda b, h: (b, first * ng + h))
    return pl.pallas_call(
        functools.partial(_fox_kernel, group=grp),
        grid=(bsz, ng),
        in_specs=[
            cols(0), cols(1), cols(2), cols(3),
            pl.BlockSpec((grp, 1, seq), lambda b, h: (b * (CUM_ROWS // grp) + h, 0, 0)),
            small((1, hw)), small((1, hw)),
        ],
        out_specs=cols(0),
        out_shape=jax.ShapeDtypeStruct((bsz * seq, C_WIDTH), BF16),
        scratch_shapes=[pltpu.VMEM((grp, seq, 2 * hw), BF16), pltpu.VMEM((grp, hw + ONES_ROWS, seq), BF16)],
        compiler_params=pltpu.CompilerParams(
            dimension_semantics=("parallel", "parallel"),
            vmem_limit_bytes=_vmem_limit(blocks, scratch, ATT_TEMP_BYTES)),
        name="fox_attn",
    )(proj, proj, proj, proj, cum3, q_g.reshape(1, hw), k_g.reshape(1, hw))


def _even_layer(h, mem2d, layer, bsz, seq, norm_g, w_in, w_out, a_qn, a_kn, a_lam, a_on,
                b_cw, b_cb, x_qn, x_kn, mem_g, w_mem_kv):
    lam_init = 0.8 - 0.6 * math.exp(-0.3 * layer)
    proj = _norm_proj(h, norm_g, w_in.astype(BF16))
    ya = _diff_attention(proj, a_qn, a_kn, a_lam, a_on, lam_init, bsz, seq)
    yb = _gated_conv(proj, b_cw, b_cb, bsz, seq)
    kv = _mem_kv(mem2d, mem_g, w_mem_kv.astype(BF16), x_kn, bsz)
    yx = _mem_xattn(proj, kv, x_qn, (4 * A_WIDTH + 4 * B_WIDTH) // X_WIDTH, bsz, seq)
    return _out_proj([ya, yb, yx], w_out.astype(BF16), h)


def _odd_layer(h, mem2d, bsz, seq, norm_g, w_in, w_out, c_qn, c_kn, c_fb, x_qn, x_kn,
               mem_g, w_mem_kv):
    main = 4 * C_WIDTH
    w_main = jnp.concatenate([w_in[:, :main], w_in[:, main + C_HEADS:]], axis=1).astype(BF16)
    w_f = jnp.pad(w_in[:, main:main + C_HEADS], ((0, 0), (0, LANES - C_HEADS))).astype(BF16)
    proj, f_logits = _norm_proj(h, norm_g, w_main, w_f)
    cum = _forget_cum(f_logits, c_fb, bsz, seq)
    yc = _fox_attention(proj, cum, c_qn, c_kn, bsz, seq)
    kv = _mem_kv(mem2d, mem_g, w_mem_kv.astype(BF16), x_kn, bsz)
    yx = _mem_xattn(proj, kv, x_qn, main // X_WIDTH, bsz, seq)
    return _out_proj([yc, yx], w_out.astype(BF16), h)


def kernel(x, mem, e_norm_g, e_w_in, e_w_out, e_a_q_norm_g, e_a_k_norm_g, e_a_lambda,
           e_a_out_norm_g, e_b_conv_w, e_b_conv_b, e_x_q_norm_g, e_x_k_norm_g, e_mem_norm_g,
           e_w_mem_kv, o_norm_g, o_w_in, o_w_out, o_c_q_norm_g, o_c_k_norm_g, o_c_forget_b,
           o_x_q_norm_g, o_x_k_norm_g, o_mem_norm_g, o_w_mem_kv):
    bsz, seq, d = x.shape
    h = x.reshape(bsz * seq, d)
    mem2d = mem.reshape(bsz * MEM_LEN, d)
    depth = e_w_in.shape[0] + o_w_in.shape[0]
    for layer in range(depth):
        i = layer // 2
        if layer % 2 == 0:
            h = _even_layer(h, mem2d, layer, bsz, seq, e_norm_g[i], e_w_in[i], e_w_out[i],
                            e_a_q_norm_g[i], e_a_k_norm_g[i], e_a_lambda[i], e_a_out_norm_g[i],
                            e_b_conv_w[i], e_b_conv_b[i], e_x_q_norm_g[i], e_x_k_norm_g[i],
                            e_mem_norm_g[i], e_w_mem_kv[i])
        else:
            h = _odd_layer(h, mem2d, bsz, seq, o_norm_g[i], o_w_in[i], o_w_out[i],
                           o_c_q_norm_g[i], o_c_k_norm_g[i], o_c_forget_b[i],
                           o_x_q_norm_g[i], o_x_k_norm_g[i], o_mem_norm_g[i], o_w_mem_kv[i])
    return h.reshape(bsz, seq, d)
```

```python
import functools
import math

import numpy as np
import jax
import jax.numpy as jnp
from jax import lax
from jax.experimental import pallas as pl
from jax.experimental.pallas import tpu as pltpu

F32 = jnp.float32
BF16 = jnp.bfloat16

D_MODEL = 1024
MEM_LEN = 256
EPS = 1e-6
NEG = -1e30
LOG2E = math.log2(math.e)

A_HEADS = 8
A_QK_DIM = 64
A_V_DIM = 128
A_WIDTH = 1024
B_WIDTH = 512
CONV_W = 3
C_HEADS = 12
C_DIM = 128
C_WIDTH = 1536
X_HEADS = 4
X_DIM = 128
X_WIDTH = 512
MIX = 2048

LANES = 128
ONES_ROWS = 16
CUM_ROWS = 16
VMEM_CAP = 56 * 1024 * 1024

PROJ_TM = 1024
PROJ_TN = 1792
OUT_TM = 512
ATT_T = 256
A_GROUP = 2
C_GROUP = 4
X_TQ = 512
CONV_TS = 512
ATT_TEMP_BYTES = 20 << 20


def _vmem_limit(block_bytes, scratch_bytes=0, temp_bytes=0):
    need = 2 * block_bytes + scratch_bytes + temp_bytes + (4 << 20)
    return int(min(max(need, 16 << 20), VMEM_CAP))


def _silu(z):
    return z * jax.nn.sigmoid(z)


def _nt_dot(a, b):
    return lax.dot_general(a, b, (((1,), (1,)), ((), ())), preferred_element_type=F32)


def _proj_kernel(x_ref, g_ref, w_ref, o_ref, xn_ref):
    @pl.when(pl.program_id(1) == 0)
    def _():
        x = x_ref[...]
        ms = jnp.mean(x * x, axis=-1, keepdims=True)
        xn_ref[...] = (x * lax.rsqrt(ms + EPS) * g_ref[...]).astype(BF16)

    o_ref[...] = jnp.dot(xn_ref[...], w_ref[...], preferred_element_type=F32).astype(o_ref.dtype)


def _proj_side_kernel(x_ref, g_ref, w_ref, ws_ref, o_ref, side_ref, xn_ref):
    @pl.when(pl.program_id(1) == 0)
    def _():
        x = x_ref[...]
        ms = jnp.mean(x * x, axis=-1, keepdims=True)
        xn = (x * lax.rsqrt(ms + EPS) * g_ref[...]).astype(BF16)
        xn_ref[...] = xn
        side_ref[...] = jnp.dot(xn, ws_ref[...], preferred_element_type=F32)

    o_ref[...] = jnp.dot(xn_ref[...], w_ref[...], preferred_element_type=F32).astype(o_ref.dtype)


def _norm_proj(h, g, w, w_side=None):
    t, d = h.shape
    n = w.shape[1]
    tm, tn = PROJ_TM, PROJ_TN
    grid = (t // tm, n // tn)
    blocks = tm * d * 4 + d * tn * 2 + tm * tn * 2
    in_specs = [
        pl.BlockSpec((tm, d), lambda i, j: (i, 0)),
        pl.BlockSpec((1, d), lambda i, j: (0, 0)),
        pl.BlockSpec((d, tn), lambda i, j: (0, j)),
    ]
    out_specs = pl.BlockSpec((tm, tn), lambda i, j: (i, j))
    out_shape = jax.ShapeDtypeStruct((t, n), BF16)
    args = [h, g.reshape(1, d), w]
    body = _proj_kernel
    if w_side is not None:
        ns = w_side.shape[1]
        in_specs.append(pl.BlockSpec((d, ns), lambda i, j: (0, 0)))
        out_specs = [out_specs, pl.BlockSpec((tm, ns), lambda i, j: (i, 0))]
        out_shape = [out_shape, jax.ShapeDtypeStruct((t, ns), F32)]
        args.append(w_side)
        body = _proj_side_kernel
        blocks += d * ns * 2 + tm * ns * 4
    return pl.pallas_call(
        body,
        grid=grid,
        in_specs=in_specs,
        out_specs=out_specs,
        out_shape=out_shape,
        scratch_shapes=[pltpu.VMEM((tm, d), BF16)],
        compiler_params=pltpu.CompilerParams(
            dimension_semantics=("parallel", "arbitrary"),
            vmem_limit_bytes=_vmem_limit(blocks, tm * d * 2, tm * tn * 4 + tm * d * 4)),
        name="norm_proj",
    )(*args)


def _out_kernel(*refs, widths):
    ys = refs[:len(widths)]
    w_ref, h_ref, o_ref = refs[len(widths):]
    acc = h_ref[...]
    off = 0
    for y_ref, width in zip(ys, widths):
        acc = acc + jnp.dot(y_ref[...], w_ref[off:off + width, :], preferred_element_type=F32)
        off += width
    o_ref[...] = acc


def _out_proj(ys, w, h):
    t, d = h.shape
    tm = OUT_TM
    widths = tuple(int(y.shape[1]) for y in ys)
    blocks = sum(tm * wd * 2 for wd in widths) + w.size * 2 + 2 * tm * d * 4
    in_specs = [pl.BlockSpec((tm, wd), lambda i: (i, 0)) for wd in widths]
    in_specs += [pl.BlockSpec(w.shape, lambda i: (0, 0)), pl.BlockSpec((tm, d), lambda i: (i, 0))]
    return pl.pallas_call(
        functools.partial(_out_kernel, widths=widths),
        grid=(t // tm,),
        in_specs=in_specs,
        out_specs=pl.BlockSpec((tm, d), lambda i: (i, 0)),
        out_shape=jax.ShapeDtypeStruct((t, d), F32),
        compiler_params=pltpu.CompilerParams(
            dimension_semantics=("parallel",),
            vmem_limit_bytes=_vmem_limit(blocks, 0, 2 * tm * d * 4)),
        name="out_proj",
    )(*ys, w, h)


def _group_rms_inv(x, group):
    x2 = x * x
    hi = x2.astype(BF16)
    lo = (x2 - hi.astype(F32)).astype(BF16)
    shift = int(math.log2(group))
    gi = lax.broadcasted_iota(jnp.int32, (LANES, LANES), 0) >> shift
    gj = lax.broadcasted_iota(jnp.int32, (LANES, LANES), 1) >> shift
    member = (gi == gj).astype(BF16)
    total = (jnp.dot(hi, member, preferred_element_type=F32)
             + jnp.dot(lo, member, preferred_element_type=F32))
    return lax.rsqrt(total * (1.0 / group) + EPS)


def _split3(c):
    c1 = c.astype(BF16).astype(F32)
    r = c - c1
    c2 = r.astype(BF16).astype(F32)
    c3 = (r - c2).astype(BF16).astype(F32)
    return c1, c2, c3


def _bias_lanes(term, key_side):
    n = term.shape[1]
    parts = jnp.concatenate(_split3(term), axis=0)
    ones = jnp.ones((3, n), F32)
    pad = jnp.zeros((10, n), F32)
    rows = jnp.concatenate([parts, ones, pad] if key_side else [ones, parts, pad], axis=0)
    sel = (lax.broadcasted_iota(jnp.int32, (16, LANES), 0)
           == lax.broadcasted_iota(jnp.int32, (16, LANES), 1)).astype(BF16)
    lanes = lax.dot_general(rows.astype(BF16), sel, (((0,), (0,)), ((), ())),
                            preferred_element_type=F32)
    return lanes.astype(BF16)


def _score_phase(chains, c, t):
    krow = lax.broadcasted_iota(jnp.int32, (t, t), 0)
    qcol = lax.broadcasted_iota(jnp.int32, (t, t), 1)
    lo, hi = c * t, (c + 1) * t
    scores = []
    for qa, ka_ref, _ in chains:
        s_d = jnp.where(krow <= qcol, _nt_dot(ka_ref[lo:hi, :], qa), NEG)
        m = jnp.max(s_d, axis=0, keepdims=True)
        s_f = None
        if c > 0:
            s_f = _nt_dot(ka_ref[0:lo, :], qa)
            m = jnp.maximum(m, jnp.max(s_f, axis=0, keepdims=True))
        scores.append((s_d, s_f, m))
    return scores


def _value_phase(chains, scores, c, t):
    lo, hi = c * t, (c + 1) * t
    outs = []
    for (_, _, vt_ref), (s_d, s_f, m) in zip(chains, scores):
        dv = vt_ref.shape[0] - ONES_ROWS
        p_d = jnp.exp2(s_d - m).astype(BF16)
        acc = jnp.dot(vt_ref[:, lo:hi], p_d, preferred_element_type=F32)
        if c > 0:
            p_f = jnp.exp2(s_f - m).astype(BF16)
            acc = acc + jnp.dot(vt_ref[:, 0:lo], p_f, preferred_element_type=F32)
        outs.append((acc[:dv], acc[dv:dv + 1]))
    return outs


def _attend_all_tiles(make_chains, finish, n_tiles, t):
    chains = make_chains(0)
    scores = _score_phase(chains, 0, t)
    for c in range(n_tiles):
        if c + 1 < n_tiles:
            next_chains = make_chains(c + 1)
            next_scores = _score_phase(next_chains, c + 1, t)
        finish(c, _value_phase(chains, scores, c, t))
        if c + 1 < n_tiles:
            chains, scores = next_chains, next_scores


def _diff_kernel(slopes_ref, q_ref, k_ref, v_ref, z_ref, qg_ref, kg_ref, lam_ref, og_ref,
                 o_ref, ka_ref, vt_ref, *, lam_init, group):
    t = ATT_T
    hw = A_V_DIM
    seq = k_ref.shape[0]
    hg = pl.program_id(1)

    k_pos = lax.broadcasted_iota(jnp.int32, (1, seq), 1).astype(F32)
    for g in range(group):
        cs = slice(g * hw, (g + 1) * hw)
        k = k_ref[:, cs].astype(F32)
        ka_ref[g, :, :hw] = (k * _group_rms_inv(k, A_QK_DIM) * kg_ref[...]).astype(BF16)
        ka_ref[g, :, hw:] = _bias_lanes(LOG2E * slopes_ref[hg * group + g] * k_pos, True)
        vt_ref[g, :hw] = v_ref[:, cs].astype(F32).T.astype(BF16)
        vt_ref[g, hw:] = jnp.ones((ONES_ROWS, seq), BF16)

    lo = lax.broadcasted_iota(jnp.int32, (t, hw), 1) < A_QK_DIM
    lp = lam_ref[...]
    lam = (jnp.exp(jnp.sum(lp[0:1, :] * lp[1:2, :], axis=-1, keepdims=True))
           - jnp.exp(jnp.sum(lp[2:3, :] * lp[3:4, :], axis=-1, keepdims=True)) + lam_init)

    def make_chains(c):
        rows = slice(c * t, (c + 1) * t)
        q_pos = (c * t + lax.broadcasted_iota(jnp.int32, (1, t), 1)).astype(F32)
        chains = []
        for g in range(group):
            q = q_ref[rows, g * hw:(g + 1) * hw].astype(F32)
            qn = q * _group_rms_inv(q, A_QK_DIM) * qg_ref[...] * (LOG2E * A_QK_DIM ** -0.5)
            qb = _bias_lanes(-LOG2E * slopes_ref[hg * group + g] * q_pos, False)
            q1 = jnp.concatenate([jnp.where(lo, qn, 0.0).astype(BF16), qb], axis=1)
            q2 = jnp.concatenate([jnp.where(lo, 0.0, qn).astype(BF16), qb], axis=1)
            chains += [(q1, ka_ref.at[g], vt_ref.at[g]), (q2, ka_ref.at[g], vt_ref.at[g])]
        return chains

    def finish(c, outs):
        rows = slice(c * t, (c + 1) * t)
        for g in range(group):
            cs = slice(g * hw, (g + 1) * hw)
            (a1, l1), (a2, l2) = outs[2 * g], outs[2 * g + 1]
            o = (a1 / l1 - lam * (a2 / l2)).T
            ms = jnp.mean(o * o, axis=-1, keepdims=True)
            on = o * lax.rsqrt(ms + EPS) * og_ref[...] * (1.0 - lam_init)
            o_ref[rows, cs] = (on * _silu(z_ref[rows, cs].astype(F32))).astype(o_ref.dtype)

    _attend_all_tiles(make_chains, finish, seq // t, t)


def _diff_attention(proj, q_g, k_g, lam_params, out_g, lam_init, bsz, seq):
    hw = A_V_DIM
    grp = A_GROUP
    gw = grp * hw
    ng = A_HEADS // grp
    slopes = jnp.asarray(np.array([2.0 ** (-8.0 * (i + 1) / A_HEADS) for i in range(A_HEADS)],
                                  dtype=np.float32))
    qg2 = jnp.concatenate([q_g, q_g]).reshape(1, hw)
    kg2 = jnp.concatenate([k_g, k_g]).reshape(1, hw)
    blocks = 5 * seq * gw * 2
    scratch = grp * seq * 3 * hw * 2
    small = lambda shape: pl.BlockSpec(shape, lambda b, h: (0, 0))
    cols = lambda first: pl.BlockSpec((seq, gw), lambda b, h: (b, first * ng + h))
    return pl.pallas_call(
        functools.partial(_diff_kernel, lam_init=lam_init, group=grp),
        grid=(bsz, ng),
        in_specs=[
            pl.BlockSpec(memory_space=pltpu.SMEM),
            cols(0), cols(1), cols(2), cols(3),
            small((1, hw)), small((1, hw)), small((4, A_QK_DIM)), small((1, hw)),
        ],
        out_specs=cols(0),
        out_shape=jax.ShapeDtypeStruct((bsz * seq, A_WIDTH), BF16),
        scratch_shapes=[pltpu.VMEM((grp, seq, 2 * hw), BF16), pltpu.VMEM((grp, hw + ONES_ROWS, seq), BF16)],
        compiler_params=pltpu.CompilerParams(
            dimension_semantics=("parallel", "parallel"),
            vmem_limit_bytes=_vmem_limit(blocks, scratch, ATT_TEMP_BYTES)),
        name="diff_attn",
    )(slopes, proj, proj, proj, proj, qg2, kg2, lam_params, out_g.reshape(1, hw))


def _conv_kernel(h_ref, c_ref, hp_ref, cp_ref, b_ref, z_ref, w_ref, bias_ref, o_ref):
    first = pl.program_id(1) == 0
    u = c_ref[...].astype(F32) * h_ref[...].astype(F32)
    up = cp_ref[...].astype(F32) * hp_ref[...].astype(F32)
    up = jnp.where(first, 0.0, up)
    row = lax.broadcasted_iota(jnp.int32, u.shape, 0)
    u1 = jnp.where(row >= 1, pltpu.roll(u, 1, 0), up[7:8, :])
    u2 = jnp.where(row >= 2, pltpu.roll(u, 2, 0),
                   jnp.where(row == 1, up[7:8, :], up[6:7, :]))
    w = w_ref[...]
    y = w[0:1, :] * u2 + w[1:2, :] * u1 + w[2:3, :] * u + bias_ref[...]
    o_ref[...] = (b_ref[...].astype(F32) * y * _silu(z_ref[...].astype(F32))).astype(o_ref.dtype)


def _gated_conv(proj, w, b, bsz, seq):
    ts = CONV_TS
    ns = seq // ts
    cw = B_WIDTH
    base = 4 * A_WIDTH // cw
    halo = 8
    rows = lambda c: pl.BlockSpec((ts, cw), lambda bi, si: (bi * ns + si, base + c))
    prev = lambda c: pl.BlockSpec(
        (halo, cw), lambda bi, si: (jnp.maximum((bi * ns + si) * (ts // halo) - 1, 0), base + c))
    blocks = 5 * ts * cw * 2
    return pl.pallas_call(
        _conv_kernel,
        grid=(bsz, ns),
        in_specs=[rows(0), rows(1), prev(0), prev(1), rows(2), rows(3),
                  pl.BlockSpec((CONV_W, cw), lambda bi, si: (0, 0)),
                  pl.BlockSpec((1, cw), lambda bi, si: (0, 0))],
        out_specs=pl.BlockSpec((ts, cw), lambda bi, si: (bi * ns + si, 0)),
        out_shape=jax.ShapeDtypeStruct((bsz * seq, cw), BF16),
        compiler_params=pltpu.CompilerParams(
            dimension_semantics=("parallel", "parallel"),
            vmem_limit_bytes=_vmem_limit(blocks, 0, 8 * ts * cw * 4)),
        name="gated_conv",
    )(proj, proj, proj, proj, proj, proj, w, b.reshape(1, cw))


def _mem_kv_kernel(mem_ref, g_ref, w_ref, kg_ref, o_ref):
    x = mem_ref[...]
    ms = jnp.mean(x * x, axis=-1, keepdims=True)
    xn = (x * lax.rsqrt(ms + EPS) * g_ref[...]).astype(BF16)
    kv = jnp.dot(xn, w_ref[...], preferred_element_type=F32)
    for hd in range(X_HEADS):
        k = kv[:, hd * X_DIM:(hd + 1) * X_DIM]
        kms = jnp.mean(k * k, axis=-1, keepdims=True)
        o_ref[:, hd * X_DIM:(hd + 1) * X_DIM] = (k * lax.rsqrt(kms + EPS) * kg_ref[...]).astype(BF16)
    o_ref[:, X_WIDTH:] = kv[:, X_WIDTH:].astype(BF16)


def _mem_kv(mem2d, g, w, k_g, bsz):
    d = mem2d.shape[1]
    blocks = MEM_LEN * d * 4 + w.size * 2 + MEM_LEN * 2 * X_WIDTH * 2
    return pl.pallas_call(
        _mem_kv_kernel,
        grid=(bsz,),
        in_specs=[pl.BlockSpec((MEM_LEN, d), lambda b: (b, 0)),
                  pl.BlockSpec((1, d), lambda b: (0, 0)),
                  pl.BlockSpec(w.shape, lambda b: (0, 0)),
                  pl.BlockSpec((1, X_DIM), lambda b: (0, 0))],
        out_specs=pl.BlockSpec((MEM_LEN, 2 * X_WIDTH), lambda b: (b, 0)),
        out_shape=jax.ShapeDtypeStruct((bsz * MEM_LEN, 2 * X_WIDTH), BF16),
        compiler_params=pltpu.CompilerParams(
            dimension_semantics=("parallel",),
            vmem_limit_bytes=_vmem_limit(blocks, 0, 4 * MEM_LEN * d * 4)),
        name="mem_kv",
    )(mem2d, g.reshape(1, d), w, k_g.reshape(1, X_DIM))


def _xattn_kernel(q_ref, z_ref, kv_ref, qg_ref, o_ref):
    for hd in range(X_HEADS):
        cs = slice(hd * X_DIM, (hd + 1) * X_DIM)
        q = q_ref[:, cs].astype(F32)
        ms = jnp.mean(q * q, axis=-1, keepdims=True)
        qn = (q * lax.rsqrt(ms + EPS) * qg_ref[...] * (X_DIM ** -0.5)).astype(BF16)
        s = _nt_dot(qn, kv_ref[:, cs])
        m = jnp.max(s, axis=-1, keepdims=True)
        p = jnp.exp(s - m)
        l = jnp.sum(p, axis=-1, keepdims=True)
        vs = slice(X_WIDTH + hd * X_DIM, X_WIDTH + (hd + 1) * X_DIM)
        o = jnp.dot(p.astype(BF16), kv_ref[:, vs], preferred_element_type=F32) / l
        o_ref[:, cs] = (o * _silu(z_ref[:, cs].astype(F32))).astype(o_ref.dtype)


def _mem_xattn(proj, kv, q_g, q_col_block, bsz, seq):
    tq = X_TQ
    nq = seq // tq
    blocks = 3 * tq * X_WIDTH * 2 + MEM_LEN * 2 * X_WIDTH * 2
    return pl.pallas_call(
        _xattn_kernel,
        grid=(bsz, nq),
        in_specs=[pl.BlockSpec((tq, X_WIDTH), lambda b, i: (b * nq + i, q_col_block)),
                  pl.BlockSpec((tq, X_WIDTH), lambda b, i: (b * nq + i, q_col_block + 1)),
                  pl.BlockSpec((MEM_LEN, 2 * X_WIDTH), lambda b, i: (b, 0)),
                  pl.BlockSpec((1, X_DIM), lambda b, i: (0, 0))],
        out_specs=pl.BlockSpec((tq, X_WIDTH), lambda b, i: (b * nq + i, 0)),
        out_shape=jax.ShapeDtypeStruct((bsz * seq, X_WIDTH), BF16),
        compiler_params=pltpu.CompilerParams(
            dimension_semantics=("parallel", "parallel"),
            vmem_limit_bytes=_vmem_limit(blocks, 0, 8 * tq * MEM_LEN * 4)),
        name="mem_xattn",
    )(proj, proj, kv, q_g.reshape(1, X_DIM))


def _forget_cum_kernel(f_ref, b_ref, cum_ref):
    x = f_ref[...] + b_ref[...]
    c = jnp.minimum(x, 0.0) - jnp.log1p(jnp.exp(-jnp.abs(x)))
    n = c.shape[0]
    row = lax.broadcasted_iota(jnp.int32, c.shape, 0)
    shift = 1
    while shift < n:
        c = c + jnp.where(row >= shift, pltpu.roll(c, shift, 0), 0.0)
        shift *= 2
    cum_ref[...] = c.T[:cum_ref.shape[0], :]


def _forget_cum(f_logits, f_bias, bsz, seq):
    bias = jnp.zeros((1, LANES), F32).at[0, :C_HEADS].set(f_bias)
    blocks = seq * LANES * 4 + CUM_ROWS * seq * 4
    return pl.pallas_call(
        _forget_cum_kernel,
        grid=(bsz,),
        in_specs=[pl.BlockSpec((seq, LANES), lambda b: (b, 0)),
                  pl.BlockSpec((1, LANES), lambda b: (0, 0))],
        out_specs=pl.BlockSpec((CUM_ROWS, seq), lambda b: (b, 0)),
        out_shape=jax.ShapeDtypeStruct((bsz * CUM_ROWS, seq), F32),
        compiler_params=pltpu.CompilerParams(
            dimension_semantics=("parallel",),
            vmem_limit_bytes=_vmem_limit(blocks, 0, 6 * seq * LANES * 4)),
        name="forget_cum",
    )(f_logits, bias)


def _fox_kernel(q_ref, k_ref, v_ref, z_ref, cum_ref, qg_ref, kg_ref, o_ref, ka_ref, vt_ref,
                *, group):
    t = ATT_T
    hw = C_DIM
    seq = k_ref.shape[0]

    for g in range(group):
        cs = slice(g * hw, (g + 1) * hw)
        k = k_ref[:, cs].astype(F32)
        ka_ref[g, :, :hw] = (k * _group_rms_inv(k, C_DIM) * kg_ref[...]).astype(BF16)
        ka_ref[g, :, hw:] = _bias_lanes(-LOG2E * cum_ref[g], True)
        vt_ref[g, :hw] = v_ref[:, cs].astype(F32).T.astype(BF16)
        vt_ref[g, hw:] = jnp.ones((ONES_ROWS, seq), BF16)

    def make_chains(c):
        rows = slice(c * t, (c + 1) * t)
        chains = []
        for g in range(group):
            q = q_ref[rows, g * hw:(g + 1) * hw].astype(F32)
            qn = (q * _group_rms_inv(q, C_DIM) * qg_ref[...] * (LOG2E * C_DIM ** -0.5)).astype(BF16)
            qb = _bias_lanes(LOG2E * cum_ref[g, :, rows], False)
            chains.append((jnp.concatenate([qn, qb], axis=1), ka_ref.at[g], vt_ref.at[g]))
        return chains

    def finish(c, outs):
        rows = slice(c * t, (c + 1) * t)
        for g, (acc, l) in enumerate(outs):
            cs = slice(g * hw, (g + 1) * hw)
            o_ref[rows, cs] = ((acc / l).T * _silu(z_ref[rows, cs].astype(F32))).astype(o_ref.dtype)

    _attend_all_tiles(make_chains, finish, seq // t, t)


def _fox_attention(proj, cum, q_g, k_g, bsz, seq):
    hw = C_DIM
    grp = C_GROUP
    gw = grp * hw
    ng = C_HEADS // grp
    blocks = 5 * seq * gw * 2 + grp * 8 * seq * 4
    scratch = grp * seq * 3 * hw * 2
    cum3 = cum.reshape(bsz * CUM_ROWS, 1, seq)
    small = lambda shape: pl.BlockSpec(shape, lambda b, h: (0, 0))
    cols = lambda first: pl.BlockSpec((seq, gw), lambda b, h: (b, first * ng + h))
    return pl.pallas_call(
        functools.partial(_fox_kernel, group=grp),
        grid=(bsz, ng),
        in_specs=[
            cols(0), cols(1), cols(2), cols(3),
            pl.BlockSpec((grp, 1, seq), lambda b, h: (b * (CUM_ROWS // grp) + h, 0, 0)),
            small((1, hw)), small((1, hw)),
        ],
        out_specs=cols(0),
        out_shape=jax.ShapeDtypeStruct((bsz * seq, C_WIDTH), BF16),
        scratch_shapes=[pltpu.VMEM((grp, seq, 2 * hw), BF16), pltpu.VMEM((grp, hw + ONES_ROWS, seq), BF16)],
        compiler_params=pltpu.CompilerParams(
            dimension_semantics=("parallel", "parallel"),
            vmem_limit_bytes=_vmem_limit(blocks, scratch, ATT_TEMP_BYTES)),
        name="fox_attn",
    )(proj, proj, proj, proj, cum3, q_g.reshape(1, hw), k_g.reshape(1, hw))


def _even_layer(h, mem2d, layer, bsz, seq, norm_g, w_in, w_out, a_qn, a_kn, a_lam, a_on,
                b_cw, b_cb, x_qn, x_kn, mem_g, w_mem_kv):
    lam_init = 0.8 - 0.6 * math.exp(-0.3 * layer)
    proj = _norm_proj(h, norm_g, w_in.astype(BF16))
    ya = _diff_attention(proj, a_qn, a_kn, a_lam, a_on, lam_init, bsz, seq)
    yb = _gated_conv(proj, b_cw, b_cb, bsz, seq)
    kv = _mem_kv(mem2d, mem_g, w_mem_kv.astype(BF16), x_kn, bsz)
    yx = _mem_xattn(proj, kv, x_qn, (4 * A_WIDTH + 4 * B_WIDTH) // X_WIDTH, bsz, seq)
    return _out_proj([ya, yb, yx], w_out.astype(BF16), h)


def _odd_layer(h, mem2d, bsz, seq, norm_g, w_in, w_out, c_qn, c_kn, c_fb, x_qn, x_kn,
               mem_g, w_mem_kv):
    main = 4 * C_WIDTH
    w_main = jnp.concatenate([w_in[:, :main], w_in[:, main + C_HEADS:]], axis=1).astype(BF16)
    w_f = jnp.pad(w_in[:, main:main + C_HEADS], ((0, 0), (0, LANES - C_HEADS))).astype(BF16)
    proj, f_logits = _norm_proj(h, norm_g, w_main, w_f)
    cum = _forget_cum(f_logits, c_fb, bsz, seq)
    yc = _fox_attention(proj, cum, c_qn, c_kn, bsz, seq)
    kv = _mem_kv(mem2d, mem_g, w_mem_kv.astype(BF16), x_kn, bsz)
    yx = _mem_xattn(proj, kv, x_qn, main // X_WIDTH, bsz, seq)
    return _out_proj([yc, yx], w_out.astype(BF16), h)


def kernel(x, mem, e_norm_g, e_w_in, e_w_out, e_a_q_norm_g, e_a_k_norm_g, e_a_lambda,
           e_a_out_norm_g, e_b_conv_w, e_b_conv_b, e_x_q_norm_g, e_x_k_norm_g, e_mem_norm_g,
           e_w_mem_kv, o_norm_g, o_w_in, o_w_out, o_c_q_norm_g, o_c_k_norm_g, o_c_forget_b,
           o_x_q_norm_g, o_x_k_norm_g, o_mem_norm_g, o_w_mem_kv):
    bsz, seq, d = x.shape
    h = x.reshape(bsz * seq, d)
    mem2d = mem.reshape(bsz * MEM_LEN, d)
    depth = e_w_in.shape[0] + o_w_in.shape[0]
    for layer in range(depth):
        i = layer // 2
        if layer % 2 == 0:
            h = _even_layer(h, mem2d, layer, bsz, seq, e_norm_g[i], e_w_in[i], e_w_out[i],
                            e_a_q_norm_g[i], e_a_k_norm_g[i], e_a_lambda[i], e_a_out_norm_g[i],
                            e_b_conv_w[i], e_b_conv_b[i], e_x_q_norm_g[i], e_x_k_norm_g[i],
                            e_mem_norm_g[i], e_w_mem_kv[i])
        else:
            h = _odd_layer(h, mem2d, bsz, seq, o_norm_g[i], o_w_in[i], o_w_out[i],
                           o_c_q_norm_g[i], o_c_k_norm_g[i], o_c_forget_b[i],
                           o_x_q_norm_g[i], o_x_k_norm_g[i], o_mem_norm_g[i], o_w_mem_kv[i])
    return h.reshape(bsz, seq, d)
```

```python
import functools
import math

import numpy as np
import jax
import jax.numpy as jnp
from jax import lax
from jax.experimental import pallas as pl
from jax.experimental.pallas import tpu as pltpu

F32 = jnp.float32
BF16 = jnp.bfloat16

D_MODEL = 1024
MEM_LEN = 256
EPS = 1e-6
NEG = -1e30
LOG2E = math.log2(math.e)

A_HEADS = 8
A_QK_DIM = 64
A_V_DIM = 128
A_WIDTH = 1024
B_WIDTH = 512
CONV_W = 3
C_HEADS = 12
C_DIM = 128
C_WIDTH = 1536
X_HEADS = 4
X_DIM = 128
X_WIDTH = 512
MIX = 2048

LANES = 128
ONES_ROWS = 16
CUM_ROWS = 16
VMEM_CAP = 56 * 1024 * 1024

PROJ_TM = 1024
PROJ_TN = 1792
OUT_TM = 1024
ATT_T = 256
A_GROUP = 2
C_GROUP = 4
X_TQ = 512
CONV_TS = 512
ATT_TEMP_BYTES = 20 << 20


def _vmem_limit(block_bytes, scratch_bytes=0, temp_bytes=0):
    need = 2 * block_bytes + scratch_bytes + temp_bytes + (4 << 20)
    return int(min(max(need, 16 << 20), VMEM_CAP))


def _silu(z):
    return z * jax.nn.sigmoid(z)


def _nt_dot(a, b):
    return lax.dot_general(a, b, (((1,), (1,)), ((), ())), preferred_element_type=F32)


def _proj_kernel(*refs, has_side, w_rows_are_outputs):
    if has_side:
        x_ref, g_ref, w_ref, ws_ref, o_ref, side_ref, xn_ref = refs
    else:
        x_ref, g_ref, w_ref, o_ref, xn_ref = refs
    matmul = _nt_dot if w_rows_are_outputs else functools.partial(jnp.dot, preferred_element_type=F32)

    @pl.when(pl.program_id(1) == 0)
    def _():
        x = x_ref[...]
        ms = jnp.mean(x * x, axis=-1, keepdims=True)
        xn = (x * lax.rsqrt(ms + EPS) * g_ref[...]).astype(BF16)
        xn_ref[...] = xn
        if has_side:
            side_ref[...] = matmul(xn, ws_ref[...])

    o_ref[...] = matmul(xn_ref[...], w_ref[...]).astype(o_ref.dtype)


def _norm_proj(h, g, w, w_side=None, w_rows_are_outputs=False):
    t, d = h.shape
    n = w.shape[0] if w_rows_are_outputs else w.shape[1]
    tm, tn = PROJ_TM, PROJ_TN
    grid = (t // tm, n // tn)
    blocks = tm * d * 4 + d * tn * 2 + tm * tn * 2
    w_spec = (pl.BlockSpec((tn, d), lambda i, j: (j, 0)) if w_rows_are_outputs
              else pl.BlockSpec((d, tn), lambda i, j: (0, j)))
    in_specs = [
        pl.BlockSpec((tm, d), lambda i, j: (i, 0)),
        pl.BlockSpec((1, d), lambda i, j: (0, 0)),
        w_spec,
    ]
    out_specs = pl.BlockSpec((tm, tn), lambda i, j: (i, j))
    out_shape = jax.ShapeDtypeStruct((t, n), BF16)
    args = [h, g.reshape(1, d), w]
    if w_side is not None:
        ns = w_side.shape[0] if w_rows_are_outputs else w_side.shape[1]
        in_specs.append(pl.BlockSpec(w_side.shape, lambda i, j: (0, 0)))
        out_specs = [out_specs, pl.BlockSpec((tm, ns), lambda i, j: (i, 0))]
        out_shape = [out_shape, jax.ShapeDtypeStruct((t, ns), F32)]
        args.append(w_side)
        blocks += d * ns * 2 + tm * ns * 4
    return pl.pallas_call(
        functools.partial(_proj_kernel, has_side=w_side is not None,
                          w_rows_are_outputs=w_rows_are_outputs),
        grid=grid,
        in_specs=in_specs,
        out_specs=out_specs,
        out_shape=out_shape,
        scratch_shapes=[pltpu.VMEM((tm, d), BF16)],
        compiler_params=pltpu.CompilerParams(
            dimension_semantics=("parallel", "arbitrary"),
            vmem_limit_bytes=_vmem_limit(blocks, tm * d * 2, tm * tn * 4 + tm * d * 4)),
        name="norm_proj",
    )(*args)


def _out_kernel(*refs, widths):
    ys = refs[:len(widths)]
    w_ref, h_ref, o_ref = refs[len(widths):]
    acc = h_ref[...]
    off = 0
    for y_ref, width in zip(ys, widths):
        acc = acc + jnp.dot(y_ref[...], w_ref[off:off + width, :], preferred_element_type=F32)
        off += width
    o_ref[...] = acc


def _out_proj(ys, w, h):
    t, d = h.shape
    tm = OUT_TM
    widths = tuple(int(y.shape[1]) for y in ys)
    blocks = sum(tm * wd * 2 for wd in widths) + w.size * 2 + 2 * tm * d * 4
    in_specs = [pl.BlockSpec((tm, wd), lambda i: (i, 0)) for wd in widths]
    in_specs += [pl.BlockSpec(w.shape, lambda i: (0, 0)), pl.BlockSpec((tm, d), lambda i: (i, 0))]
    return pl.pallas_call(
        functools.partial(_out_kernel, widths=widths),
        grid=(t // tm,),
        in_specs=in_specs,
        out_specs=pl.BlockSpec((tm, d), lambda i: (i, 0)),
        out_shape=jax.ShapeDtypeStruct((t, d), F32),
        compiler_params=pltpu.CompilerParams(
            dimension_semantics=("parallel",),
            vmem_limit_bytes=_vmem_limit(blocks, 0, 2 * tm * d * 4)),
        name="out_proj",
    )(*ys, w, h)


def _group_rms_inv(x, group):
    x2 = x * x
    hi = x2.astype(BF16)
    lo = (x2 - hi.astype(F32)).astype(BF16)
    shift = int(math.log2(group))
    gi = lax.broadcasted_iota(jnp.int32, (LANES, LANES), 0) >> shift
    gj = lax.broadcasted_iota(jnp.int32, (LANES, LANES), 1) >> shift
    member = (gi == gj).astype(BF16)
    total = (jnp.dot(hi, member, preferred_element_type=F32)
             + jnp.dot(lo, member, preferred_element_type=F32))
    return lax.rsqrt(total * (1.0 / group) + EPS)


def _split3(c):
    c1 = c.astype(BF16).astype(F32)
    r = c - c1
    c2 = r.astype(BF16).astype(F32)
    c3 = (r - c2).astype(BF16).astype(F32)
    return c1, c2, c3


def _bias_lanes(term, key_side):
    n = term.shape[1]
    parts = jnp.concatenate(_split3(term), axis=0)
    ones = jnp.ones((3, n), F32)
    pad = jnp.zeros((10, n), F32)
    rows = jnp.concatenate([parts, ones, pad] if key_side else [ones, parts, pad], axis=0)
    sel = (lax.broadcasted_iota(jnp.int32, (16, LANES), 0)
           == lax.broadcasted_iota(jnp.int32, (16, LANES), 1)).astype(BF16)
    lanes = lax.dot_general(rows.astype(BF16), sel, (((0,), (0,)), ((), ())),
                            preferred_element_type=F32)
    return lanes.astype(BF16)


def _score_phase(chains, c, t):
    krow = lax.broadcasted_iota(jnp.int32, (t, t), 0)
    qcol = lax.broadcasted_iota(jnp.int32, (t, t), 1)
    lo, hi = c * t, (c + 1) * t
    scores = []
    for qa, ka_ref, _ in chains:
        s_d = jnp.where(krow <= qcol, _nt_dot(ka_ref[lo:hi, :], qa), NEG)
        m = jnp.max(s_d, axis=0, keepdims=True)
        s_f = None
        if c > 0:
            s_f = _nt_dot(ka_ref[0:lo, :], qa)
            m = jnp.maximum(m, jnp.max(s_f, axis=0, keepdims=True))
        scores.append((s_d, s_f, m))
    return scores


def _value_phase(chains, scores, c, t):
    lo, hi = c * t, (c + 1) * t
    outs = []
    for (_, _, vt_ref), (s_d, s_f, m) in zip(chains, scores):
        dv = vt_ref.shape[0] - ONES_ROWS
        p_d = jnp.exp2(s_d - m).astype(BF16)
        acc = jnp.dot(vt_ref[:, lo:hi], p_d, preferred_element_type=F32)
        if c > 0:
            p_f = jnp.exp2(s_f - m).astype(BF16)
            acc = acc + jnp.dot(vt_ref[:, 0:lo], p_f, preferred_element_type=F32)
        outs.append((acc[:dv], acc[dv:dv + 1]))
    return outs


def _attend_all_tiles(make_chains, finish, n_tiles, t):
    chains = make_chains(0)
    scores = _score_phase(chains, 0, t)
    for c in range(n_tiles):
        if c + 1 < n_tiles:
            next_chains = make_chains(c + 1)
            next_scores = _score_phase(next_chains, c + 1, t)
        finish(c, _value_phase(chains, scores, c, t))
        if c + 1 < n_tiles:
            chains, scores = next_chains, next_scores


def _diff_kernel(slopes_ref, q_ref, k_ref, v_ref, z_ref, qg_ref, kg_ref, lam_ref, og_ref,
                 o_ref, ka_ref, vt_ref, *, lam_init, group):
    t = ATT_T
    hw = A_V_DIM
    seq = k_ref.shape[0]
    hg = pl.program_id(1)

    k_pos = lax.broadcasted_iota(jnp.int32, (1, seq), 1).astype(F32)
    for g in range(group):
        cs = slice(g * hw, (g + 1) * hw)
        k = k_ref[:, cs].astype(F32)
        ka_ref[g, :, :hw] = (k * _group_rms_inv(k, A_QK_DIM) * kg_ref[...]).astype(BF16)
        ka_ref[g, :, hw:] = _bias_lanes(LOG2E * slopes_ref[hg * group + g] * k_pos, True)
        vt_ref[g, :hw] = v_ref[:, cs].astype(F32).T.astype(BF16)
        vt_ref[g, hw:] = jnp.ones((ONES_ROWS, seq), BF16)

    lo = lax.broadcasted_iota(jnp.int32, (t, hw), 1) < A_QK_DIM
    lp = lam_ref[...]
    lam = (jnp.exp(jnp.sum(lp[0:1, :] * lp[1:2, :], axis=-1, keepdims=True))
           - jnp.exp(jnp.sum(lp[2:3, :] * lp[3:4, :], axis=-1, keepdims=True)) + lam_init)

    def make_chains(c):
        rows = slice(c * t, (c + 1) * t)
        q_pos = (c * t + lax.broadcasted_iota(jnp.int32, (1, t), 1)).astype(F32)
        chains = []
        for g in range(group):
            q = q_ref[rows, g * hw:(g + 1) * hw].astype(F32)
            qn = q * _group_rms_inv(q, A_QK_DIM) * qg_ref[...] * (LOG2E * A_QK_DIM ** -0.5)
            qb = _bias_lanes(-LOG2E * slopes_ref[hg * group + g] * q_pos, False)
            q1 = jnp.concatenate([jnp.where(lo, qn, 0.0).astype(BF16), qb], axis=1)
            q2 = jnp.concatenate([jnp.where(lo, 0.0, qn).astype(BF16), qb], axis=1)
            chains += [(q1, ka_ref.at[g], vt_ref.at[g]), (q2, ka_ref.at[g], vt_ref.at[g])]
        return chains

    def finish(c, outs):
        rows = slice(c * t, (c + 1) * t)
        for g in range(group):
            cs = slice(g * hw, (g + 1) * hw)
            (a1, l1), (a2, l2) = outs[2 * g], outs[2 * g + 1]
            o = (a1 / l1 - lam * (a2 / l2)).T
            ms = jnp.mean(o * o, axis=-1, keepdims=True)
            on = o * lax.rsqrt(ms + EPS) * og_ref[...] * (1.0 - lam_init)
            o_ref[rows, cs] = (on * _silu(z_ref[rows, cs].astype(F32))).astype(o_ref.dtype)

    _attend_all_tiles(make_chains, finish, seq // t, t)


def _diff_attention(proj, q_g, k_g, lam_params, out_g, lam_init, bsz, seq):
    hw = A_V_DIM
    grp = A_GROUP
    gw = grp * hw
    ng = A_HEADS // grp
    slopes = jnp.asarray(np.array([2.0 ** (-8.0 * (i + 1) / A_HEADS) for i in range(A_HEADS)],
                                  dtype=np.float32))
    qg2 = jnp.concatenate([q_g, q_g]).reshape(1, hw)
    kg2 = jnp.concatenate([k_g, k_g]).reshape(1, hw)
    blocks = 5 * seq * gw * 2
    scratch = grp * seq * 3 * hw * 2
    small = lambda shape: pl.BlockSpec(shape, lambda b, h: (0, 0))
    cols = lambda first: pl.BlockSpec((seq, gw), lambda b, h: (b, first * ng + h))
    return pl.pallas_call(
        functools.partial(_diff_kernel, lam_init=lam_init, group=grp),
        grid=(bsz, ng),
        in_specs=[
            pl.BlockSpec(memory_space=pltpu.SMEM),
            cols(0), cols(1), cols(2), cols(3),
            small((1, hw)), small((1, hw)), small((4, A_QK_DIM)), small((1, hw)),
        ],
        out_specs=cols(0),
        out_shape=jax.ShapeDtypeStruct((bsz * seq, A_WIDTH), BF16),
        scratch_shapes=[pltpu.VMEM((grp, seq, 2 * hw), BF16), pltpu.VMEM((grp, hw + ONES_ROWS, seq), BF16)],
        compiler_params=pltpu.CompilerParams(
            dimension_semantics=("parallel", "parallel"),
            vmem_limit_bytes=_vmem_limit(blocks, scratch, ATT_TEMP_BYTES)),
        name="diff_attn",
    )(slopes, proj, proj, proj, proj, qg2, kg2, lam_params, out_g.reshape(1, hw))


def _conv_kernel(h_ref, c_ref, hp_ref, cp_ref, b_ref, z_ref, w_ref, bias_ref, o_ref):
    first = pl.program_id(1) == 0
    u = c_ref[...].astype(F32) * h_ref[...].astype(F32)
    up = cp_ref[...].astype(F32) * hp_ref[...].astype(F32)
    up = jnp.where(first, 0.0, up)
    row = lax.broadcasted_iota(jnp.int32, u.shape, 0)
    u1 = jnp.where(row >= 1, pltpu.roll(u, 1, 0), up[7:8, :])
    u2 = jnp.where(row >= 2, pltpu.roll(u, 2, 0),
                   jnp.where(row == 1, up[7:8, :], up[6:7, :]))
    w = w_ref[...]
    y = w[0:1, :] * u2 + w[1:2, :] * u1 + w[2:3, :] * u + bias_ref[...]
    o_ref[...] = (b_ref[...].astype(F32) * y * _silu(z_ref[...].astype(F32))).astype(o_ref.dtype)


def _gated_conv(proj, w, b, bsz, seq):
    ts = CONV_TS
    ns = seq // ts
    cw = B_WIDTH
    base = 4 * A_WIDTH // cw
    halo = 8
    rows = lambda c: pl.BlockSpec((ts, cw), lambda bi, si: (bi * ns + si, base + c))
    prev = lambda c: pl.BlockSpec(
        (halo, cw), lambda bi, si: (jnp.maximum((bi * ns + si) * (ts // halo) - 1, 0), base + c))
    blocks = 5 * ts * cw * 2
    return pl.pallas_call(
        _conv_kernel,
        grid=(bsz, ns),
        in_specs=[rows(0), rows(1), prev(0), prev(1), rows(2), rows(3),
                  pl.BlockSpec((CONV_W, cw), lambda bi, si: (0, 0)),
                  pl.BlockSpec((1, cw), lambda bi, si: (0, 0))],
        out_specs=pl.BlockSpec((ts, cw), lambda bi, si: (bi * ns + si, 0)),
        out_shape=jax.ShapeDtypeStruct((bsz * seq, cw), BF16),
        compiler_params=pltpu.CompilerParams(
            dimension_semantics=("parallel", "parallel"),
            vmem_limit_bytes=_vmem_limit(blocks, 0, 8 * ts * cw * 4)),
        name="gated_conv",
    )(proj, proj, proj, proj, proj, proj, w, b.reshape(1, cw))


def _mem_kv_kernel(mem_ref, g_ref, w_ref, kg_ref, o_ref):
    x = mem_ref[...]
    ms = jnp.mean(x * x, axis=-1, keepdims=True)
    xn = (x * lax.rsqrt(ms + EPS) * g_ref[...]).astype(BF16)
    kv = jnp.dot(xn, w_ref[...], preferred_element_type=F32)
    for hd in range(X_HEADS):
        k = kv[:, hd * X_DIM:(hd + 1) * X_DIM]
        kms = jnp.mean(k * k, axis=-1, keepdims=True)
        o_ref[:, hd * X_DIM:(hd + 1) * X_DIM] = (k * lax.rsqrt(kms + EPS) * kg_ref[...]).astype(BF16)
    o_ref[:, X_WIDTH:] = kv[:, X_WIDTH:].astype(BF16)


def _mem_kv(mem2d, g, w, k_g, bsz):
    d = mem2d.shape[1]
    blocks = MEM_LEN * d * 4 + w.size * 2 + MEM_LEN * 2 * X_WIDTH * 2
    return pl.pallas_call(
        _mem_kv_kernel,
        grid=(bsz,),
        in_specs=[pl.BlockSpec((MEM_LEN, d), lambda b: (b, 0)),
                  pl.BlockSpec((1, d), lambda b: (0, 0)),
                  pl.BlockSpec(w.shape, lambda b: (0, 0)),
                  pl.BlockSpec((1, X_DIM), lambda b: (0, 0))],
        out_specs=pl.BlockSpec((MEM_LEN, 2 * X_WIDTH), lambda b: (b, 0)),
        out_shape=jax.ShapeDtypeStruct((bsz * MEM_LEN, 2 * X_WIDTH), BF16),
        compiler_params=pltpu.CompilerParams(
            dimension_semantics=("parallel",),
            vmem_limit_bytes=_vmem_limit(blocks, 0, 4 * MEM_LEN * d * 4)),
        name="mem_kv",
    )(mem2d, g.reshape(1, d), w, k_g.reshape(1, X_DIM))


def _xattn_kernel(q_ref, z_ref, kv_ref, qg_ref, o_ref):
    for hd in range(X_HEADS):
        cs = slice(hd * X_DIM, (hd + 1) * X_DIM)
        q = q_ref[:, cs].astype(F32)
        ms = jnp.mean(q * q, axis=-1, keepdims=True)
        qn = (q * lax.rsqrt(ms + EPS) * qg_ref[...] * (X_DIM ** -0.5)).astype(BF16)
        s = _nt_dot(qn, kv_ref[:, cs])
        m = jnp.max(s, axis=-1, keepdims=True)
        p = jnp.exp(s - m)
        l = jnp.sum(p, axis=-1, keepdims=True)
        vs = slice(X_WIDTH + hd * X_DIM, X_WIDTH + (hd + 1) * X_DIM)
        o = jnp.dot(p.astype(BF16), kv_ref[:, vs], preferred_element_type=F32) / l
        o_ref[:, cs] = (o * _silu(z_ref[:, cs].astype(F32))).astype(o_ref.dtype)


def _mem_xattn(proj, kv, q_g, q_col_block, bsz, seq):
    tq = X_TQ
    nq = seq // tq
    blocks = 3 * tq * X_WIDTH * 2 + MEM_LEN * 2 * X_WIDTH * 2
    return pl.pallas_call(
        _xattn_kernel,
        grid=(bsz, nq),
        in_specs=[pl.BlockSpec((tq, X_WIDTH), lambda b, i: (b * nq + i, q_col_block)),
                  pl.BlockSpec((tq, X_WIDTH), lambda b, i: (b * nq + i, q_col_block + 1)),
                  pl.BlockSpec((MEM_LEN, 2 * X_WIDTH), lambda b, i: (b, 0)),
                  pl.BlockSpec((1, X_DIM), lambda b, i: (0, 0))],
        out_specs=pl.BlockSpec((tq, X_WIDTH), lambda b, i: (b * nq + i, 0)),
        out_shape=jax.ShapeDtypeStruct((bsz * seq, X_WIDTH), BF16),
        compiler_params=pltpu.CompilerParams(
            dimension_semantics=("parallel", "parallel"),
            vmem_limit_bytes=_vmem_limit(blocks, 0, 8 * tq * MEM_LEN * 4)),
        name="mem_xattn",
    )(proj, proj, kv, q_g.reshape(1, X_DIM))


def _forget_cum_kernel(f_ref, b_ref, cum_ref):
    x = f_ref[...] + b_ref[...]
    c = jnp.minimum(x, 0.0) - jnp.log1p(jnp.exp(-jnp.abs(x)))
    n = c.shape[0]
    row = lax.broadcasted_iota(jnp.int32, c.shape, 0)
    shift = 1
    while shift < n:
        c = c + jnp.where(row >= shift, pltpu.roll(c, shift, 0), 0.0)
        shift *= 2
    cum_ref[...] = c.T[:cum_ref.shape[0], :]


def _forget_cum(f_logits, f_bias, bsz, seq):
    bias = jnp.zeros((1, LANES), F32).at[0, :C_HEADS].set(f_bias)
    blocks = seq * LANES * 4 + CUM_ROWS * seq * 4
    return pl.pallas_call(
        _forget_cum_kernel,
        grid=(bsz,),
        in_specs=[pl.BlockSpec((seq, LANES), lambda b: (b, 0)),
                  pl.BlockSpec((1, LANES), lambda b: (0, 0))],
        out_specs=pl.BlockSpec((CUM_ROWS, seq), lambda b: (b, 0)),
        out_shape=jax.ShapeDtypeStruct((bsz * CUM_ROWS, seq), F32),
        compiler_params=pltpu.CompilerParams(
            dimension_semantics=("parallel",),
            vmem_limit_bytes=_vmem_limit(blocks, 0, 6 * seq * LANES * 4)),
        name="forget_cum",
    )(f_logits, bias)


def _fox_kernel(q_ref, k_ref, v_ref, z_ref, cum_ref, qg_ref, kg_ref, o_ref, ka_ref, vt_ref,
                *, group):
    t = ATT_T
    hw = C_DIM
    seq = k_ref.shape[0]

    for g in range(group):
        cs = slice(g * hw, (g + 1) * hw)
        k = k_ref[:, cs].astype(F32)
        ka_ref[g, :, :hw] = (k * _group_rms_inv(k, C_DIM) * kg_ref[...]).astype(BF16)
        ka_ref[g, :, hw:] = _bias_lanes(-LOG2E * cum_ref[g], True)
        vt_ref[g, :hw] = v_ref[:, cs].astype(F32).T.astype(BF16)
        vt_ref[g, hw:] = jnp.ones((ONES_ROWS, seq), BF16)

    def make_chains(c):
        rows = slice(c * t, (c + 1) * t)
        chains = []
        for g in range(group):
            q = q_ref[rows, g * hw:(g + 1) * hw].astype(F32)
            qn = (q * _group_rms_inv(q, C_DIM) * qg_ref[...] * (LOG2E * C_DIM ** -0.5)).astype(BF16)
            qb = _bias_lanes(LOG2E * cum_ref[g, :, rows], False)
            chains.append((jnp.concatenate([qn, qb], axis=1), ka_ref.at[g], vt_ref.at[g]))
        return chains

    def finish(c, outs):
        rows = slice(c * t, (c + 1) * t)
        for g, (acc, l) in enumerate(outs):
            cs = slice(g * hw, (g + 1) * hw)
            o_ref[rows, cs] = ((acc / l).T * _silu(z_ref[rows, cs].astype(F32))).astype(o_ref.dtype)

    _attend_all_tiles(make_chains, finish, seq // t, t)


def _fox_attention(proj, cum, q_g, k_g, bsz, seq):
    hw = C_DIM
    grp = C_GROUP
    gw = grp * hw
    ng = C_HEADS // grp
    blocks = 5 * seq * gw * 2 + grp * 8 * seq * 4
    scratch = grp * seq * 3 * hw * 2
    cum3 = cum.reshape(bsz * CUM_ROWS, 1, seq)
    small = lambda shape: pl.BlockSpec(shape, lambda b, h: (0, 0))
    cols = lambda first: pl.BlockSpec((seq, gw), lambda b, h: (b, first * ng + h))
    return pl.pallas_call(
        functools.partial(_fox_kernel, group=grp),
        grid=(bsz, ng),
        in_specs=[
            cols(0), cols(1), cols(2), cols(3),
            pl.BlockSpec((grp, 1, seq), lambda b, h: (b * (CUM_ROWS // grp) + h, 0, 0)),
            small((1, hw)), small((1, hw)),
        ],
        out_specs=cols(0),
        out_shape=jax.ShapeDtypeStruct((bsz * seq, C_WIDTH), BF16),
        scratch_shapes=[pltpu.VMEM((grp, seq, 2 * hw), BF16), pltpu.VMEM((grp, hw + ONES_ROWS, seq), BF16)],
        compiler_params=pltpu.CompilerParams(
            dimension_semantics=("parallel", "parallel"),
            vmem_limit_bytes=_vmem_limit(blocks, scratch, ATT_TEMP_BYTES)),
        name="fox_attn",
    )(proj, proj, proj, proj, cum3, q_g.reshape(1, hw), k_g.reshape(1, hw))


def _even_layer(h, mem2d, layer, bsz, seq, norm_g, w_in, w_out, a_qn, a_kn, a_lam, a_on,
                b_cw, b_cb, x_qn, x_kn, mem_g, w_mem_kv):
    lam_init = 0.8 - 0.6 * math.exp(-0.3 * layer)
    proj = _norm_proj(h, norm_g, w_in.astype(BF16))
    ya = _diff_attention(proj, a_qn, a_kn, a_lam, a_on, lam_init, bsz, seq)
    yb = _gated_conv(proj, b_cw, b_cb, bsz, seq)
    kv = _mem_kv(mem2d, mem_g, w_mem_kv.astype(BF16), x_kn, bsz)
    yx = _mem_xattn(proj, kv, x_qn, (4 * A_WIDTH + 4 * B_WIDTH) // X_WIDTH, bsz, seq)
    return _out_proj([ya, yb, yx], w_out.astype(BF16), h)


def _odd_layer(h, mem2d, bsz, seq, norm_g, w_in, w_out, c_qn, c_kn, c_fb, x_qn, x_kn,
               mem_g, w_mem_kv):
    main = 4 * C_WIDTH
    w_t = w_in.T
    w_main = jnp.concatenate([w_t[:main], w_t[main + C_HEADS:]], axis=0).astype(BF16)
    w_f = jnp.pad(w_t[main:main + C_HEADS], ((0, LANES - C_HEADS), (0, 0))).astype(BF16)
    proj, f_logits = _norm_proj(h, norm_g, w_main, w_f, w_rows_are_outputs=True)
    cum = _forget_cum(f_logits, c_fb, bsz, seq)
    yc = _fox_attention(proj, cum, c_qn, c_kn, bsz, seq)
    kv = _mem_kv(mem2d, mem_g, w_mem_kv.astype(BF16), x_kn, bsz)
    yx = _mem_xattn(proj, kv, x_qn, main // X_WIDTH, bsz, seq)
    return _out_proj([yc, yx], w_out.astype(BF16), h)


def kernel(x, mem, e_norm_g, e_w_in, e_w_out, e_a_q_norm_g, e_a_k_norm_g, e_a_lambda,
           e_a_out_norm_g, e_b_conv_w, e_b_conv_b, e_x_q_norm_g, e_x_k_norm_g, e_mem_norm_g,
           e_w_mem_kv, o_norm_g, o_w_in, o_w_out, o_c_q_norm_g, o_c_k_norm_g, o_c_forget_b,
           o_x_q_norm_g, o_x_k_norm_g, o_mem_norm_g, o_w_mem_kv):
    bsz, seq, d = x.shape
    h = x.reshape(bsz * seq, d)
    mem2d = mem.reshape(bsz * MEM_LEN, d)
    depth = e_w_in.shape[0] + o_w_in.shape[0]
    for layer in range(depth):
        i = layer // 2
        if layer % 2 == 0:
            h = _even_layer(h, mem2d, layer, bsz, seq, e_norm_g[i], e_w_in[i], e_w_out[i],
                            e_a_q_norm_g[i], e_a_k_norm_g[i], e_a_lambda[i], e_a_out_norm_g[i],
                            e_b_conv_w[i], e_b_conv_b[i], e_x_q_norm_g[i], e_x_k_norm_g[i],
                            e_mem_norm_g[i], e_w_mem_kv[i])
        else:
            h = _odd_layer(h, mem2d, bsz, seq, o_norm_g[i], o_w_in[i], o_w_out[i],
                           o_c_q_norm_g[i], o_c_k_norm_g[i], o_c_forget_b[i],
                           o_x_q_norm_g[i], o_x_k_norm_g[i], o_mem_norm_g[i], o_w_mem_kv[i])
    return h.reshape(bsz, seq, d)
```

```python
import functools
import math

import numpy as np
import jax
import jax.numpy as jnp
from jax import lax
from jax.experimental import pallas as pl
from jax.experimental.pallas import tpu as pltpu

F32 = jnp.float32
BF16 = jnp.bfloat16

D_MODEL = 1024
MEM_LEN = 256
EPS = 1e-6
NEG = -1e30
LOG2E = math.log2(math.e)

A_HEADS = 8
A_QK_DIM = 64
A_V_DIM = 128
A_WIDTH = 1024
B_WIDTH = 512
CONV_W = 3
C_HEADS = 12
C_DIM = 128
C_WIDTH = 1536
X_HEADS = 4
X_DIM = 128
X_WIDTH = 512
MIX = 2048

LANES = 128
ONES_ROWS = 16
CUM_ROWS = 16
VMEM_CAP = 56 * 1024 * 1024

PROJ_TM = 1024
PROJ_TN = 1792
OUT_TM = 512
HALO = 8
ATT_T = 256
A_GROUP = 2
C_GROUP = 4
ATT_TEMP_BYTES = 20 << 20


def _vmem_limit(block_bytes, scratch_bytes=0, temp_bytes=0):
    need = 2 * block_bytes + scratch_bytes + temp_bytes + (4 << 20)
    return int(min(max(need, 16 << 20), VMEM_CAP))


def _silu(z):
    return z * jax.nn.sigmoid(z)


def _nt_dot(a, b):
    return lax.dot_general(a, b, (((1,), (1,)), ((), ())), preferred_element_type=F32)


def _proj_kernel(*refs, has_side, w_rows_are_outputs):
    if has_side:
        x_ref, g_ref, w_ref, ws_ref, o_ref, side_ref, xn_ref = refs
    else:
        x_ref, g_ref, w_ref, o_ref, xn_ref = refs
    matmul = _nt_dot if w_rows_are_outputs else functools.partial(jnp.dot, preferred_element_type=F32)

    @pl.when(pl.program_id(1) == 0)
    def _():
        x = x_ref[...]
        ms = jnp.mean(x * x, axis=-1, keepdims=True)
        xn = (x * lax.rsqrt(ms + EPS) * g_ref[...]).astype(BF16)
        xn_ref[...] = xn
        if has_side:
            side_ref[...] = matmul(xn, ws_ref[...])

    o_ref[...] = matmul(xn_ref[...], w_ref[...]).astype(o_ref.dtype)


def _norm_proj(h, g, w, w_side=None, w_rows_are_outputs=False):
    t, d = h.shape
    n = w.shape[0] if w_rows_are_outputs else w.shape[1]
    tm, tn = PROJ_TM, PROJ_TN
    grid = (t // tm, n // tn)
    blocks = tm * d * 4 + d * tn * 2 + tm * tn * 2
    w_spec = (pl.BlockSpec((tn, d), lambda i, j: (j, 0)) if w_rows_are_outputs
              else pl.BlockSpec((d, tn), lambda i, j: (0, j)))
    in_specs = [
        pl.BlockSpec((tm, d), lambda i, j: (i, 0)),
        pl.BlockSpec((1, d), lambda i, j: (0, 0)),
        w_spec,
    ]
    out_specs = pl.BlockSpec((tm, tn), lambda i, j: (i, j))
    out_shape = jax.ShapeDtypeStruct((t, n), BF16)
    args = [h, g.reshape(1, d), w]
    if w_side is not None:
        ns = w_side.shape[0] if w_rows_are_outputs else w_side.shape[1]
        in_specs.append(pl.BlockSpec(w_side.shape, lambda i, j: (0, 0)))
        out_specs = [out_specs, pl.BlockSpec((tm, ns), lambda i, j: (i, 0))]
        out_shape = [out_shape, jax.ShapeDtypeStruct((t, ns), F32)]
        args.append(w_side)
        blocks += d * ns * 2 + tm * ns * 4
    return pl.pallas_call(
        functools.partial(_proj_kernel, has_side=w_side is not None,
                          w_rows_are_outputs=w_rows_are_outputs),
        grid=grid,
        in_specs=in_specs,
        out_specs=out_specs,
        out_shape=out_shape,
        scratch_shapes=[pltpu.VMEM((tm, d), BF16)],
        compiler_params=pltpu.CompilerParams(
            dimension_semantics=("parallel", "arbitrary"),
            vmem_limit_bytes=_vmem_limit(blocks, tm * d * 2, tm * tn * 4 + tm * d * 4)),
        name="norm_proj",
    )(*args)


def _group_rms_inv(x, group):
    x2 = x * x
    hi = x2.astype(BF16)
    lo = (x2 - hi.astype(F32)).astype(BF16)
    shift = int(math.log2(group))
    gi = lax.broadcasted_iota(jnp.int32, (LANES, LANES), 0) >> shift
    gj = lax.broadcasted_iota(jnp.int32, (LANES, LANES), 1) >> shift
    member = (gi == gj).astype(BF16)
    total = (jnp.dot(hi, member, preferred_element_type=F32)
             + jnp.dot(lo, member, preferred_element_type=F32))
    return lax.rsqrt(total * (1.0 / group) + EPS)


def _split3(c):
    c1 = c.astype(BF16).astype(F32)
    r = c - c1
    c2 = r.astype(BF16).astype(F32)
    c3 = (r - c2).astype(BF16).astype(F32)
    return c1, c2, c3


def _bias_lanes(term, key_side):
    n = term.shape[1]
    parts = jnp.concatenate(_split3(term), axis=0)
    ones = jnp.ones((3, n), F32)
    pad = jnp.zeros((10, n), F32)
    rows = jnp.concatenate([parts, ones, pad] if key_side else [ones, parts, pad], axis=0)
    sel = (lax.broadcasted_iota(jnp.int32, (16, LANES), 0)
           == lax.broadcasted_iota(jnp.int32, (16, LANES), 1)).astype(BF16)
    lanes = lax.dot_general(rows.astype(BF16), sel, (((0,), (0,)), ((), ())),
                            preferred_element_type=F32)
    return lanes.astype(BF16)


def _score_phase(chains, c, t):
    krow = lax.broadcasted_iota(jnp.int32, (t, t), 0)
    qcol = lax.broadcasted_iota(jnp.int32, (t, t), 1)
    lo, hi = c * t, (c + 1) * t
    scores = []
    for qa, ka_ref, _ in chains:
        s_d = jnp.where(krow <= qcol, _nt_dot(ka_ref[lo:hi, :], qa), NEG)
        m = jnp.max(s_d, axis=0, keepdims=True)
        s_f = None
        if c > 0:
            s_f = _nt_dot(ka_ref[0:lo, :], qa)
            m = jnp.maximum(m, jnp.max(s_f, axis=0, keepdims=True))
        scores.append((s_d, s_f, m))
    return scores


def _value_phase(chains, scores, c, t):
    lo, hi = c * t, (c + 1) * t
    outs = []
    for (_, _, vt_ref), (s_d, s_f, m) in zip(chains, scores):
        dv = vt_ref.shape[0] - ONES_ROWS
        p_d = jnp.exp2(s_d - m).astype(BF16)
        acc = jnp.dot(vt_ref[:, lo:hi], p_d, preferred_element_type=F32)
        if c > 0:
            p_f = jnp.exp2(s_f - m).astype(BF16)
            acc = acc + jnp.dot(vt_ref[:, 0:lo], p_f, preferred_element_type=F32)
        outs.append((acc[:dv], acc[dv:dv + 1]))
    return outs


def _attend_all_tiles(make_chains, finish, n_tiles, t):
    chains = make_chains(0)
    scores = _score_phase(chains, 0, t)
    for c in range(n_tiles):
        if c + 1 < n_tiles:
            next_chains = make_chains(c + 1)
            next_scores = _score_phase(next_chains, c + 1, t)
        finish(c, _value_phase(chains, scores, c, t))
        if c + 1 < n_tiles:
            chains, scores = next_chains, next_scores


def _diff_kernel(slopes_ref, q_ref, k_ref, v_ref, z_ref, qg_ref, kg_ref, lam_ref, og_ref,
                 o_ref, ka_ref, vt_ref, *, lam_init, group):
    t = ATT_T
    hw = A_V_DIM
    seq = k_ref.shape[0]
    hg = pl.program_id(1)

    k_pos = lax.broadcasted_iota(jnp.int32, (1, seq), 1).astype(F32)
    for g in range(group):
        cs = slice(g * hw, (g + 1) * hw)
        k = k_ref[:, cs].astype(F32)
        ka_ref[g, :, :hw] = (k * _group_rms_inv(k, A_QK_DIM) * kg_ref[...]).astype(BF16)
        ka_ref[g, :, hw:] = _bias_lanes(LOG2E * slopes_ref[hg * group + g] * k_pos, True)
        vt_ref[g, :hw] = v_ref[:, cs].astype(F32).T.astype(BF16)
        vt_ref[g, hw:] = jnp.ones((ONES_ROWS, seq), BF16)

    lo = lax.broadcasted_iota(jnp.int32, (t, hw), 1) < A_QK_DIM
    lp = lam_ref[...]
    lam = (jnp.exp(jnp.sum(lp[0:1, :] * lp[1:2, :], axis=-1, keepdims=True))
           - jnp.exp(jnp.sum(lp[2:3, :] * lp[3:4, :], axis=-1, keepdims=True)) + lam_init)

    def make_chains(c):
        rows = slice(c * t, (c + 1) * t)
        q_pos = (c * t + lax.broadcasted_iota(jnp.int32, (1, t), 1)).astype(F32)
        chains = []
        for g in range(group):
            q = q_ref[rows, g * hw:(g + 1) * hw].astype(F32)
            qn = q * _group_rms_inv(q, A_QK_DIM) * qg_ref[...] * (LOG2E * A_QK_DIM ** -0.5)
            qb = _bias_lanes(-LOG2E * slopes_ref[hg * group + g] * q_pos, False)
            q1 = jnp.concatenate([jnp.where(lo, qn, 0.0).astype(BF16), qb], axis=1)
            q2 = jnp.concatenate([jnp.where(lo, 0.0, qn).astype(BF16), qb], axis=1)
            chains += [(q1, ka_ref.at[g], vt_ref.at[g]), (q2, ka_ref.at[g], vt_ref.at[g])]
        return chains

    def finish(c, outs):
        rows = slice(c * t, (c + 1) * t)
        for g in range(group):
            cs = slice(g * hw, (g + 1) * hw)
            (a1, l1), (a2, l2) = outs[2 * g], outs[2 * g + 1]
            o = (a1 / l1 - lam * (a2 / l2)).T
            ms = jnp.mean(o * o, axis=-1, keepdims=True)
            on = o * lax.rsqrt(ms + EPS) * og_ref[...] * (1.0 - lam_init)
            o_ref[rows, cs] = (on * _silu(z_ref[rows, cs].astype(F32))).astype(o_ref.dtype)

    _attend_all_tiles(make_chains, finish, seq // t, t)


def _diff_attention(proj, q_g, k_g, lam_params, out_g, lam_init, bsz, seq):
    hw = A_V_DIM
    grp = A_GROUP
    gw = grp * hw
    ng = A_HEADS // grp
    slopes = jnp.asarray(np.array([2.0 ** (-8.0 * (i + 1) / A_HEADS) for i in range(A_HEADS)],
                                  dtype=np.float32))
    qg2 = jnp.concatenate([q_g, q_g]).reshape(1, hw)
    kg2 = jnp.concatenate([k_g, k_g]).reshape(1, hw)
    blocks = 5 * seq * gw * 2
    scratch = grp * seq * 3 * hw * 2
    small = lambda shape: pl.BlockSpec(shape, lambda b, h: (0, 0))
    cols = lambda first: pl.BlockSpec((seq, gw), lambda b, h: (b, first * ng + h))
    return pl.pallas_call(
        functools.partial(_diff_kernel, lam_init=lam_init, group=grp),
        grid=(bsz, ng),
        in_specs=[
            pl.BlockSpec(memory_space=pltpu.SMEM),
            cols(0), cols(1), cols(2), cols(3),
            small((1, hw)), small((1, hw)), small((4, A_QK_DIM)), small((1, hw)),
        ],
        out_specs=cols(0),
        out_shape=jax.ShapeDtypeStruct((bsz * seq, A_WIDTH), BF16),
        scratch_shapes=[pltpu.VMEM((grp, seq, 2 * hw), BF16), pltpu.VMEM((grp, hw + ONES_ROWS, seq), BF16)],
        compiler_params=pltpu.CompilerParams(
            dimension_semantics=("parallel", "parallel"),
            vmem_limit_bytes=_vmem_limit(blocks, scratch, ATT_TEMP_BYTES)),
        name="diff_attn",
    )(slopes, proj, proj, proj, proj, qg2, kg2, lam_params, out_g.reshape(1, hw))


def _conv_gated(h_ref, c_ref, hp_ref, cp_ref, b_ref, z_ref, w_ref, bias_ref, starts_sequence):
    u = c_ref[...].astype(F32) * h_ref[...].astype(F32)
    up = cp_ref[...].astype(F32) * hp_ref[...].astype(F32)
    up = jnp.where(starts_sequence, 0.0, up)
    row = lax.broadcasted_iota(jnp.int32, u.shape, 0)
    u1 = jnp.where(row >= 1, pltpu.roll(u, 1, 0), up[HALO - 1:HALO, :])
    u2 = jnp.where(row >= 2, pltpu.roll(u, 2, 0),
                   jnp.where(row == 1, up[HALO - 1:HALO, :], up[HALO - 2:HALO - 1, :]))
    w = w_ref[...]
    y = w[0:1, :] * u2 + w[1:2, :] * u1 + w[2:3, :] * u + bias_ref[...]
    return (b_ref[...].astype(F32) * y * _silu(z_ref[...].astype(F32))).astype(BF16)


def _mem_kv_kernel(mem_ref, g_ref, w_ref, kg_ref, o_ref):
    x = mem_ref[...]
    ms = jnp.mean(x * x, axis=-1, keepdims=True)
    xn = (x * lax.rsqrt(ms + EPS) * g_ref[...]).astype(BF16)
    kv = jnp.dot(xn, w_ref[...], preferred_element_type=F32)
    for hd in range(X_HEADS):
        k = kv[:, hd * X_DIM:(hd + 1) * X_DIM]
        kms = jnp.mean(k * k, axis=-1, keepdims=True)
        o_ref[:, hd * X_DIM:(hd + 1) * X_DIM] = (k * lax.rsqrt(kms + EPS) * kg_ref[...]).astype(BF16)
    o_ref[:, X_WIDTH:] = kv[:, X_WIDTH:].astype(BF16)


def _mem_kv(mem2d, g, w, k_g, bsz):
    d = mem2d.shape[1]
    blocks = MEM_LEN * d * 4 + w.size * 2 + MEM_LEN * 2 * X_WIDTH * 2
    return pl.pallas_call(
        _mem_kv_kernel,
        grid=(bsz,),
        in_specs=[pl.BlockSpec((MEM_LEN, d), lambda b: (b, 0)),
                  pl.BlockSpec((1, d), lambda b: (0, 0)),
                  pl.BlockSpec(w.shape, lambda b: (0, 0)),
                  pl.BlockSpec((1, X_DIM), lambda b: (0, 0))],
        out_specs=pl.BlockSpec((MEM_LEN, 2 * X_WIDTH), lambda b: (b, 0)),
        out_shape=jax.ShapeDtypeStruct((bsz * MEM_LEN, 2 * X_WIDTH), BF16),
        compiler_params=pltpu.CompilerParams(
            dimension_semantics=("parallel",),
            vmem_limit_bytes=_vmem_limit(blocks, 0, 4 * MEM_LEN * d * 4)),
        name="mem_kv",
    )(mem2d, g.reshape(1, d), w, k_g.reshape(1, X_DIM))


def _xattn_scores(q_ref, kv_ref, qg_ref):
    scores = []
    for hd in range(X_HEADS):
        cs = slice(hd * X_DIM, (hd + 1) * X_DIM)
        q = q_ref[:, cs].astype(F32)
        ms = jnp.mean(q * q, axis=-1, keepdims=True)
        qn = (q * lax.rsqrt(ms + EPS) * qg_ref[...] * (X_DIM ** -0.5)).astype(BF16)
        scores.append(_nt_dot(qn, kv_ref[:, cs]))
    return scores


def _xattn_gated(scores, z_ref, kv_ref):
    heads = []
    for hd, s in enumerate(scores):
        cs = slice(hd * X_DIM, (hd + 1) * X_DIM)
        m = jnp.max(s, axis=-1, keepdims=True)
        p = jnp.exp(s - m)
        l = jnp.sum(p, axis=-1, keepdims=True)
        vs = slice(X_WIDTH + hd * X_DIM, X_WIDTH + (hd + 1) * X_DIM)
        o = jnp.dot(p.astype(BF16), kv_ref[:, vs], preferred_element_type=F32) / l
        heads.append((o * _silu(z_ref[:, cs].astype(F32))).astype(BF16))
    return jnp.concatenate(heads, axis=1)


def _forget_cum_kernel(f_ref, b_ref, cum_ref):
    x = f_ref[...] + b_ref[...]
    c = jnp.minimum(x, 0.0) - jnp.log1p(jnp.exp(-jnp.abs(x)))
    n = c.shape[0]
    row = lax.broadcasted_iota(jnp.int32, c.shape, 0)
    shift = 1
    while shift < n:
        c = c + jnp.where(row >= shift, pltpu.roll(c, shift, 0), 0.0)
        shift *= 2
    cum_ref[...] = c.T[:cum_ref.shape[0], :]


def _forget_cum(f_logits, f_bias, bsz, seq):
    bias = jnp.zeros((1, LANES), F32).at[0, :C_HEADS].set(f_bias)
    blocks = seq * LANES * 4 + CUM_ROWS * seq * 4
    return pl.pallas_call(
        _forget_cum_kernel,
        grid=(bsz,),
        in_specs=[pl.BlockSpec((seq, LANES), lambda b: (b, 0)),
                  pl.BlockSpec((1, LANES), lambda b: (0, 0))],
        out_specs=pl.BlockSpec((CUM_ROWS, seq), lambda b: (b, 0)),
        out_shape=jax.ShapeDtypeStruct((bsz * CUM_ROWS, seq), F32),
        compiler_params=pltpu.CompilerParams(
            dimension_semantics=("parallel",),
            vmem_limit_bytes=_vmem_limit(blocks, 0, 6 * seq * LANES * 4)),
        name="forget_cum",
    )(f_logits, bias)


def _fox_kernel(q_ref, k_ref, v_ref, z_ref, cum_ref, qg_ref, kg_ref, o_ref, ka_ref, vt_ref,
                *, group):
    t = ATT_T
    hw = C_DIM
    seq = k_ref.shape[0]

    for g in range(group):
        cs = slice(g * hw, (g + 1) * hw)
        k = k_ref[:, cs].astype(F32)
        ka_ref[g, :, :hw] = (k * _group_rms_inv(k, C_DIM) * kg_ref[...]).astype(BF16)
        ka_ref[g, :, hw:] = _bias_lanes(-LOG2E * cum_ref[g], True)
        vt_ref[g, :hw] = v_ref[:, cs].astype(F32).T.astype(BF16)
        vt_ref[g, hw:] = jnp.ones((ONES_ROWS, seq), BF16)

    def make_chains(c):
        rows = slice(c * t, (c + 1) * t)
        chains = []
        for g in range(group):
            q = q_ref[rows, g * hw:(g + 1) * hw].astype(F32)
            qn = (q * _group_rms_inv(q, C_DIM) * qg_ref[...] * (LOG2E * C_DIM ** -0.5)).astype(BF16)
            qb = _bias_lanes(LOG2E * cum_ref[g, :, rows], False)
            chains.append((jnp.concatenate([qn, qb], axis=1), ka_ref.at[g], vt_ref.at[g]))
        return chains

    def finish(c, outs):
        rows = slice(c * t, (c + 1) * t)
        for g, (acc, l) in enumerate(outs):
            cs = slice(g * hw, (g + 1) * hw)
            o_ref[rows, cs] = ((acc / l).T * _silu(z_ref[rows, cs].astype(F32))).astype(o_ref.dtype)

    _attend_all_tiles(make_chains, finish, seq // t, t)


def _fox_attention(proj, cum, q_g, k_g, bsz, seq):
    hw = C_DIM
    grp = C_GROUP
    gw = grp * hw
    ng = C_HEADS // grp
    blocks = 5 * seq * gw * 2 + grp * 8 * seq * 4
    scratch = grp * seq * 3 * hw * 2
    cum3 = cum.reshape(bsz * CUM_ROWS, 1, seq)
    small = lambda shape: pl.BlockSpec(shape, lambda b, h: (0, 0))
    cols = lambda first: pl.BlockSpec((seq, gw), lambda b, h: (b, first * ng + h))
    return pl.pallas_call(
        functools.partial(_fox_kernel, group=grp),
        grid=(bsz, ng),
        in_specs=[
            cols(0), cols(1), cols(2), cols(3),
            pl.BlockSpec((grp, 1, seq), lambda b, h: (b * (CUM_ROWS // grp) + h, 0, 0)),
            small((1, hw)), small((1, hw)),
        ],
        out_specs=cols(0),
        out_shape=jax.ShapeDtypeStruct((bsz * seq, C_WIDTH), BF16),
        scratch_shapes=[pltpu.VMEM((grp, seq, 2 * hw), BF16), pltpu.VMEM((grp, hw + ONES_ROWS, seq), BF16)],
        compiler_params=pltpu.CompilerParams(
            dimension_semantics=("parallel", "parallel"),
            vmem_limit_bytes=_vmem_limit(blocks, scratch, ATT_TEMP_BYTES)),
        name="fox_attn",
    )(proj, proj, proj, proj, cum3, q_g.reshape(1, hw), k_g.reshape(1, hw))


def _mix_out_kernel(*refs, has_conv, tiles_per_seq):
    refs = list(refs)
    attn_ref = refs.pop(0)
    conv_refs = [refs.pop(0) for _ in range(8)] if has_conv else None
    xq_ref, xz_ref, kv_ref, xqg_ref, w_ref, h_ref, o_ref = refs

    scores = _xattn_scores(xq_ref, kv_ref, xqg_ref)
    width = attn_ref.shape[1]
    acc = h_ref[...] + jnp.dot(attn_ref[...], w_ref[:width, :], preferred_element_type=F32)
    if has_conv:
        starts_sequence = pl.program_id(0) % tiles_per_seq == 0
        yb = _conv_gated(*conv_refs, starts_sequence)
        acc = acc + jnp.dot(yb, w_ref[width:width + B_WIDTH, :], preferred_element_type=F32)
        width += B_WIDTH
    yx = _xattn_gated(scores, xz_ref, kv_ref)
    o_ref[...] = acc + jnp.dot(yx, w_ref[width:, :], preferred_element_type=F32)


def _mix_out(y_attn, proj, kv, x_qg, w_out, h, seq, x_col_block, conv=None):
    t, d = h.shape
    tm = OUT_TM
    tiles_per_seq = seq // tm
    aw = y_attn.shape[1]
    rows = lambda width, col: pl.BlockSpec((tm, width), lambda i: (i, col))
    in_specs = [rows(aw, 0)]
    args = [y_attn]
    blocks = tm * aw * 2 + 2 * tm * X_WIDTH * 2 + MEM_LEN * 2 * X_WIDTH * 2 + w_out.size * 2 + 2 * tm * d * 4
    if conv is not None:
        cw, cb, base = conv
        prev = lambda col: pl.BlockSpec(
            (HALO, B_WIDTH), lambda i: (jnp.maximum(i * (tm // HALO) - 1, 0), col))
        in_specs += [rows(B_WIDTH, base), rows(B_WIDTH, base + 1), prev(base), prev(base + 1),
                     rows(B_WIDTH, base + 2), rows(B_WIDTH, base + 3),
                     pl.BlockSpec((CONV_W, B_WIDTH), lambda i: (0, 0)),
                     pl.BlockSpec((1, B_WIDTH), lambda i: (0, 0))]
        args += [proj] * 6 + [cw, cb.reshape(1, B_WIDTH)]
        blocks += 4 * tm * B_WIDTH * 2
    in_specs += [rows(X_WIDTH, x_col_block), rows(X_WIDTH, x_col_block + 1),
                 pl.BlockSpec((MEM_LEN, 2 * X_WIDTH), lambda i: (i // tiles_per_seq, 0)),
                 pl.BlockSpec((1, X_DIM), lambda i: (0, 0)),
                 pl.BlockSpec(w_out.shape, lambda i: (0, 0)),
                 pl.BlockSpec((tm, d), lambda i: (i, 0))]
    args += [proj, proj, kv, x_qg.reshape(1, X_DIM), w_out, h]
    return pl.pallas_call(
        functools.partial(_mix_out_kernel, has_conv=conv is not None, tiles_per_seq=tiles_per_seq),
        grid=(t // tm,),
        in_specs=in_specs,
        out_specs=pl.BlockSpec((tm, d), lambda i: (i, 0)),
        out_shape=jax.ShapeDtypeStruct((t, d), F32),
        compiler_params=pltpu.CompilerParams(
            dimension_semantics=("parallel",),
            vmem_limit_bytes=_vmem_limit(blocks, 0, 3 * tm * d * 4 + 8 * tm * B_WIDTH * 4)),
        name="mix_out",
    )(*args)


def _even_layer(h, mem2d, layer, bsz, seq, norm_g, w_in, w_out, a_qn, a_kn, a_lam, a_on,
                b_cw, b_cb, x_qn, x_kn, mem_g, w_mem_kv):
    lam_init = 0.8 - 0.6 * math.exp(-0.3 * layer)
    proj = _norm_proj(h, norm_g, w_in.astype(BF16))
    ya = _diff_attention(proj, a_qn, a_kn, a_lam, a_on, lam_init, bsz, seq)
    kv = _mem_kv(mem2d, mem_g, w_mem_kv.astype(BF16), x_kn, bsz)
    return _mix_out(ya, proj, kv, x_qn, w_out.astype(BF16), h, seq,
                    x_col_block=(4 * A_WIDTH + 4 * B_WIDTH) // X_WIDTH,
                    conv=(b_cw, b_cb, 4 * A_WIDTH // B_WIDTH))


def _odd_layer(h, mem2d, bsz, seq, norm_g, w_in, w_out, c_qn, c_kn, c_fb, x_qn, x_kn,
               mem_g, w_mem_kv):
    main = 4 * C_WIDTH
    w_t = w_in.T
    w_main = jnp.concatenate([w_t[:main], w_t[main + C_HEADS:]], axis=0).astype(BF16)
    w_f = jnp.pad(w_t[main:main + C_HEADS], ((0, LANES - C_HEADS), (0, 0))).astype(BF16)
    proj, f_logits = _norm_proj(h, norm_g, w_main, w_f, w_rows_are_outputs=True)
    cum = _forget_cum(f_logits, c_fb, bsz, seq)
    yc = _fox_attention(proj, cum, c_qn, c_kn, bsz, seq)
    kv = _mem_kv(mem2d, mem_g, w_mem_kv.astype(BF16), x_kn, bsz)
    return _mix_out(yc, proj, kv, x_qn, w_out.astype(BF16), h, seq, x_col_block=main // X_WIDTH)


def kernel(x, mem, e_norm_g, e_w_in, e_w_out, e_a_q_norm_g, e_a_k_norm_g, e_a_lambda,
           e_a_out_norm_g, e_b_conv_w, e_b_conv_b, e_x_q_norm_g, e_x_k_norm_g, e_mem_norm_g,
           e_w_mem_kv, o_norm_g, o_w_in, o_w_out, o_c_q_norm_g, o_c_k_norm_g, o_c_forget_b,
           o_x_q_norm_g, o_x_k_norm_g, o_mem_norm_g, o_w_mem_kv):
    bsz, seq, d = x.shape
    h = x.reshape(bsz * seq, d)
    mem2d = mem.reshape(bsz * MEM_LEN, d)
    depth = e_w_in.shape[0] + o_w_in.shape[0]
    for layer in range(depth):
        i = layer // 2
        if layer % 2 == 0:
            h = _even_layer(h, mem2d, layer, bsz, seq, e_norm_g[i], e_w_in[i], e_w_out[i],
                            e_a_q_norm_g[i], e_a_k_norm_g[i], e_a_lambda[i], e_a_out_norm_g[i],
                            e_b_conv_w[i], e_b_conv_b[i], e_x_q_norm_g[i], e_x_k_norm_g[i],
                            e_mem_norm_g[i], e_w_mem_kv[i])
        else:
            h = _odd_layer(h, mem2d, bsz, seq, o_norm_g[i], o_w_in[i], o_w_out[i],
                           o_c_q_norm_g[i], o_c_k_norm_g[i], o_c_forget_b[i],
                           o_x_q_norm_g[i], o_x_k_norm_g[i], o_mem_norm_g[i], o_w_mem_kv[i])
    return h.reshape(bsz, seq, d)
```

```python
import functools
import math

import numpy as np
import jax
import jax.numpy as jnp
from jax import lax
from jax.experimental import pallas as pl
from jax.experimental.pallas import tpu as pltpu

F32 = jnp.float32
BF16 = jnp.bfloat16

D_MODEL = 1024
MEM_LEN = 256
EPS = 1e-6
NEG = -1e30
LOG2E = math.log2(math.e)

A_HEADS = 8
A_QK_DIM = 64
A_V_DIM = 128
A_WIDTH = 1024
B_WIDTH = 512
CONV_W = 3
C_HEADS = 12
C_DIM = 128
C_WIDTH = 1536
X_HEADS = 4
X_DIM = 128
X_WIDTH = 512
MIX = 2048

LANES = 128
ONES_ROWS = 16
CUM_ROWS = 16
VMEM_CAP = 56 * 1024 * 1024

PROJ_TM = 1024
PROJ_TN = 1792
OUT_TM = 512
HALO = 8
ATT_T = 256
A_GROUP = 2
C_GROUP = 4
ATT_TEMP_BYTES = 20 << 20


def _vmem_limit(block_bytes, scratch_bytes=0, temp_bytes=0):
    need = 2 * block_bytes + scratch_bytes + temp_bytes + (4 << 20)
    return int(min(max(need, 16 << 20), VMEM_CAP))


def _silu(z):
    return z * jax.nn.sigmoid(z)


def _nt_dot(a, b):
    return lax.dot_general(a, b, (((1,), (1,)), ((), ())), preferred_element_type=F32)


def _proj_kernel(*refs, has_side, w_rows_are_outputs):
    if has_side:
        x_ref, g_ref, w_ref, ws_ref, o_ref, side_ref, xn_ref = refs
    else:
        x_ref, g_ref, w_ref, o_ref, xn_ref = refs
    matmul = _nt_dot if w_rows_are_outputs else functools.partial(jnp.dot, preferred_element_type=F32)

    @pl.when(pl.program_id(1) == 0)
    def _():
        x = x_ref[...]
        ms = jnp.mean(x * x, axis=-1, keepdims=True)
        xn = (x * lax.rsqrt(ms + EPS) * g_ref[...]).astype(BF16)
        xn_ref[...] = xn
        if has_side:
            side_ref[...] = matmul(xn, ws_ref[...])

    o_ref[...] = matmul(xn_ref[...], w_ref[...]).astype(o_ref.dtype)


def _norm_proj(h, g, w, w_side=None, w_rows_are_outputs=False):
    t, d = h.shape
    n = w.shape[0] if w_rows_are_outputs else w.shape[1]
    tm, tn = PROJ_TM, PROJ_TN
    grid = (t // tm, n // tn)
    blocks = tm * d * 4 + d * tn * 2 + tm * tn * 2
    w_spec = (pl.BlockSpec((tn, d), lambda i, j: (j, 0)) if w_rows_are_outputs
              else pl.BlockSpec((d, tn), lambda i, j: (0, j)))
    in_specs = [
        pl.BlockSpec((tm, d), lambda i, j: (i, 0)),
        pl.BlockSpec((1, d), lambda i, j: (0, 0)),
        w_spec,
    ]
    out_specs = pl.BlockSpec((tm, tn), lambda i, j: (i, j))
    out_shape = jax.ShapeDtypeStruct((t, n), BF16)
    args = [h, g.reshape(1, d), w]
    if w_side is not None:
        ns = w_side.shape[0] if w_rows_are_outputs else w_side.shape[1]
        in_specs.append(pl.BlockSpec(w_side.shape, lambda i, j: (0, 0)))
        out_specs = [out_specs, pl.BlockSpec((tm, ns), lambda i, j: (i, 0))]
        out_shape = [out_shape, jax.ShapeDtypeStruct((t, ns), F32)]
        args.append(w_side)
        blocks += d * ns * 2 + tm * ns * 4
    return pl.pallas_call(
        functools.partial(_proj_kernel, has_side=w_side is not None,
                          w_rows_are_outputs=w_rows_are_outputs),
        grid=grid,
        in_specs=in_specs,
        out_specs=out_specs,
        out_shape=out_shape,
        scratch_shapes=[pltpu.VMEM((tm, d), BF16)],
        compiler_params=pltpu.CompilerParams(
            dimension_semantics=("parallel", "arbitrary"),
            vmem_limit_bytes=_vmem_limit(blocks, tm * d * 2, tm * tn * 4 + tm * d * 4)),
        name="norm_proj",
    )(*args)


def _group_rms_inv(x, group):
    x2 = x * x
    if group == LANES:
        return lax.rsqrt(jnp.mean(x2, axis=-1, keepdims=True) + EPS)
    assert 2 * group == LANES
    first = lax.broadcasted_iota(jnp.int32, x.shape, 1) < group
    ms_a = jnp.sum(jnp.where(first, x2, 0.0), axis=-1, keepdims=True) * (1.0 / group)
    ms_b = jnp.sum(jnp.where(first, 0.0, x2), axis=-1, keepdims=True) * (1.0 / group)
    return lax.rsqrt(jnp.where(first, ms_a, ms_b) + EPS)


def _split3(c):
    c1 = c.astype(BF16).astype(F32)
    r = c - c1
    c2 = r.astype(BF16).astype(F32)
    c3 = (r - c2).astype(BF16).astype(F32)
    return c1, c2, c3


def _bias_lanes(term, key_side):
    n = term.shape[1]
    parts = jnp.concatenate(_split3(term), axis=0)
    ones = jnp.ones((3, n), F32)
    pad = jnp.zeros((10, n), F32)
    rows = jnp.concatenate([parts, ones, pad] if key_side else [ones, parts, pad], axis=0)
    sel = (lax.broadcasted_iota(jnp.int32, (16, LANES), 0)
           == lax.broadcasted_iota(jnp.int32, (16, LANES), 1)).astype(BF16)
    lanes = lax.dot_general(rows.astype(BF16), sel, (((0,), (0,)), ((), ())),
                            preferred_element_type=F32)
    return lanes.astype(BF16)


def _score_phase(chains, c, t):
    krow = lax.broadcasted_iota(jnp.int32, (t, t), 0)
    qcol = lax.broadcasted_iota(jnp.int32, (t, t), 1)
    lo, hi = c * t, (c + 1) * t
    scores = []
    for qa, ka_ref, _ in chains:
        s_d = jnp.where(krow <= qcol, _nt_dot(ka_ref[lo:hi, :], qa), NEG)
        m = jnp.max(s_d, axis=0, keepdims=True)
        s_f = None
        if c > 0:
            s_f = _nt_dot(ka_ref[0:lo, :], qa)
            m = jnp.maximum(m, jnp.max(s_f, axis=0, keepdims=True))
        scores.append((s_d, s_f, m))
    return scores


def _value_phase(chains, scores, c, t):
    lo, hi = c * t, (c + 1) * t
    outs = []
    for (_, _, vt_ref), (s_d, s_f, m) in zip(chains, scores):
        dv = vt_ref.shape[0] - ONES_ROWS
        p_d = jnp.exp2(s_d - m).astype(BF16)
        acc = jnp.dot(vt_ref[:, lo:hi], p_d, preferred_element_type=F32)
        if c > 0:
            p_f = jnp.exp2(s_f - m).astype(BF16)
            acc = acc + jnp.dot(vt_ref[:, 0:lo], p_f, preferred_element_type=F32)
        outs.append((acc[:dv], acc[dv:dv + 1]))
    return outs


def _attend_all_tiles(make_chains, finish, n_tiles, t):
    order = list(range(1, n_tiles, 2)) + list(range(n_tiles - 1 - (n_tiles % 2 == 0), -1, -2))
    assert sorted(order) == list(range(n_tiles))
    chains = make_chains(order[0])
    scores = _score_phase(chains, order[0], t)
    for pos, c in enumerate(order):
        if pos + 1 < n_tiles:
            nxt = order[pos + 1]
            next_chains = make_chains(nxt)
            next_scores = _score_phase(next_chains, nxt, t)
        finish(c, _value_phase(chains, scores, c, t))
        if pos + 1 < n_tiles:
            chains, scores = next_chains, next_scores


def _diff_kernel(slopes_ref, q_ref, k_ref, v_ref, z_ref, qg_ref, kg_ref, lam_ref, og_ref,
                 o_ref, ka_ref, vt_ref, *, lam_init, group):
    t = ATT_T
    hw = A_V_DIM
    seq = k_ref.shape[0]
    hg = pl.program_id(1)

    k_pos = lax.broadcasted_iota(jnp.int32, (1, seq), 1).astype(F32)
    for g in range(group):
        cs = slice(g * hw, (g + 1) * hw)
        k = k_ref[:, cs].astype(F32)
        ka_ref[g, :, :hw] = (k * _group_rms_inv(k, A_QK_DIM) * kg_ref[...]).astype(BF16)
        ka_ref[g, :, hw:] = _bias_lanes(LOG2E * slopes_ref[hg * group + g] * k_pos, True)
        vt_ref[g, :hw] = v_ref[:, cs].astype(F32).T.astype(BF16)
        vt_ref[g, hw:] = jnp.ones((ONES_ROWS, seq), BF16)

    lo = lax.broadcasted_iota(jnp.int32, (t, hw), 1) < A_QK_DIM
    lp = lam_ref[...]
    lam = (jnp.exp(jnp.sum(lp[0:1, :] * lp[1:2, :], axis=-1, keepdims=True))
           - jnp.exp(jnp.sum(lp[2:3, :] * lp[3:4, :], axis=-1, keepdims=True)) + lam_init)

    def make_chains(c):
        rows = slice(c * t, (c + 1) * t)
        q_pos = (c * t + lax.broadcasted_iota(jnp.int32, (1, t), 1)).astype(F32)
        chains = []
        for g in range(group):
            q = q_ref[rows, g * hw:(g + 1) * hw].astype(F32)
            qn = q * _group_rms_inv(q, A_QK_DIM) * qg_ref[...] * (LOG2E * A_QK_DIM ** -0.5)
            qb = _bias_lanes(-LOG2E * slopes_ref[hg * group + g] * q_pos, False)
            q1 = jnp.concatenate([jnp.where(lo, qn, 0.0).astype(BF16), qb], axis=1)
            q2 = jnp.concatenate([jnp.where(lo, 0.0, qn).astype(BF16), qb], axis=1)
            chains += [(q1, ka_ref.at[g], vt_ref.at[g]), (q2, ka_ref.at[g], vt_ref.at[g])]
        return chains

    def finish(c, outs):
        rows = slice(c * t, (c + 1) * t)
        for g in range(group):
            cs = slice(g * hw, (g + 1) * hw)
            (a1, l1), (a2, l2) = outs[2 * g], outs[2 * g + 1]
            o = (a1 / l1 - lam * (a2 / l2)).T
            ms = jnp.mean(o * o, axis=-1, keepdims=True)
            on = o * lax.rsqrt(ms + EPS) * og_ref[...] * (1.0 - lam_init)
            o_ref[rows, cs] = (on * _silu(z_ref[rows, cs].astype(F32))).astype(o_ref.dtype)

    _attend_all_tiles(make_chains, finish, seq // t, t)


def _diff_attention(proj, q_g, k_g, lam_params, out_g, lam_init, bsz, seq):
    hw = A_V_DIM
    grp = A_GROUP
    gw = grp * hw
    ng = A_HEADS // grp
    slopes = jnp.asarray(np.array([2.0 ** (-8.0 * (i + 1) / A_HEADS) for i in range(A_HEADS)],
                                  dtype=np.float32))
    qg2 = jnp.concatenate([q_g, q_g]).reshape(1, hw)
    kg2 = jnp.concatenate([k_g, k_g]).reshape(1, hw)
    blocks = 5 * seq * gw * 2
    scratch = grp * seq * 3 * hw * 2
    small = lambda shape: pl.BlockSpec(shape, lambda b, h: (0, 0))
    cols = lambda first: pl.BlockSpec((seq, gw), lambda b, h: (b, first * ng + h))
    return pl.pallas_call(
        functools.partial(_diff_kernel, lam_init=lam_init, group=grp),
        grid=(bsz, ng),
        in_specs=[
            pl.BlockSpec(memory_space=pltpu.SMEM),
            cols(0), cols(1), cols(2), cols(3),
            small((1, hw)), small((1, hw)), small((4, A_QK_DIM)), small((1, hw)),
        ],
        out_specs=cols(0),
        out_shape=jax.ShapeDtypeStruct((bsz * seq, A_WIDTH), BF16),
        scratch_shapes=[pltpu.VMEM((grp, seq, 2 * hw), BF16), pltpu.VMEM((grp, hw + ONES_ROWS, seq), BF16)],
        compiler_params=pltpu.CompilerParams(
            dimension_semantics=("parallel", "parallel"),
            vmem_limit_bytes=_vmem_limit(blocks, scratch, ATT_TEMP_BYTES)),
        name="diff_attn",
    )(slopes, proj, proj, proj, proj, qg2, kg2, lam_params, out_g.reshape(1, hw))


def _conv_gated(h_ref, c_ref, hp_ref, cp_ref, b_ref, z_ref, w_ref, bias_ref, starts_sequence):
    u = c_ref[...].astype(F32) * h_ref[...].astype(F32)
    up = cp_ref[...].astype(F32) * hp_ref[...].astype(F32)
    up = jnp.where(starts_sequence, 0.0, up)
    row = lax.broadcasted_iota(jnp.int32, u.shape, 0)
    u1 = jnp.where(row >= 1, pltpu.roll(u, 1, 0), up[HALO - 1:HALO, :])
    u2 = jnp.where(row >= 2, pltpu.roll(u, 2, 0),
                   jnp.where(row == 1, up[HALO - 1:HALO, :], up[HALO - 2:HALO - 1, :]))
    w = w_ref[...]
    y = w[0:1, :] * u2 + w[1:2, :] * u1 + w[2:3, :] * u + bias_ref[...]
    return (b_ref[...].astype(F32) * y * _silu(z_ref[...].astype(F32))).astype(BF16)


def _mem_kv_kernel(mem_ref, g_ref, w_ref, kg_ref, o_ref):
    x = mem_ref[...]
    ms = jnp.mean(x * x, axis=-1, keepdims=True)
    xn = (x * lax.rsqrt(ms + EPS) * g_ref[...]).astype(BF16)
    kv = jnp.dot(xn, w_ref[...], preferred_element_type=F32)
    for hd in range(X_HEADS):
        k = kv[:, hd * X_DIM:(hd + 1) * X_DIM]
        kms = jnp.mean(k * k, axis=-1, keepdims=True)
        o_ref[:, hd * X_DIM:(hd + 1) * X_DIM] = (k * lax.rsqrt(kms + EPS) * kg_ref[...]).astype(BF16)
    o_ref[:, X_WIDTH:] = kv[:, X_WIDTH:].astype(BF16)


def _mem_kv(mem2d, g, w, k_g, bsz):
    d = mem2d.shape[1]
    blocks = MEM_LEN * d * 4 + w.size * 2 + MEM_LEN * 2 * X_WIDTH * 2
    return pl.pallas_call(
        _mem_kv_kernel,
        grid=(bsz,),
        in_specs=[pl.BlockSpec((MEM_LEN, d), lambda b: (b, 0)),
                  pl.BlockSpec((1, d), lambda b: (0, 0)),
                  pl.BlockSpec(w.shape, lambda b: (0, 0)),
                  pl.BlockSpec((1, X_DIM), lambda b: (0, 0))],
        out_specs=pl.BlockSpec((MEM_LEN, 2 * X_WIDTH), lambda b: (b, 0)),
        out_shape=jax.ShapeDtypeStruct((bsz * MEM_LEN, 2 * X_WIDTH), BF16),
        compiler_params=pltpu.CompilerParams(
            dimension_semantics=("parallel",),
            vmem_limit_bytes=_vmem_limit(blocks, 0, 4 * MEM_LEN * d * 4)),
        name="mem_kv",
    )(mem2d, g.reshape(1, d), w, k_g.reshape(1, X_DIM))


def _xattn_scores(q_ref, kv_ref, qg_ref):
    scores = []
    for hd in range(X_HEADS):
        cs = slice(hd * X_DIM, (hd + 1) * X_DIM)
        q = q_ref[:, cs].astype(F32)
        ms = jnp.mean(q * q, axis=-1, keepdims=True)
        qn = (q * lax.rsqrt(ms + EPS) * qg_ref[...] * (X_DIM ** -0.5)).astype(BF16)
        scores.append(_nt_dot(qn, kv_ref[:, cs]))
    return scores


def _xattn_gated(scores, z_ref, kv_ref):
    heads = []
    for hd, s in enumerate(scores):
        cs = slice(hd * X_DIM, (hd + 1) * X_DIM)
        m = jnp.max(s, axis=-1, keepdims=True)
        p = jnp.exp(s - m)
        l = jnp.sum(p, axis=-1, keepdims=True)
        vs = slice(X_WIDTH + hd * X_DIM, X_WIDTH + (hd + 1) * X_DIM)
        o = jnp.dot(p.astype(BF16), kv_ref[:, vs], preferred_element_type=F32) / l
        heads.append((o * _silu(z_ref[:, cs].astype(F32))).astype(BF16))
    return jnp.concatenate(heads, axis=1)


def _forget_cum_kernel(f_ref, b_ref, cum_ref):
    x = f_ref[...] + b_ref[...]
    c = jnp.minimum(x, 0.0) - jnp.log1p(jnp.exp(-jnp.abs(x)))
    n = c.shape[0]
    row = lax.broadcasted_iota(jnp.int32, c.shape, 0)
    shift = 1
    while shift < n:
        c = c + jnp.where(row >= shift, pltpu.roll(c, shift, 0), 0.0)
        shift *= 2
    cum_ref[...] = c.T[:cum_ref.shape[0], :]


def _forget_cum(f_logits, f_bias, bsz, seq):
    bias = jnp.zeros((1, LANES), F32).at[0, :C_HEADS].set(f_bias)
    blocks = seq * LANES * 4 + CUM_ROWS * seq * 4
    return pl.pallas_call(
        _forget_cum_kernel,
        grid=(bsz,),
        in_specs=[pl.BlockSpec((seq, LANES), lambda b: (b, 0)),
                  pl.BlockSpec((1, LANES), lambda b: (0, 0))],
        out_specs=pl.BlockSpec((CUM_ROWS, seq), lambda b: (b, 0)),
        out_shape=jax.ShapeDtypeStruct((bsz * CUM_ROWS, seq), F32),
        compiler_params=pltpu.CompilerParams(
            dimension_semantics=("parallel",),
            vmem_limit_bytes=_vmem_limit(blocks, 0, 6 * seq * LANES * 4)),
        name="forget_cum",
    )(f_logits, bias)


def _fox_kernel(q_ref, k_ref, v_ref, z_ref, cum_ref, qg_ref, kg_ref, o_ref, ka_ref, vt_ref,
                *, group):
    t = ATT_T
    hw = C_DIM
    seq = k_ref.shape[0]

    for g in range(group):
        cs = slice(g * hw, (g + 1) * hw)
        k = k_ref[:, cs].astype(F32)
        ka_ref[g, :, :hw] = (k * _group_rms_inv(k, C_DIM) * kg_ref[...]).astype(BF16)
        ka_ref[g, :, hw:] = _bias_lanes(-LOG2E * cum_ref[g], True)
        vt_ref[g, :hw] = v_ref[:, cs].astype(F32).T.astype(BF16)
        vt_ref[g, hw:] = jnp.ones((ONES_ROWS, seq), BF16)

    def make_chains(c):
        rows = slice(c * t, (c + 1) * t)
        chains = []
        for g in range(group):
            q = q_ref[rows, g * hw:(g + 1) * hw].astype(F32)
            qn = (q * _group_rms_inv(q, C_DIM) * qg_ref[...] * (LOG2E * C_DIM ** -0.5)).astype(BF16)
            qb = _bias_lanes(LOG2E * cum_ref[g, :, rows], False)
            chains.append((jnp.concatenate([qn, qb], axis=1), ka_ref.at[g], vt_ref.at[g]))
        return chains

    def finish(c, outs):
        rows = slice(c * t, (c + 1) * t)
        for g, (acc, l) in enumerate(outs):
            cs = slice(g * hw, (g + 1) * hw)
            o_ref[rows, cs] = ((acc / l).T * _silu(z_ref[rows, cs].astype(F32))).astype(o_ref.dtype)

    _attend_all_tiles(make_chains, finish, seq // t, t)


def _fox_attention(proj, cum, q_g, k_g, bsz, seq):
    hw = C_DIM
    grp = C_GROUP
    gw = grp * hw
    ng = C_HEADS // grp
    blocks = 5 * seq * gw * 2 + grp * 8 * seq * 4
    scratch = grp * seq * 3 * hw * 2
    cum3 = cum.reshape(bsz * CUM_ROWS, 1, seq)
    small = lambda shape: pl.BlockSpec(shape, lambda b, h: (0, 0))
    cols = lambda first: pl.BlockSpec((seq, gw), lambda b, h: (b, first * ng + h))
    return pl.pallas_call(
        functools.partial(_fox_kernel, group=grp),
        grid=(bsz, ng),
        in_specs=[
            cols(0), cols(1), cols(2), cols(3),
            pl.BlockSpec((grp, 1, seq), lambda b, h: (b * (CUM_ROWS // grp) + h, 0, 0)),
            small((1, hw)), small((1, hw)),
        ],
        out_specs=cols(0),
        out_shape=jax.ShapeDtypeStruct((bsz * seq, C_WIDTH), BF16),
        scratch_shapes=[pltpu.VMEM((grp, seq, 2 * hw), BF16), pltpu.VMEM((grp, hw + ONES_ROWS, seq), BF16)],
        compiler_params=pltpu.CompilerParams(
            dimension_semantics=("parallel", "parallel"),
            vmem_limit_bytes=_vmem_limit(blocks, scratch, ATT_TEMP_BYTES)),
        name="fox_attn",
    )(proj, proj, proj, proj, cum3, q_g.reshape(1, hw), k_g.reshape(1, hw))


def _mix_out_kernel(*refs, has_conv, tiles_per_seq):
    refs = list(refs)
    attn_ref = refs.pop(0)
    conv_refs = [refs.pop(0) for _ in range(8)] if has_conv else None
    xq_ref, xz_ref, kv_ref, xqg_ref, w_ref, h_ref, o_ref = refs

    scores = _xattn_scores(xq_ref, kv_ref, xqg_ref)
    width = attn_ref.shape[1]
    acc = h_ref[...] + jnp.dot(attn_ref[...], w_ref[:width, :], preferred_element_type=F32)
    if has_conv:
        starts_sequence = pl.program_id(0) % tiles_per_seq == 0
        yb = _conv_gated(*conv_refs, starts_sequence)
        acc = acc + jnp.dot(yb, w_ref[width:width + B_WIDTH, :], preferred_element_type=F32)
        width += B_WIDTH
    yx = _xattn_gated(scores, xz_ref, kv_ref)
    o_ref[...] = acc + jnp.dot(yx, w_ref[width:, :], preferred_element_type=F32)


def _mix_out(y_attn, proj, kv, x_qg, w_out, h, seq, x_col_block, conv=None):
    t, d = h.shape
    tm = OUT_TM
    tiles_per_seq = seq // tm
    aw = y_attn.shape[1]
    rows = lambda width, col: pl.BlockSpec((tm, width), lambda i: (i, col))
    in_specs = [rows(aw, 0)]
    args = [y_attn]
    blocks = tm * aw * 2 + 2 * tm * X_WIDTH * 2 + MEM_LEN * 2 * X_WIDTH * 2 + w_out.size * 2 + 2 * tm * d * 4
    if conv is not None:
        cw, cb, base = conv
        prev = lambda col: pl.BlockSpec(
            (HALO, B_WIDTH), lambda i: (jnp.maximum(i * (tm // HALO) - 1, 0), col))
        in_specs += [rows(B_WIDTH, base), rows(B_WIDTH, base + 1), prev(base), prev(base + 1),
                     rows(B_WIDTH, base + 2), rows(B_WIDTH, base + 3),
                     pl.BlockSpec((CONV_W, B_WIDTH), lambda i: (0, 0)),
                     pl.BlockSpec((1, B_WIDTH), lambda i: (0, 0))]
        args += [proj] * 6 + [cw, cb.reshape(1, B_WIDTH)]
        blocks += 4 * tm * B_WIDTH * 2
    in_specs += [rows(X_WIDTH, x_col_block), rows(X_WIDTH, x_col_block + 1),
                 pl.BlockSpec((MEM_LEN, 2 * X_WIDTH), lambda i: (i // tiles_per_seq, 0)),
                 pl.BlockSpec((1, X_DIM), lambda i: (0, 0)),
                 pl.BlockSpec(w_out.shape, lambda i: (0, 0)),
                 pl.BlockSpec((tm, d), lambda i: (i, 0))]
    args += [proj, proj, kv, x_qg.reshape(1, X_DIM), w_out, h]
    return pl.pallas_call(
        functools.partial(_mix_out_kernel, has_conv=conv is not None, tiles_per_seq=tiles_per_seq),
        grid=(t // tm,),
        in_specs=in_specs,
        out_specs=pl.BlockSpec((tm, d), lambda i: (i, 0)),
        out_shape=jax.ShapeDtypeStruct((t, d), F32),
        compiler_params=pltpu.CompilerParams(
            dimension_semantics=("parallel",),
            vmem_limit_bytes=_vmem_limit(blocks, 0, 3 * tm * d * 4 + 8 * tm * B_WIDTH * 4)),
        name="mix_out",
    )(*args)


def _even_layer(h, mem2d, layer, bsz, seq, norm_g, w_in, w_out, a_qn, a_kn, a_lam, a_on,
                b_cw, b_cb, x_qn, x_kn, mem_g, w_mem_kv):
    lam_init = 0.8 - 0.6 * math.exp(-0.3 * layer)
    proj = _norm_proj(h, norm_g, w_in.astype(BF16))
    ya = _diff_attention(proj, a_qn, a_kn, a_lam, a_on, lam_init, bsz, seq)
    kv = _mem_kv(mem2d, mem_g, w_mem_kv.astype(BF16), x_kn, bsz)
    return _mix_out(ya, proj, kv, x_qn, w_out.astype(BF16), h, seq,
                    x_col_block=(4 * A_WIDTH + 4 * B_WIDTH) // X_WIDTH,
                    conv=(b_cw, b_cb, 4 * A_WIDTH // B_WIDTH))


def _odd_layer(h, mem2d, bsz, seq, norm_g, w_in, w_out, c_qn, c_kn, c_fb, x_qn, x_kn,
               mem_g, w_mem_kv):
    main = 4 * C_WIDTH
    w_t = w_in.T
    w_main = jnp.concatenate([w_t[:main], w_t[main + C_HEADS:]], axis=0).astype(BF16)
    w_f = jnp.pad(w_t[main:main + C_HEADS], ((0, LANES - C_HEADS), (0, 0))).astype(BF16)
    proj, f_logits = _norm_proj(h, norm_g, w_main, w_f, w_rows_are_outputs=True)
    cum = _forget_cum(f_logits, c_fb, bsz, seq)
    yc = _fox_attention(proj, cum, c_qn, c_kn, bsz, seq)
    kv = _mem_kv(mem2d, mem_g, w_mem_kv.astype(BF16), x_kn, bsz)
    return _mix_out(yc, proj, kv, x_qn, w_out.astype(BF16), h, seq, x_col_block=main // X_WIDTH)


def kernel(x, mem, e_norm_g, e_w_in, e_w_out, e_a_q_norm_g, e_a_k_norm_g, e_a_lambda,
           e_a_out_norm_g, e_b_conv_w, e_b_conv_b, e_x_q_norm_g, e_x_k_norm_g, e_mem_norm_g,
           e_w_mem_kv, o_norm_g, o_w_in, o_w_out, o_c_q_norm_g, o_c_k_norm_g, o_c_forget_b,
           o_x_q_norm_g, o_x_k_norm_g, o_mem_norm_g, o_w_mem_kv):
    bsz, seq, d = x.shape
    h = x.reshape(bsz * seq, d)
    mem2d = mem.reshape(bsz * MEM_LEN, d)
    depth = e_w_in.shape[0] + o_w_in.shape[0]
    for layer in range(depth):
        i = layer // 2
        if layer % 2 == 0:
            h = _even_layer(h, mem2d, layer, bsz, seq, e_norm_g[i], e_w_in[i], e_w_out[i],
                            e_a_q_norm_g[i], e_a_k_norm_g[i], e_a_lambda[i], e_a_out_norm_g[i],
                            e_b_conv_w[i], e_b_conv_b[i], e_x_q_norm_g[i], e_x_k_norm_g[i],
                            e_mem_norm_g[i], e_w_mem_kv[i])
        else:
            h = _odd_layer(h, mem2d, bsz, seq, o_norm_g[i], o_w_in[i], o_w_out[i],
                           o_c_q_norm_g[i], o_c_k_norm_g[i], o_c_forget_b[i],
                           o_x_q_norm_g[i], o_x_k_norm_g[i], o_mem_norm_g[i], o_w_mem_kv[i])
    return h.reshape(bsz, seq, d)
```

```python
import functools
import math

import numpy as np
import jax
import jax.numpy as jnp
from jax import lax
from jax.experimental import pallas as pl
from jax.experimental.pallas import tpu as pltpu

F32 = jnp.float32
BF16 = jnp.bfloat16

D_MODEL = 1024
MEM_LEN = 256
EPS = 1e-6
NEG = -1e30
LOG2E = math.log2(math.e)

A_HEADS = 8
A_QK_DIM = 64
A_V_DIM = 128
A_WIDTH = 1024
B_WIDTH = 512
CONV_W = 3
C_HEADS = 12
C_DIM = 128
C_WIDTH = 1536
X_HEADS = 4
X_DIM = 128
X_WIDTH = 512
MIX = 2048

LANES = 128
ONES_ROWS = 16
CUM_ROWS = 16
VMEM_CAP = 56 * 1024 * 1024

PROJ_TM = 512
PROJ_TN = 1792
OUT_TM = 512
HALO = 8
ATT_T = 256
A_GROUP = 2
C_GROUP = 4
ATT_TEMP_BYTES = 20 << 20


def _vmem_limit(block_bytes, scratch_bytes=0, temp_bytes=0):
    need = 2 * block_bytes + scratch_bytes + temp_bytes + (4 << 20)
    return int(min(max(need, 16 << 20), VMEM_CAP))


def _silu(z):
    return z * jax.nn.sigmoid(z)


def _nt_dot(a, b):
    return lax.dot_general(a, b, (((1,), (1,)), ((), ())), preferred_element_type=F32)


def _proj_kernel(*refs, has_side, w_rows_are_outputs):
    if has_side:
        x_ref, g_ref, w_ref, ws_ref, o_ref, side_ref = refs
    else:
        x_ref, g_ref, w_ref, o_ref = refs
    matmul = _nt_dot if w_rows_are_outputs else functools.partial(jnp.dot, preferred_element_type=F32)

    x = x_ref[...]
    ms = jnp.mean(x * x, axis=-1, keepdims=True)
    xn = (x * lax.rsqrt(ms + EPS) * g_ref[...]).astype(BF16)
    if has_side:
        side_ref[...] = matmul(xn, ws_ref[...])
    for c0 in range(0, o_ref.shape[1], PROJ_TN):
        cols = slice(c0, c0 + PROJ_TN)
        w = w_ref[cols, :] if w_rows_are_outputs else w_ref[:, cols]
        o_ref[:, cols] = matmul(xn, w).astype(o_ref.dtype)


def _norm_proj(h, g, w, w_side=None, w_rows_are_outputs=False):
    t, d = h.shape
    n = w.shape[0] if w_rows_are_outputs else w.shape[1]
    tm = PROJ_TM
    resident = lambda shape: pl.BlockSpec(shape, lambda i: (0, 0), pipeline_mode=pl.Buffered(1))
    in_specs = [
        pl.BlockSpec((tm, d), lambda i: (i, 0)),
        pl.BlockSpec((1, d), lambda i: (0, 0)),
        resident(w.shape),
    ]
    out_specs = pl.BlockSpec((tm, n), lambda i: (i, 0))
    out_shape = jax.ShapeDtypeStruct((t, n), BF16)
    args = [h, g.reshape(1, d), w]
    blocks = tm * d * 4 + tm * n * 2
    single = w.size * 2
    if w_side is not None:
        ns = w_side.shape[0] if w_rows_are_outputs else w_side.shape[1]
        in_specs.append(resident(w_side.shape))
        out_specs = [out_specs, pl.BlockSpec((tm, ns), lambda i: (i, 0))]
        out_shape = [out_shape, jax.ShapeDtypeStruct((t, ns), F32)]
        args.append(w_side)
        blocks += tm * ns * 4
        single += w_side.size * 2
    return pl.pallas_call(
        functools.partial(_proj_kernel, has_side=w_side is not None,
                          w_rows_are_outputs=w_rows_are_outputs),
        grid=(t // tm,),
        in_specs=in_specs,
        out_specs=out_specs,
        out_shape=out_shape,
        compiler_params=pltpu.CompilerParams(
            dimension_semantics=("parallel",),
            vmem_limit_bytes=_vmem_limit(blocks, single, 3 * tm * PROJ_TN * 4 + 2 * tm * d * 4)),
        name="norm_proj",
    )(*args)


def _group_rms_inv(x, group):
    x2 = x * x
    if group == LANES:
        return lax.rsqrt(jnp.mean(x2, axis=-1, keepdims=True) + EPS)
    assert 2 * group == LANES
    first = lax.broadcasted_iota(jnp.int32, x.shape, 1) < group
    ms_a = jnp.sum(jnp.where(first, x2, 0.0), axis=-1, keepdims=True) * (1.0 / group)
    ms_b = jnp.sum(jnp.where(first, 0.0, x2), axis=-1, keepdims=True) * (1.0 / group)
    return lax.rsqrt(jnp.where(first, ms_a, ms_b) + EPS)


def _split3(c):
    c1 = c.astype(BF16).astype(F32)
    r = c - c1
    c2 = r.astype(BF16).astype(F32)
    c3 = (r - c2).astype(BF16).astype(F32)
    return c1, c2, c3


def _bias_lanes(term, key_side):
    n = term.shape[1]
    parts = jnp.concatenate(_split3(term), axis=0)
    ones = jnp.ones((3, n), F32)
    pad = jnp.zeros((10, n), F32)
    rows = jnp.concatenate([parts, ones, pad] if key_side else [ones, parts, pad], axis=0)
    sel = (lax.broadcasted_iota(jnp.int32, (16, LANES), 0)
           == lax.broadcasted_iota(jnp.int32, (16, LANES), 1)).astype(BF16)
    lanes = lax.dot_general(rows.astype(BF16), sel, (((0,), (0,)), ((), ())),
                            preferred_element_type=F32)
    return lanes.astype(BF16)


def _score_phase(chains, c, t):
    krow = lax.broadcasted_iota(jnp.int32, (t, t), 0)
    qcol = lax.broadcasted_iota(jnp.int32, (t, t), 1)
    lo, hi = c * t, (c + 1) * t
    scores = []
    for qa, ka_ref, _ in chains:
        s_d = jnp.where(krow <= qcol, _nt_dot(ka_ref[lo:hi, :], qa), NEG)
        m = jnp.max(s_d, axis=0, keepdims=True)
        s_f = None
        if c > 0:
            s_f = _nt_dot(ka_ref[0:lo, :], qa)
            m = jnp.maximum(m, jnp.max(s_f, axis=0, keepdims=True))
        scores.append((s_d, s_f, m))
    return scores


def _value_phase(chains, scores, c, t):
    lo, hi = c * t, (c + 1) * t
    outs = []
    for (_, _, vt_ref), (s_d, s_f, m) in zip(chains, scores):
        dv = vt_ref.shape[0] - ONES_ROWS
        p_d = jnp.exp2(s_d - m).astype(BF16)
        acc = jnp.dot(vt_ref[:, lo:hi], p_d, preferred_element_type=F32)
        if c > 0:
            p_f = jnp.exp2(s_f - m).astype(BF16)
            acc = acc + jnp.dot(vt_ref[:, 0:lo], p_f, preferred_element_type=F32)
        outs.append((acc[:dv], acc[dv:dv + 1]))
    return outs


def _attend_all_tiles(make_chains, finish, n_tiles, t):
    order = list(range(1, n_tiles, 2)) + list(range(n_tiles - 1 - (n_tiles % 2 == 0), -1, -2))
    assert sorted(order) == list(range(n_tiles))
    chains = make_chains(order[0])
    scores = _score_phase(chains, order[0], t)
    for pos, c in enumerate(order):
        if pos + 1 < n_tiles:
            nxt = order[pos + 1]
            next_chains = make_chains(nxt)
            next_scores = _score_phase(next_chains, nxt, t)
        finish(c, _value_phase(chains, scores, c, t))
        if pos + 1 < n_tiles:
            chains, scores = next_chains, next_scores


def _diff_kernel(slopes_ref, q_ref, k_ref, v_ref, z_ref, qg_ref, kg_ref, lam_ref, og_ref,
                 o_ref, ka_ref, vt_ref, *, lam_init, group):
    t = ATT_T
    hw = A_V_DIM
    seq = k_ref.shape[0]
    hg = pl.program_id(1)

    k_pos = lax.broadcasted_iota(jnp.int32, (1, seq), 1).astype(F32)
    for g in range(group):
        cs = slice(g * hw, (g + 1) * hw)
        k = k_ref[:, cs].astype(F32)
        ka_ref[g, :, :hw] = (k * _group_rms_inv(k, A_QK_DIM) * kg_ref[...]).astype(BF16)
        ka_ref[g, :, hw:] = _bias_lanes(LOG2E * slopes_ref[hg * group + g] * k_pos, True)
        vt_ref[g, :hw] = v_ref[:, cs].astype(F32).T.astype(BF16)
        vt_ref[g, hw:] = jnp.ones((ONES_ROWS, seq), BF16)

    lo = lax.broadcasted_iota(jnp.int32, (t, hw), 1) < A_QK_DIM
    lp = lam_ref[...]
    lam = (jnp.exp(jnp.sum(lp[0:1, :] * lp[1:2, :], axis=-1, keepdims=True))
           - jnp.exp(jnp.sum(lp[2:3, :] * lp[3:4, :], axis=-1, keepdims=True)) + lam_init)

    def make_chains(c):
        rows = slice(c * t, (c + 1) * t)
        q_pos = (c * t + lax.broadcasted_iota(jnp.int32, (1, t), 1)).astype(F32)
        chains = []
        for g in range(group):
            q = q_ref[rows, g * hw:(g + 1) * hw].astype(F32)
            qn = q * _group_rms_inv(q, A_QK_DIM) * qg_ref[...] * (LOG2E * A_QK_DIM ** -0.5)
            qb = _bias_lanes(-LOG2E * slopes_ref[hg * group + g] * q_pos, False)
            q1 = jnp.concatenate([jnp.where(lo, qn, 0.0).astype(BF16), qb], axis=1)
            q2 = jnp.concatenate([jnp.where(lo, 0.0, qn).astype(BF16), qb], axis=1)
            chains += [(q1, ka_ref.at[g], vt_ref.at[g]), (q2, ka_ref.at[g], vt_ref.at[g])]
        return chains

    def finish(c, outs):
        rows = slice(c * t, (c + 1) * t)
        for g in range(group):
            cs = slice(g * hw, (g + 1) * hw)
            (a1, l1), (a2, l2) = outs[2 * g], outs[2 * g + 1]
            o = (a1 / l1 - lam * (a2 / l2)).T
            ms = jnp.mean(o * o, axis=-1, keepdims=True)
            on = o * lax.rsqrt(ms + EPS) * og_ref[...] * (1.0 - lam_init)
            o_ref[rows, cs] = (on * _silu(z_ref[rows, cs].astype(F32))).astype(o_ref.dtype)

    _attend_all_tiles(make_chains, finish, seq // t, t)


def _diff_attention(proj, q_g, k_g, lam_params, out_g, lam_init, bsz, seq):
    hw = A_V_DIM
    grp = A_GROUP
    gw = grp * hw
    ng = A_HEADS // grp
    slopes = jnp.asarray(np.array([2.0 ** (-8.0 * (i + 1) / A_HEADS) for i in range(A_HEADS)],
                                  dtype=np.float32))
    qg2 = jnp.concatenate([q_g, q_g]).reshape(1, hw)
    kg2 = jnp.concatenate([k_g, k_g]).reshape(1, hw)
    blocks = 5 * seq * gw * 2
    scratch = grp * seq * 3 * hw * 2
    small = lambda shape: pl.BlockSpec(shape, lambda b, h: (0, 0))
    cols = lambda first: pl.BlockSpec((seq, gw), lambda b, h: (b, first * ng + h))
    return pl.pallas_call(
        functools.partial(_diff_kernel, lam_init=lam_init, group=grp),
        grid=(bsz, ng),
        in_specs=[
            pl.BlockSpec(memory_space=pltpu.SMEM),
            cols(0), cols(1), cols(2), cols(3),
            small((1, hw)), small((1, hw)), small((4, A_QK_DIM)), small((1, hw)),
        ],
        out_specs=cols(0),
        out_shape=jax.ShapeDtypeStruct((bsz * seq, A_WIDTH), BF16),
        scratch_shapes=[pltpu.VMEM((grp, seq, 2 * hw), BF16), pltpu.VMEM((grp, hw + ONES_ROWS, seq), BF16)],
        compiler_params=pltpu.CompilerParams(
            dimension_semantics=("parallel", "parallel"),
            vmem_limit_bytes=_vmem_limit(blocks, scratch, ATT_TEMP_BYTES)),
        name="diff_attn",
    )(slopes, proj, proj, proj, proj, qg2, kg2, lam_params, out_g.reshape(1, hw))


def _conv_gated(h_ref, c_ref, hp_ref, cp_ref, b_ref, z_ref, w_ref, bias_ref, starts_sequence):
    u = c_ref[...].astype(F32) * h_ref[...].astype(F32)
    up = cp_ref[...].astype(F32) * hp_ref[...].astype(F32)
    up = jnp.where(starts_sequence, 0.0, up)
    row = lax.broadcasted_iota(jnp.int32, u.shape, 0)
    u1 = jnp.where(row >= 1, pltpu.roll(u, 1, 0), up[HALO - 1:HALO, :])
    u2 = jnp.where(row >= 2, pltpu.roll(u, 2, 0),
                   jnp.where(row == 1, up[HALO - 1:HALO, :], up[HALO - 2:HALO - 1, :]))
    w = w_ref[...]
    y = w[0:1, :] * u2 + w[1:2, :] * u1 + w[2:3, :] * u + bias_ref[...]
    return (b_ref[...].astype(F32) * y * _silu(z_ref[...].astype(F32))).astype(BF16)


def _mem_kv_kernel(mem_ref, g_ref, w_ref, kg_ref, o_ref):
    x = mem_ref[...]
    ms = jnp.mean(x * x, axis=-1, keepdims=True)
    xn = (x * lax.rsqrt(ms + EPS) * g_ref[...]).astype(BF16)
    kv = jnp.dot(xn, w_ref[...], preferred_element_type=F32)
    for hd in range(X_HEADS):
        k = kv[:, hd * X_DIM:(hd + 1) * X_DIM]
        kms = jnp.mean(k * k, axis=-1, keepdims=True)
        o_ref[:, hd * X_DIM:(hd + 1) * X_DIM] = (k * lax.rsqrt(kms + EPS) * kg_ref[...]).astype(BF16)
    o_ref[:, X_WIDTH:] = kv[:, X_WIDTH:].astype(BF16)


def _mem_kv(mem2d, g, w, k_g, bsz):
    d = mem2d.shape[1]
    blocks = MEM_LEN * d * 4 + w.size * 2 + MEM_LEN * 2 * X_WIDTH * 2
    return pl.pallas_call(
        _mem_kv_kernel,
        grid=(bsz,),
        in_specs=[pl.BlockSpec((MEM_LEN, d), lambda b: (b, 0)),
                  pl.BlockSpec((1, d), lambda b: (0, 0)),
                  pl.BlockSpec(w.shape, lambda b: (0, 0)),
                  pl.BlockSpec((1, X_DIM), lambda b: (0, 0))],
        out_specs=pl.BlockSpec((MEM_LEN, 2 * X_WIDTH), lambda b: (b, 0)),
        out_shape=jax.ShapeDtypeStruct((bsz * MEM_LEN, 2 * X_WIDTH), BF16),
        compiler_params=pltpu.CompilerParams(
            dimension_semantics=("parallel",),
            vmem_limit_bytes=_vmem_limit(blocks, 0, 4 * MEM_LEN * d * 4)),
        name="mem_kv",
    )(mem2d, g.reshape(1, d), w, k_g.reshape(1, X_DIM))


def _xattn_scores(q_ref, kv_ref, qg_ref):
    scores = []
    for hd in range(X_HEADS):
        cs = slice(hd * X_DIM, (hd + 1) * X_DIM)
        q = q_ref[:, cs].astype(F32)
        ms = jnp.mean(q * q, axis=-1, keepdims=True)
        qn = (q * lax.rsqrt(ms + EPS) * qg_ref[...] * (X_DIM ** -0.5)).astype(BF16)
        scores.append(_nt_dot(qn, kv_ref[:, cs]))
    return scores


def _xattn_gated(scores, z_ref, kv_ref):
    heads = []
    for hd, s in enumerate(scores):
        cs = slice(hd * X_DIM, (hd + 1) * X_DIM)
        m = jnp.max(s, axis=-1, keepdims=True)
        p = jnp.exp(s - m)
        l = jnp.sum(p, axis=-1, keepdims=True)
        vs = slice(X_WIDTH + hd * X_DIM, X_WIDTH + (hd + 1) * X_DIM)
        o = jnp.dot(p.astype(BF16), kv_ref[:, vs], preferred_element_type=F32) / l
        heads.append((o * _silu(z_ref[:, cs].astype(F32))).astype(BF16))
    return jnp.concatenate(heads, axis=1)


def _forget_cum_kernel(f_ref, b_ref, cum_ref):
    x = f_ref[...] + b_ref[...]
    c = jnp.minimum(x, 0.0) - jnp.log1p(jnp.exp(-jnp.abs(x)))
    n = c.shape[0]
    row = lax.broadcasted_iota(jnp.int32, c.shape, 0)
    shift = 1
    while shift < n:
        c = c + jnp.where(row >= shift, pltpu.roll(c, shift, 0), 0.0)
        shift *= 2
    cum_ref[...] = c.T[:cum_ref.shape[0], :]


def _forget_cum(f_logits, f_bias, bsz, seq):
    bias = jnp.zeros((1, LANES), F32).at[0, :C_HEADS].set(f_bias)
    blocks = seq * LANES * 4 + CUM_ROWS * seq * 4
    return pl.pallas_call(
        _forget_cum_kernel,
        grid=(bsz,),
        in_specs=[pl.BlockSpec((seq, LANES), lambda b: (b, 0)),
                  pl.BlockSpec((1, LANES), lambda b: (0, 0))],
        out_specs=pl.BlockSpec((CUM_ROWS, seq), lambda b: (b, 0)),
        out_shape=jax.ShapeDtypeStruct((bsz * CUM_ROWS, seq), F32),
        compiler_params=pltpu.CompilerParams(
            dimension_semantics=("parallel",),
            vmem_limit_bytes=_vmem_limit(blocks, 0, 6 * seq * LANES * 4)),
        name="forget_cum",
    )(f_logits, bias)


def _fox_kernel(q_ref, k_ref, v_ref, z_ref, cum_ref, qg_ref, kg_ref, o_ref, ka_ref, vt_ref,
                *, group):
    t = ATT_T
    hw = C_DIM
    seq = k_ref.shape[0]

    for g in range(group):
        cs = slice(g * hw, (g + 1) * hw)
        k = k_ref[:, cs].astype(F32)
        ka_ref[g, :, :hw] = (k * _group_rms_inv(k, C_DIM) * kg_ref[...]).astype(BF16)
        ka_ref[g, :, hw:] = _bias_lanes(-LOG2E * cum_ref[g], True)
        vt_ref[g, :hw] = v_ref[:, cs].astype(F32).T.astype(BF16)
        vt_ref[g, hw:] = jnp.ones((ONES_ROWS, seq), BF16)

    def make_chains(c):
        rows = slice(c * t, (c + 1) * t)
        chains = []
        for g in range(group):
            q = q_ref[rows, g * hw:(g + 1) * hw].astype(F32)
            qn = (q * _group_rms_inv(q, C_DIM) * qg_ref[...] * (LOG2E * C_DIM ** -0.5)).astype(BF16)
            qb = _bias_lanes(LOG2E * cum_ref[g, :, rows], False)
            chains.append((jnp.concatenate([qn, qb], axis=1), ka_ref.at[g], vt_ref.at[g]))
        return chains

    def finish(c, outs):
        rows = slice(c * t, (c + 1) * t)
        for g, (acc, l) in enumerate(outs):
            cs = slice(g * hw, (g + 1) * hw)
            o_ref[rows, cs] = ((acc / l).T * _silu(z_ref[rows, cs].astype(F32))).astype(o_ref.dtype)

    _attend_all_tiles(make_chains, finish, seq // t, t)


def _fox_attention(proj, cum, q_g, k_g, bsz, seq):
    hw = C_DIM
    grp = C_GROUP
    gw = grp * hw
    ng = C_HEADS // grp
    blocks = 5 * seq * gw * 2 + grp * 8 * seq * 4
    scratch = grp * seq * 3 * hw * 2
    cum3 = cum.reshape(bsz * CUM_ROWS, 1, seq)
    small = lambda shape: pl.BlockSpec(shape, lambda b, h: (0, 0))
    cols = lambda first: pl.BlockSpec((seq, gw), lambda b, h: (b, first * ng + h))
    return pl.pallas_call(
        functools.partial(_fox_kernel, group=grp),
        grid=(bsz, ng),
        in_specs=[
            cols(0), cols(1), cols(2), cols(3),
            pl.BlockSpec((grp, 1, seq), lambda b, h: (b * (CUM_ROWS // grp) + h, 0, 0)),
            small((1, hw)), small((1, hw)),
        ],
        out_specs=cols(0),
        out_shape=jax.ShapeDtypeStruct((bsz * seq, C_WIDTH), BF16),
        scratch_shapes=[pltpu.VMEM((grp, seq, 2 * hw), BF16), pltpu.VMEM((grp, hw + ONES_ROWS, seq), BF16)],
        compiler_params=pltpu.CompilerParams(
            dimension_semantics=("parallel", "parallel"),
            vmem_limit_bytes=_vmem_limit(blocks, scratch, ATT_TEMP_BYTES)),
        name="fox_attn",
    )(proj, proj, proj, proj, cum3, q_g.reshape(1, hw), k_g.reshape(1, hw))


def _mix_out_kernel(*refs, has_conv, tiles_per_seq):
    refs = list(refs)
    attn_ref = refs.pop(0)
    conv_refs = [refs.pop(0) for _ in range(8)] if has_conv else None
    xq_ref, xz_ref, kv_ref, xqg_ref, w_ref, h_ref, o_ref = refs

    scores = _xattn_scores(xq_ref, kv_ref, xqg_ref)
    width = attn_ref.shape[1]
    acc = h_ref[...] + jnp.dot(attn_ref[...], w_ref[:width, :], preferred_element_type=F32)
    if has_conv:
        starts_sequence = pl.program_id(0) % tiles_per_seq == 0
        yb = _conv_gated(*conv_refs, starts_sequence)
        acc = acc + jnp.dot(yb, w_ref[width:width + B_WIDTH, :], preferred_element_type=F32)
        width += B_WIDTH
    yx = _xattn_gated(scores, xz_ref, kv_ref)
    o_ref[...] = acc + jnp.dot(yx, w_ref[width:, :], preferred_element_type=F32)


def _mix_out(y_attn, proj, kv, x_qg, w_out, h, seq, x_col_block, conv=None):
    t, d = h.shape
    tm = OUT_TM
    tiles_per_seq = seq // tm
    aw = y_attn.shape[1]
    rows = lambda width, col: pl.BlockSpec((tm, width), lambda i: (i, col))
    in_specs = [rows(aw, 0)]
    args = [y_attn]
    blocks = tm * aw * 2 + 2 * tm * X_WIDTH * 2 + MEM_LEN * 2 * X_WIDTH * 2 + w_out.size * 2 + 2 * tm * d * 4
    if conv is not None:
        cw, cb, base = conv
        prev = lambda col: pl.BlockSpec(
            (HALO, B_WIDTH), lambda i: (jnp.maximum(i * (tm // HALO) - 1, 0), col))
        in_specs += [rows(B_WIDTH, base), rows(B_WIDTH, base + 1), prev(base), prev(base + 1),
                     rows(B_WIDTH, base + 2), rows(B_WIDTH, base + 3),
                     pl.BlockSpec((CONV_W, B_WIDTH), lambda i: (0, 0)),
                     pl.BlockSpec((1, B_WIDTH), lambda i: (0, 0))]
        args += [proj] * 6 + [cw, cb.reshape(1, B_WIDTH)]
        blocks += 4 * tm * B_WIDTH * 2
    in_specs += [rows(X_WIDTH, x_col_block), rows(X_WIDTH, x_col_block + 1),
                 pl.BlockSpec((MEM_LEN, 2 * X_WIDTH), lambda i: (i // tiles_per_seq, 0)),
                 pl.BlockSpec((1, X_DIM), lambda i: (0, 0)),
                 pl.BlockSpec(w_out.shape, lambda i: (0, 0)),
                 pl.BlockSpec((tm, d), lambda i: (i, 0))]
    args += [proj, proj, kv, x_qg.reshape(1, X_DIM), w_out, h]
    return pl.pallas_call(
        functools.partial(_mix_out_kernel, has_conv=conv is not None, tiles_per_seq=tiles_per_seq),
        grid=(t // tm,),
        in_specs=in_specs,
        out_specs=pl.BlockSpec((tm, d), lambda i: (i, 0)),
        out_shape=jax.ShapeDtypeStruct((t, d), F32),
        compiler_params=pltpu.CompilerParams(
            dimension_semantics=("parallel",),
            vmem_limit_bytes=_vmem_limit(blocks, 0, 3 * tm * d * 4 + 8 * tm * B_WIDTH * 4)),
        name="mix_out",
    )(*args)


def _even_layer(h, mem2d, layer, bsz, seq, norm_g, w_in, w_out, a_qn, a_kn, a_lam, a_on,
                b_cw, b_cb, x_qn, x_kn, mem_g, w_mem_kv):
    lam_init = 0.8 - 0.6 * math.exp(-0.3 * layer)
    proj = _norm_proj(h, norm_g, w_in.astype(BF16))
    ya = _diff_attention(proj, a_qn, a_kn, a_lam, a_on, lam_init, bsz, seq)
    kv = _mem_kv(mem2d, mem_g, w_mem_kv.astype(BF16), x_kn, bsz)
    return _mix_out(ya, proj, kv, x_qn, w_out.astype(BF16), h, seq,
                    x_col_block=(4 * A_WIDTH + 4 * B_WIDTH) // X_WIDTH,
                    conv=(b_cw, b_cb, 4 * A_WIDTH // B_WIDTH))


def _odd_layer(h, mem2d, bsz, seq, norm_g, w_in, w_out, c_qn, c_kn, c_fb, x_qn, x_kn,
               mem_g, w_mem_kv):
    main = 4 * C_WIDTH
    w_t = w_in.T
    w_main = jnp.concatenate([w_t[:main], w_t[main + C_HEADS:]], axis=0).astype(BF16)
    w_f = jnp.pad(w_t[main:main + C_HEADS], ((0, LANES - C_HEADS), (0, 0))).astype(BF16)
    proj, f_logits = _norm_proj(h, norm_g, w_main, w_f, w_rows_are_outputs=True)
    cum = _forget_cum(f_logits, c_fb, bsz, seq)
    yc = _fox_attention(proj, cum, c_qn, c_kn, bsz, seq)
    kv = _mem_kv(mem2d, mem_g, w_mem_kv.astype(BF16), x_kn, bsz)
    return _mix_out(yc, proj, kv, x_qn, w_out.astype(BF16), h, seq, x_col_block=main // X_WIDTH)


def kernel(x, mem, e_norm_g, e_w_in, e_w_out, e_a_q_norm_g, e_a_k_norm_g, e_a_lambda,
           e_a_out_norm_g, e_b_conv_w, e_b_conv_b, e_x_q_norm_g, e_x_k_norm_g, e_mem_norm_g,
           e_w_mem_kv, o_norm_g, o_w_in, o_w_out, o_c_q_norm_g, o_c_k_norm_g, o_c_forget_b,
           o_x_q_norm_g, o_x_k_norm_g, o_mem_norm_g, o_w_mem_kv):
    bsz, seq, d = x.shape
    h = x.reshape(bsz * seq, d)
    mem2d = mem.reshape(bsz * MEM_LEN, d)
    depth = e_w_in.shape[0] + o_w_in.shape[0]
    for layer in range(depth):
        i = layer // 2
        if layer % 2 == 0:
            h = _even_layer(h, mem2d, layer, bsz, seq, e_norm_g[i], e_w_in[i], e_w_out[i],
                            e_a_q_norm_g[i], e_a_k_norm_g[i], e_a_lambda[i], e_a_out_norm_g[i],
                            e_b_conv_w[i], e_b_conv_b[i], e_x_q_norm_g[i], e_x_k_norm_g[i],
                            e_mem_norm_g[i], e_w_mem_kv[i])
        else:
            h = _odd_layer(h, mem2d, bsz, seq, o_norm_g[i], o_w_in[i], o_w_out[i],
                           o_c_q_norm_g[i], o_c_k_norm_g[i], o_c_forget_b[i],
                           o_x_q_norm_g[i], o_x_k_norm_g[i], o_mem_norm_g[i], o_w_mem_kv[i])
    return h.reshape(bsz, seq, d)
```

```python
import functools
import math

import numpy as np
import jax
import jax.numpy as jnp
from jax import lax
from jax.experimental import pallas as pl
from jax.experimental.pallas import tpu as pltpu

F32 = jnp.float32
BF16 = jnp.bfloat16

D_MODEL = 1024
MEM_LEN = 256
EPS = 1e-6
NEG = -1e30
LOG2E = math.log2(math.e)

A_HEADS = 8
A_QK_DIM = 64
A_V_DIM = 128
A_WIDTH = 1024
B_WIDTH = 512
CONV_W = 3
C_HEADS = 12
C_DIM = 128
C_WIDTH = 1536
X_HEADS = 4
X_DIM = 128
X_WIDTH = 512
MIX = 2048

LANES = 128
ONES_ROWS = 16
CUM_ROWS = 16
VMEM_CAP = 56 * 1024 * 1024

PROJ_TM = 512
OUT_TM = 512
HALO = 8
ATT_T = 256
A_GROUP = 2
C_GROUP = 4
ATT_TEMP_BYTES = 20 << 20


def _vmem_limit(block_bytes, scratch_bytes=0, temp_bytes=0):
    need = 2 * block_bytes + scratch_bytes + temp_bytes + (4 << 20)
    return int(min(max(need, 16 << 20), VMEM_CAP))


def _silu(z):
    return z * jax.nn.sigmoid(z)


def _nt_dot(a, b):
    return lax.dot_general(a, b, (((1,), (1,)), ((), ())), preferred_element_type=F32)


def _proj_kernel(*refs, has_side, w_rows_are_outputs, chunks):
    if has_side:
        x_ref, g_ref, w_ref, gain_ref, ws_ref, o_ref, side_ref = refs
    else:
        x_ref, g_ref, w_ref, gain_ref, o_ref = refs
    matmul = _nt_dot if w_rows_are_outputs else functools.partial(jnp.dot, preferred_element_type=F32)

    x = x_ref[...]
    ms = jnp.mean(x * x, axis=-1, keepdims=True)
    xn = (x * lax.rsqrt(ms + EPS) * g_ref[...]).astype(BF16)
    if has_side:
        side_ref[...] = matmul(xn, ws_ref[...])
    for c0, width, kind in chunks:
        cols = slice(c0, c0 + width)
        w = w_ref[cols, :] if w_rows_are_outputs else w_ref[:, cols]
        y = matmul(xn, w)
        if kind == "silu":
            o_ref[:, cols] = _silu(y).astype(o_ref.dtype)
        elif kind == "plain":
            o_ref[:, cols] = y.astype(o_ref.dtype)
        else:
            for h0 in range(0, width, LANES):
                yh = y[:, h0:h0 + LANES]
                gain = gain_ref[:, c0 + h0:c0 + h0 + LANES]
                o_ref[:, c0 + h0:c0 + h0 + LANES] = (yh * _group_rms_inv(yh, kind) * gain).astype(o_ref.dtype)


def _norm_proj(h, g, w, chunks, gains, w_side=None, w_rows_are_outputs=False):
    t, d = h.shape
    n = w.shape[0] if w_rows_are_outputs else w.shape[1]
    assert sum(width for _, width, _ in chunks) == n
    tm = PROJ_TM
    resident = lambda shape: pl.BlockSpec(shape, lambda i: (0, 0), pipeline_mode=pl.Buffered(1))
    in_specs = [
        pl.BlockSpec((tm, d), lambda i: (i, 0)),
        pl.BlockSpec((1, d), lambda i: (0, 0)),
        resident(w.shape),
        pl.BlockSpec((1, n), lambda i: (0, 0)),
    ]
    out_specs = pl.BlockSpec((tm, n), lambda i: (i, 0))
    out_shape = jax.ShapeDtypeStruct((t, n), BF16)
    args = [h, g.reshape(1, d), w, gains]
    blocks = tm * d * 4 + tm * n * 2 + 8 * n * 4
    single = w.size * 2
    if w_side is not None:
        ns = w_side.shape[0] if w_rows_are_outputs else w_side.shape[1]
        in_specs.append(resident(w_side.shape))
        out_specs = [out_specs, pl.BlockSpec((tm, ns), lambda i: (i, 0))]
        out_shape = [out_shape, jax.ShapeDtypeStruct((t, ns), F32)]
        args.append(w_side)
        blocks += tm * ns * 4
        single += w_side.size * 2
    return pl.pallas_call(
        functools.partial(_proj_kernel, has_side=w_side is not None,
                          w_rows_are_outputs=w_rows_are_outputs, chunks=tuple(chunks)),
        grid=(t // tm,),
        in_specs=in_specs,
        out_specs=out_specs,
        out_shape=out_shape,
        compiler_params=pltpu.CompilerParams(
            dimension_semantics=("parallel",),
            vmem_limit_bytes=_vmem_limit(
                blocks, single, 3 * tm * max(wd for _, wd, _ in chunks) * 4 + 2 * tm * d * 4)),
        name="norm_proj",
    )(*args)


def _group_rms_inv(x, group):
    x2 = x * x
    if group == LANES:
        return lax.rsqrt(jnp.mean(x2, axis=-1, keepdims=True) + EPS)
    assert 2 * group == LANES
    first = lax.broadcasted_iota(jnp.int32, x.shape, 1) < group
    ms_a = jnp.sum(jnp.where(first, x2, 0.0), axis=-1, keepdims=True) * (1.0 / group)
    ms_b = jnp.sum(jnp.where(first, 0.0, x2), axis=-1, keepdims=True) * (1.0 / group)
    return lax.rsqrt(jnp.where(first, ms_a, ms_b) + EPS)


def _split3(c):
    c1 = c.astype(BF16).astype(F32)
    r = c - c1
    c2 = r.astype(BF16).astype(F32)
    c3 = (r - c2).astype(BF16).astype(F32)
    return c1, c2, c3


def _bias_lanes(term, key_side):
    n = term.shape[1]
    parts = jnp.concatenate(_split3(term), axis=0)
    ones = jnp.ones((3, n), F32)
    pad = jnp.zeros((10, n), F32)
    rows = jnp.concatenate([parts, ones, pad] if key_side else [ones, parts, pad], axis=0)
    sel = (lax.broadcasted_iota(jnp.int32, (16, LANES), 0)
           == lax.broadcasted_iota(jnp.int32, (16, LANES), 1)).astype(BF16)
    lanes = lax.dot_general(rows.astype(BF16), sel, (((0,), (0,)), ((), ())),
                            preferred_element_type=F32)
    return lanes.astype(BF16)


def _score_phase(chains, c, t):
    krow = lax.broadcasted_iota(jnp.int32, (t, t), 0)
    qcol = lax.broadcasted_iota(jnp.int32, (t, t), 1)
    lo, hi = c * t, (c + 1) * t
    scores = []
    for qa, ka_ref, _ in chains:
        s_d = jnp.where(krow <= qcol, _nt_dot(ka_ref[lo:hi, :], qa), NEG)
        m = jnp.max(s_d, axis=0, keepdims=True)
        s_f = None
        if c > 0:
            s_f = _nt_dot(ka_ref[0:lo, :], qa)
            m = jnp.maximum(m, jnp.max(s_f, axis=0, keepdims=True))
        scores.append((s_d, s_f, m))
    return scores


def _value_phase(chains, scores, c, t):
    lo, hi = c * t, (c + 1) * t
    outs = []
    for (_, _, vt_ref), (s_d, s_f, m) in zip(chains, scores):
        dv = vt_ref.shape[0] - ONES_ROWS
        p_d = jnp.exp2(s_d - m).astype(BF16)
        acc = jnp.dot(vt_ref[:, lo:hi], p_d, preferred_element_type=F32)
        if c > 0:
            p_f = jnp.exp2(s_f - m).astype(BF16)
            acc = acc + jnp.dot(vt_ref[:, 0:lo], p_f, preferred_element_type=F32)
        outs.append((acc[:dv], acc[dv:dv + 1]))
    return outs


def _attend_all_tiles(make_chains, finish, n_tiles, t):
    order = list(range(1, n_tiles, 2)) + list(range(n_tiles - 1 - (n_tiles % 2 == 0), -1, -2))
    assert sorted(order) == list(range(n_tiles))
    chains = make_chains(order[0])
    scores = _score_phase(chains, order[0], t)
    for pos, c in enumerate(order):
        if pos + 1 < n_tiles:
            nxt = order[pos + 1]
            next_chains = make_chains(nxt)
            next_scores = _score_phase(next_chains, nxt, t)
        finish(c, _value_phase(chains, scores, c, t))
        if pos + 1 < n_tiles:
            chains, scores = next_chains, next_scores


def _diff_kernel(slopes_ref, q_ref, k_ref, v_ref, z_ref, lam_ref, og_ref,
                 o_ref, ka_ref, vt_ref, *, lam_init, group):
    t = ATT_T
    hw = A_V_DIM
    seq = k_ref.shape[0]
    hg = pl.program_id(1)

    k_pos = lax.broadcasted_iota(jnp.int32, (1, seq), 1).astype(F32)
    for g in range(group):
        cs = slice(g * hw, (g + 1) * hw)
        ka_ref[g, :, :hw] = k_ref[:, cs]
        ka_ref[g, :, hw:] = _bias_lanes(LOG2E * slopes_ref[hg * group + g] * k_pos, True)
        vt_ref[g, :hw] = v_ref[:, cs].astype(F32).T.astype(BF16)
        vt_ref[g, hw:] = jnp.ones((ONES_ROWS, seq), BF16)

    lo = lax.broadcasted_iota(jnp.int32, (t, hw), 1) < A_QK_DIM
    lp = lam_ref[...]
    lam = (jnp.exp(jnp.sum(lp[0:1, :] * lp[1:2, :], axis=-1, keepdims=True))
           - jnp.exp(jnp.sum(lp[2:3, :] * lp[3:4, :], axis=-1, keepdims=True)) + lam_init)

    def make_chains(c):
        rows = slice(c * t, (c + 1) * t)
        q_pos = (c * t + lax.broadcasted_iota(jnp.int32, (1, t), 1)).astype(F32)
        chains = []
        for g in range(group):
            qn = q_ref[rows, g * hw:(g + 1) * hw]
            qb = _bias_lanes(-LOG2E * slopes_ref[hg * group + g] * q_pos, False)
            zero = jnp.zeros_like(qn)
            q1 = jnp.concatenate([jnp.where(lo, qn, zero), qb], axis=1)
            q2 = jnp.concatenate([jnp.where(lo, zero, qn), qb], axis=1)
            chains += [(q1, ka_ref.at[g], vt_ref.at[g]), (q2, ka_ref.at[g], vt_ref.at[g])]
        return chains

    def finish(c, outs):
        rows = slice(c * t, (c + 1) * t)
        for g in range(group):
            cs = slice(g * hw, (g + 1) * hw)
            (a1, l1), (a2, l2) = outs[2 * g], outs[2 * g + 1]
            o = (a1 / l1 - lam * (a2 / l2)).T
            ms = jnp.mean(o * o, axis=-1, keepdims=True)
            on = o * lax.rsqrt(ms + EPS) * og_ref[...] * (1.0 - lam_init)
            o_ref[rows, cs] = (on * z_ref[rows, cs].astype(F32)).astype(o_ref.dtype)

    _attend_all_tiles(make_chains, finish, seq // t, t)


def _diff_attention(proj, lam_params, out_g, lam_init, bsz, seq):
    hw = A_V_DIM
    grp = A_GROUP
    gw = grp * hw
    ng = A_HEADS // grp
    slopes = jnp.asarray(np.array([2.0 ** (-8.0 * (i + 1) / A_HEADS) for i in range(A_HEADS)],
                                  dtype=np.float32))
    blocks = 5 * seq * gw * 2
    scratch = grp * seq * 3 * hw * 2
    small = lambda shape: pl.BlockSpec(shape, lambda b, h: (0, 0))
    cols = lambda first: pl.BlockSpec((seq, gw), lambda b, h: (b, first * ng + h))
    return pl.pallas_call(
        functools.partial(_diff_kernel, lam_init=lam_init, group=grp),
        grid=(bsz, ng),
        in_specs=[
            pl.BlockSpec(memory_space=pltpu.SMEM),
            cols(0), cols(1), cols(2), cols(3),
            small((4, A_QK_DIM)), small((1, hw)),
        ],
        out_specs=cols(0),
        out_shape=jax.ShapeDtypeStruct((bsz * seq, A_WIDTH), BF16),
        scratch_shapes=[pltpu.VMEM((grp, seq, 2 * hw), BF16), pltpu.VMEM((grp, hw + ONES_ROWS, seq), BF16)],
        compiler_params=pltpu.CompilerParams(
            dimension_semantics=("parallel", "parallel"),
            vmem_limit_bytes=_vmem_limit(blocks, scratch, ATT_TEMP_BYTES)),
        name="diff_attn",
    )(slopes, proj, proj, proj, proj, lam_params, out_g.reshape(1, hw))


def _conv_gated(h_ref, c_ref, hp_ref, cp_ref, b_ref, z_ref, w_ref, bias_ref, starts_sequence):
    u = c_ref[...].astype(F32) * h_ref[...].astype(F32)
    up = cp_ref[...].astype(F32) * hp_ref[...].astype(F32)
    up = jnp.where(starts_sequence, 0.0, up)
    row = lax.broadcasted_iota(jnp.int32, u.shape, 0)
    u1 = jnp.where(row >= 1, pltpu.roll(u, 1, 0), up[HALO - 1:HALO, :])
    u2 = jnp.where(row >= 2, pltpu.roll(u, 2, 0),
                   jnp.where(row == 1, up[HALO - 1:HALO, :], up[HALO - 2:HALO - 1, :]))
    w = w_ref[...]
    y = w[0:1, :] * u2 + w[1:2, :] * u1 + w[2:3, :] * u + bias_ref[...]
    return (b_ref[...].astype(F32) * y * z_ref[...].astype(F32)).astype(BF16)


def _mem_kv_kernel(mem_ref, g_ref, w_ref, kg_ref, o_ref):
    x = mem_ref[...]
    ms = jnp.mean(x * x, axis=-1, keepdims=True)
    xn = (x * lax.rsqrt(ms + EPS) * g_ref[...]).astype(BF16)
    kv = jnp.dot(xn, w_ref[...], preferred_element_type=F32)
    for hd in range(X_HEADS):
        k = kv[:, hd * X_DIM:(hd + 1) * X_DIM]
        kms = jnp.mean(k * k, axis=-1, keepdims=True)
        o_ref[:, hd * X_DIM:(hd + 1) * X_DIM] = (k * lax.rsqrt(kms + EPS) * kg_ref[...]).astype(BF16)
    o_ref[:, X_WIDTH:] = kv[:, X_WIDTH:].astype(BF16)


def _mem_kv(mem2d, g, w, k_g, bsz):
    d = mem2d.shape[1]
    blocks = MEM_LEN * d * 4 + w.size * 2 + MEM_LEN * 2 * X_WIDTH * 2
    return pl.pallas_call(
        _mem_kv_kernel,
        grid=(bsz,),
        in_specs=[pl.BlockSpec((MEM_LEN, d), lambda b: (b, 0)),
                  pl.BlockSpec((1, d), lambda b: (0, 0)),
                  pl.BlockSpec(w.shape, lambda b: (0, 0)),
                  pl.BlockSpec((1, X_DIM), lambda b: (0, 0))],
        out_specs=pl.BlockSpec((MEM_LEN, 2 * X_WIDTH), lambda b: (b, 0)),
        out_shape=jax.ShapeDtypeStruct((bsz * MEM_LEN, 2 * X_WIDTH), BF16),
        compiler_params=pltpu.CompilerParams(
            dimension_semantics=("parallel",),
            vmem_limit_bytes=_vmem_limit(blocks, 0, 4 * MEM_LEN * d * 4)),
        name="mem_kv",
    )(mem2d, g.reshape(1, d), w, k_g.reshape(1, X_DIM))


def _xattn_scores(q_ref, kv_ref):
    return [_nt_dot(q_ref[:, hd * X_DIM:(hd + 1) * X_DIM], kv_ref[:, hd * X_DIM:(hd + 1) * X_DIM])
            for hd in range(X_HEADS)]


def _xattn_gated(scores, z_ref, kv_ref):
    heads = []
    for hd, s in enumerate(scores):
        cs = slice(hd * X_DIM, (hd + 1) * X_DIM)
        m = jnp.max(s, axis=-1, keepdims=True)
        p = jnp.exp(s - m)
        l = jnp.sum(p, axis=-1, keepdims=True)
        vs = slice(X_WIDTH + hd * X_DIM, X_WIDTH + (hd + 1) * X_DIM)
        o = jnp.dot(p.astype(BF16), kv_ref[:, vs], preferred_element_type=F32) / l
        heads.append((o * z_ref[:, cs].astype(F32)).astype(BF16))
    return jnp.concatenate(heads, axis=1)


def _forget_cum_kernel(f_ref, b_ref, cum_ref):
    x = f_ref[...] + b_ref[...]
    c = jnp.minimum(x, 0.0) - jnp.log1p(jnp.exp(-jnp.abs(x)))
    n = c.shape[0]
    row = lax.broadcasted_iota(jnp.int32, c.shape, 0)
    shift = 1
    while shift < n:
        c = c + jnp.where(row >= shift, pltpu.roll(c, shift, 0), 0.0)
        shift *= 2
    cum_ref[...] = c.T[:cum_ref.shape[0], :]


def _forget_cum(f_logits, f_bias, bsz, seq):
    bias = jnp.zeros((1, LANES), F32).at[0, :C_HEADS].set(f_bias)
    blocks = seq * LANES * 4 + CUM_ROWS * seq * 4
    return pl.pallas_call(
        _forget_cum_kernel,
        grid=(bsz,),
        in_specs=[pl.BlockSpec((seq, LANES), lambda b: (b, 0)),
                  pl.BlockSpec((1, LANES), lambda b: (0, 0))],
        out_specs=pl.BlockSpec((CUM_ROWS, seq), lambda b: (b, 0)),
        out_shape=jax.ShapeDtypeStruct((bsz * CUM_ROWS, seq), F32),
        compiler_params=pltpu.CompilerParams(
            dimension_semantics=("parallel",),
            vmem_limit_bytes=_vmem_limit(blocks, 0, 6 * seq * LANES * 4)),
        name="forget_cum",
    )(f_logits, bias)


def _fox_kernel(q_ref, k_ref, v_ref, z_ref, cum_ref, o_ref, ka_ref, vt_ref, *, group):
    t = ATT_T
    hw = C_DIM
    seq = k_ref.shape[0]

    for g in range(group):
        cs = slice(g * hw, (g + 1) * hw)
        ka_ref[g, :, :hw] = k_ref[:, cs]
        ka_ref[g, :, hw:] = _bias_lanes(-LOG2E * cum_ref[g], True)
        vt_ref[g, :hw] = v_ref[:, cs].astype(F32).T.astype(BF16)
        vt_ref[g, hw:] = jnp.ones((ONES_ROWS, seq), BF16)

    def make_chains(c):
        rows = slice(c * t, (c + 1) * t)
        chains = []
        for g in range(group):
            qb = _bias_lanes(LOG2E * cum_ref[g, :, rows], False)
            qa = jnp.concatenate([q_ref[rows, g * hw:(g + 1) * hw], qb], axis=1)
            chains.append((qa, ka_ref.at[g], vt_ref.at[g]))
        return chains

    def finish(c, outs):
        rows = slice(c * t, (c + 1) * t)
        for g, (acc, l) in enumerate(outs):
            cs = slice(g * hw, (g + 1) * hw)
            o_ref[rows, cs] = ((acc / l).T * z_ref[rows, cs].astype(F32)).astype(o_ref.dtype)

    _attend_all_tiles(make_chains, finish, seq // t, t)


def _fox_attention(proj, cum, bsz, seq):
    hw = C_DIM
    grp = C_GROUP
    gw = grp * hw
    ng = C_HEADS // grp
    blocks = 5 * seq * gw * 2 + grp * 8 * seq * 4
    scratch = grp * seq * 3 * hw * 2
    cum3 = cum.reshape(bsz * CUM_ROWS, 1, seq)
    cols = lambda first: pl.BlockSpec((seq, gw), lambda b, h: (b, first * ng + h))
    return pl.pallas_call(
        functools.partial(_fox_kernel, group=grp),
        grid=(bsz, ng),
        in_specs=[
            cols(0), cols(1), cols(2), cols(3),
            pl.BlockSpec((grp, 1, seq), lambda b, h: (b * (CUM_ROWS // grp) + h, 0, 0)),
        ],
        out_specs=cols(0),
        out_shape=jax.ShapeDtypeStruct((bsz * seq, C_WIDTH), BF16),
        scratch_shapes=[pltpu.VMEM((grp, seq, 2 * hw), BF16), pltpu.VMEM((grp, hw + ONES_ROWS, seq), BF16)],
        compiler_params=pltpu.CompilerParams(
            dimension_semantics=("parallel", "parallel"),
            vmem_limit_bytes=_vmem_limit(blocks, scratch, ATT_TEMP_BYTES)),
        name="fox_attn",
    )(proj, proj, proj, proj, cum3)


def _mix_out_kernel(*refs, has_conv, tiles_per_seq):
    refs = list(refs)
    attn_ref = refs.pop(0)
    conv_refs = [refs.pop(0) for _ in range(8)] if has_conv else None
    xq_ref, xz_ref, kv_ref, w_ref, h_ref, o_ref = refs

    scores = _xattn_scores(xq_ref, kv_ref)
    width = attn_ref.shape[1]
    acc = h_ref[...] + jnp.dot(attn_ref[...], w_ref[:width, :], preferred_element_type=F32)
    if has_conv:
        starts_sequence = pl.program_id(0) % tiles_per_seq == 0
        yb = _conv_gated(*conv_refs, starts_sequence)
        acc = acc + jnp.dot(yb, w_ref[width:width + B_WIDTH, :], preferred_element_type=F32)
        width += B_WIDTH
    yx = _xattn_gated(scores, xz_ref, kv_ref)
    o_ref[...] = acc + jnp.dot(yx, w_ref[width:, :], preferred_element_type=F32)


def _mix_out(y_attn, proj, kv, w_out, h, seq, x_col_block, conv=None):
    t, d = h.shape
    tm = OUT_TM
    tiles_per_seq = seq // tm
    aw = y_attn.shape[1]
    rows = lambda width, col: pl.BlockSpec((tm, width), lambda i: (i, col))
    in_specs = [rows(aw, 0)]
    args = [y_attn]
    blocks = tm * aw * 2 + 2 * tm * X_WIDTH * 2 + MEM_LEN * 2 * X_WIDTH * 2 + w_out.size * 2 + 2 * tm * d * 4
    if conv is not None:
        cw, cb, base = conv
        prev = lambda col: pl.BlockSpec(
            (HALO, B_WIDTH), lambda i: (jnp.maximum(i * (tm // HALO) - 1, 0), col))
        in_specs += [rows(B_WIDTH, base), rows(B_WIDTH, base + 1), prev(base), prev(base + 1),
                     rows(B_WIDTH, base + 2), rows(B_WIDTH, base + 3),
                     pl.BlockSpec((CONV_W, B_WIDTH), lambda i: (0, 0)),
                     pl.BlockSpec((1, B_WIDTH), lambda i: (0, 0))]
        args += [proj] * 6 + [cw, cb.reshape(1, B_WIDTH)]
        blocks += 4 * tm * B_WIDTH * 2
    in_specs += [rows(X_WIDTH, x_col_block), rows(X_WIDTH, x_col_block + 1),
                 pl.BlockSpec((MEM_LEN, 2 * X_WIDTH), lambda i: (i // tiles_per_seq, 0)),
                 pl.BlockSpec(w_out.shape, lambda i: (0, 0)),
                 pl.BlockSpec((tm, d), lambda i: (i, 0))]
    args += [proj, proj, kv, w_out, h]
    return pl.pallas_call(
        functools.partial(_mix_out_kernel, has_conv=conv is not None, tiles_per_seq=tiles_per_seq),
        grid=(t // tm,),
        in_specs=in_specs,
        out_specs=pl.BlockSpec((tm, d), lambda i: (i, 0)),
        out_shape=jax.ShapeDtypeStruct((t, d), F32),
        compiler_params=pltpu.CompilerParams(
            dimension_semantics=("parallel",),
            vmem_limit_bytes=_vmem_limit(blocks, 0, 3 * tm * d * 4 + 8 * tm * B_WIDTH * 4)),
        name="mix_out",
    )(*args)


def _even_layer(h, mem2d, layer, bsz, seq, norm_g, w_in, w_out, a_qn, a_kn, a_lam, a_on,
                b_cw, b_cb, x_qn, x_kn, mem_g, w_mem_kv):
    lam_init = 0.8 - 0.6 * math.exp(-0.3 * layer)
    conv0 = 4 * A_WIDTH
    x0 = conv0 + 4 * B_WIDTH
    chunks = [(0, A_WIDTH, A_QK_DIM), (A_WIDTH, A_WIDTH, A_QK_DIM), (2 * A_WIDTH, A_WIDTH, "plain"),
              (3 * A_WIDTH, A_WIDTH, "silu"), (conv0, 3 * B_WIDTH, "plain"),
              (conv0 + 3 * B_WIDTH, B_WIDTH, "silu"), (x0, X_WIDTH, X_DIM), (x0 + X_WIDTH, X_WIDTH, "silu")]
    gains = jnp.ones((1, w_in.shape[1]), F32)
    gains = gains.at[0, :A_WIDTH].set(jnp.tile(a_qn, 2 * A_HEADS) * (LOG2E * A_QK_DIM ** -0.5))
    gains = gains.at[0, A_WIDTH:2 * A_WIDTH].set(jnp.tile(a_kn, 2 * A_HEADS))
    gains = gains.at[0, x0:x0 + X_WIDTH].set(jnp.tile(x_qn, X_HEADS) * (X_DIM ** -0.5))
    proj = _norm_proj(h, norm_g, w_in.astype(BF16), chunks, gains)
    ya = _diff_attention(proj, a_lam, a_on, lam_init, bsz, seq)
    kv = _mem_kv(mem2d, mem_g, w_mem_kv.astype(BF16), x_kn, bsz)
    return _mix_out(ya, proj, kv, w_out.astype(BF16), h, seq, x_col_block=x0 // X_WIDTH,
                    conv=(b_cw, b_cb, conv0 // B_WIDTH))


def _odd_layer(h, mem2d, bsz, seq, norm_g, w_in, w_out, c_qn, c_kn, c_fb, x_qn, x_kn,
               mem_g, w_mem_kv):
    main = 4 * C_WIDTH
    w_t = w_in.T
    w_main = jnp.concatenate([w_t[:main], w_t[main + C_HEADS:]], axis=0).astype(BF16)
    w_f = jnp.pad(w_t[main:main + C_HEADS], ((0, LANES - C_HEADS), (0, 0))).astype(BF16)
    chunks = [(0, C_WIDTH, C_DIM), (C_WIDTH, C_WIDTH, C_DIM), (2 * C_WIDTH, C_WIDTH, "plain"),
              (3 * C_WIDTH, C_WIDTH, "silu"), (main, X_WIDTH, X_DIM), (main + X_WIDTH, X_WIDTH, "silu")]
    gains = jnp.ones((1, main + 2 * X_WIDTH), F32)
    gains = gains.at[0, :C_WIDTH].set(jnp.tile(c_qn, C_HEADS) * (LOG2E * C_DIM ** -0.5))
    gains = gains.at[0, C_WIDTH:2 * C_WIDTH].set(jnp.tile(c_kn, C_HEADS))
    gains = gains.at[0, main:main + X_WIDTH].set(jnp.tile(x_qn, X_HEADS) * (X_DIM ** -0.5))
    proj, f_logits = _norm_proj(h, norm_g, w_main, chunks, gains, w_f, w_rows_are_outputs=True)
    cum = _forget_cum(f_logits, c_fb, bsz, seq)
    yc = _fox_attention(proj, cum, bsz, seq)
    kv = _mem_kv(mem2d, mem_g, w_mem_kv.astype(BF16), x_kn, bsz)
    return _mix_out(yc, proj, kv, w_out.astype(BF16), h, seq, x_col_block=main // X_WIDTH)


def kernel(x, mem, e_norm_g, e_w_in, e_w_out, e_a_q_norm_g, e_a_k_norm_g, e_a_lambda,
           e_a_out_norm_g, e_b_conv_w, e_b_conv_b, e_x_q_norm_g, e_x_k_norm_g, e_mem_norm_g,
           e_w_mem_kv, o_norm_g, o_w_in, o_w_out, o_c_q_norm_g, o_c_k_norm_g, o_c_forget_b,
           o_x_q_norm_g, o_x_k_norm_g, o_mem_norm_g, o_w_mem_kv):
    bsz, seq, d = x.shape
    h = x.reshape(bsz * seq, d)
    mem2d = mem.reshape(bsz * MEM_LEN, d)
    depth = e_w_in.shape[0] + o_w_in.shape[0]
    for layer in range(depth):
        i = layer // 2
        if layer % 2 == 0:
            h = _even_layer(h, mem2d, layer, bsz, seq, e_norm_g[i], e_w_in[i], e_w_out[i],
                            e_a_q_norm_g[i], e_a_k_norm_g[i], e_a_lambda[i], e_a_out_norm_g[i],
                            e_b_conv_w[i], e_b_conv_b[i], e_x_q_norm_g[i], e_x_k_norm_g[i],
                            e_mem_norm_g[i], e_w_mem_kv[i])
        else:
            h = _odd_layer(h, mem2d, bsz, seq, o_norm_g[i], o_w_in[i], o_w_out[i],
                           o_c_q_norm_g[i], o_c_k_norm_g[i], o_c_forget_b[i],
                           o_x_q_norm_g[i], o_x_k_norm_g[i], o_mem_norm_g[i], o_w_mem_kv[i])
    return h.reshape(bsz, seq, d)
```

```python
import functools
import math

import numpy as np
import jax
import jax.numpy as jnp
from jax import lax
from jax.experimental import pallas as pl
from jax.experimental.pallas import tpu as pltpu

F32 = jnp.float32
BF16 = jnp.bfloat16

D_MODEL = 1024
MEM_LEN = 256
EPS = 1e-6
NEG = -1e30
LOG2E = math.log2(math.e)

A_HEADS = 8
A_QK_DIM = 64
A_V_DIM = 128
A_WIDTH = 1024
B_WIDTH = 512
CONV_W = 3
C_HEADS = 12
C_DIM = 128
C_WIDTH = 1536
X_HEADS = 4
X_DIM = 128
X_WIDTH = 512
MIX = 2048

LANES = 128
ONES_ROWS = 16
CUM_ROWS = 16
VMEM_CAP = 56 * 1024 * 1024

PROJ_TM = 512
OUT_TM = 512
HALO = 8
ATT_T = 256
A_GROUP = 2
C_GROUP = 4
ATT_TEMP_BYTES = 20 << 20


def _vmem_limit(block_bytes, scratch_bytes=0, temp_bytes=0):
    need = 2 * block_bytes + scratch_bytes + temp_bytes + (4 << 20)
    return int(min(max(need, 16 << 20), VMEM_CAP))


def _silu(z):
    return z * jax.nn.sigmoid(z)


def _nt_dot(a, b):
    return lax.dot_general(a, b, (((1,), (1,)), ((), ())), preferred_element_type=F32)


def _group_rms_inv(x, group):
    x2 = x * x
    if group == LANES:
        return lax.rsqrt(jnp.mean(x2, axis=-1, keepdims=True) + EPS)
    assert 2 * group == LANES
    first = lax.broadcasted_iota(jnp.int32, x.shape, 1) < group
    ms_a = jnp.sum(jnp.where(first, x2, 0.0), axis=-1, keepdims=True) * (1.0 / group)
    ms_b = jnp.sum(jnp.where(first, 0.0, x2), axis=-1, keepdims=True) * (1.0 / group)
    return lax.rsqrt(jnp.where(first, ms_a, ms_b) + EPS)


def _proj_kernel(*refs, n_weights, has_side, w_rows_are_outputs, chunks):
    x_ref, g_ref, gain_ref = refs[:3]
    w_refs = refs[3:3 + n_weights]
    if has_side:
        ws_ref, o_ref, side_ref = refs[3 + n_weights:]
    else:
        (o_ref,) = refs[3 + n_weights:]
    matmul = _nt_dot if w_rows_are_outputs else functools.partial(jnp.dot, preferred_element_type=F32)

    x = x_ref[...]
    ms = jnp.mean(x * x, axis=-1, keepdims=True)
    xn = (x * lax.rsqrt(ms + EPS) * g_ref[...]).astype(BF16)
    if has_side:
        side_ref[...] = matmul(xn, ws_ref[...])
    for src, w0, c0, width, kind in chunks:
        cols = slice(c0, c0 + width)
        w = w_refs[src][w0:w0 + width, :] if w_rows_are_outputs else w_refs[src][:, w0:w0 + width]
        y = matmul(xn, w)
        if kind == "silu":
            o_ref[:, cols] = _silu(y).astype(o_ref.dtype)
        elif kind == "plain":
            o_ref[:, cols] = y.astype(o_ref.dtype)
        else:
            for h0 in range(0, width, LANES):
                yh = y[:, h0:h0 + LANES]
                gain = gain_ref[:, c0 + h0:c0 + h0 + LANES]
                o_ref[:, c0 + h0:c0 + h0 + LANES] = (yh * _group_rms_inv(yh, kind) * gain).astype(o_ref.dtype)


def _norm_proj(h, g, weights, chunks, gains, w_side=None, w_rows_are_outputs=False):
    t, d = h.shape
    n = gains.shape[1]
    assert sum(width for _, _, _, width, _ in chunks) == n
    tm = PROJ_TM
    resident = lambda shape: pl.BlockSpec(shape, lambda i: (0, 0), pipeline_mode=pl.Buffered(1))
    in_specs = [
        pl.BlockSpec((tm, d), lambda i: (i, 0)),
        pl.BlockSpec((1, d), lambda i: (0, 0)),
        pl.BlockSpec((1, n), lambda i: (0, 0)),
    ] + [resident(w.shape) for w in weights]
    out_specs = pl.BlockSpec((tm, n), lambda i: (i, 0))
    out_shape = jax.ShapeDtypeStruct((t, n), BF16)
    args = [h, g.reshape(1, d), gains, *weights]
    blocks = tm * d * 4 + tm * n * 2 + 8 * n * 4
    single = sum(w.size * 2 for w in weights)
    if w_side is not None:
        ns = w_side.shape[0] if w_rows_are_outputs else w_side.shape[1]
        in_specs.append(resident(w_side.shape))
        out_specs = [out_specs, pl.BlockSpec((tm, ns), lambda i: (i, 0))]
        out_shape = [out_shape, jax.ShapeDtypeStruct((t, ns), F32)]
        args.append(w_side)
        blocks += tm * ns * 4
        single += w_side.size * 2
    return pl.pallas_call(
        functools.partial(_proj_kernel, n_weights=len(weights), has_side=w_side is not None,
                          w_rows_are_outputs=w_rows_are_outputs, chunks=tuple(chunks)),
        grid=(t // tm,),
        in_specs=in_specs,
        out_specs=out_specs,
        out_shape=out_shape,
        compiler_params=pltpu.CompilerParams(
            dimension_semantics=("parallel",),
            vmem_limit_bytes=_vmem_limit(
                blocks, single, 3 * tm * max(c[3] for c in chunks) * 4 + 2 * tm * d * 4)),
        name="norm_proj",
    )(*args)


def _split3(c):
    c1 = c.astype(BF16).astype(F32)
    r = c - c1
    c2 = r.astype(BF16).astype(F32)
    c3 = (r - c2).astype(BF16).astype(F32)
    return c1, c2, c3


def _bias_lanes(term, key_side):
    n = term.shape[1]
    parts = jnp.concatenate(_split3(term), axis=0)
    ones = jnp.ones((3, n), F32)
    pad = jnp.zeros((10, n), F32)
    rows = jnp.concatenate([parts, ones, pad] if key_side else [ones, parts, pad], axis=0)
    sel = (lax.broadcasted_iota(jnp.int32, (16, LANES), 0)
           == lax.broadcasted_iota(jnp.int32, (16, LANES), 1)).astype(BF16)
    lanes = lax.dot_general(rows.astype(BF16), sel, (((0,), (0,)), ((), ())),
                            preferred_element_type=F32)
    return lanes.astype(BF16)


def _score_phase(chains, c, t):
    krow = lax.broadcasted_iota(jnp.int32, (t, t), 0)
    qcol = lax.broadcasted_iota(jnp.int32, (t, t), 1)
    lo, hi = c * t, (c + 1) * t
    scores = []
    for qa, ka_ref, _ in chains:
        s_d = jnp.where(krow <= qcol, _nt_dot(ka_ref[lo:hi, :], qa), NEG)
        m = jnp.max(s_d, axis=0, keepdims=True)
        s_f = None
        if c > 0:
            s_f = _nt_dot(ka_ref[0:lo, :], qa)
            m = jnp.maximum(m, jnp.max(s_f, axis=0, keepdims=True))
        scores.append((s_d, s_f, m))
    return scores


def _value_phase(chains, scores, c, t):
    lo, hi = c * t, (c + 1) * t
    outs = []
    for (_, _, vt_ref), (s_d, s_f, m) in zip(chains, scores):
        dv = vt_ref.shape[0] - ONES_ROWS
        p_d = jnp.exp2(s_d - m).astype(BF16)
        acc = jnp.dot(vt_ref[:, lo:hi], p_d, preferred_element_type=F32)
        if c > 0:
            p_f = jnp.exp2(s_f - m).astype(BF16)
            acc = acc + jnp.dot(vt_ref[:, 0:lo], p_f, preferred_element_type=F32)
        outs.append((acc[:dv], acc[dv:dv + 1]))
    return outs


def _attend_all_tiles(make_chains, finish, n_tiles, t):
    order = list(range(1, n_tiles, 2)) + list(range(n_tiles - 1 - (n_tiles % 2 == 0), -1, -2))
    assert sorted(order) == list(range(n_tiles))
    chains = make_chains(order[0])
    scores = _score_phase(chains, order[0], t)
    for pos, c in enumerate(order):
        if pos + 1 < n_tiles:
            nxt = order[pos + 1]
            next_chains = make_chains(nxt)
            next_scores = _score_phase(next_chains, nxt, t)
        finish(c, _value_phase(chains, scores, c, t))
        if pos + 1 < n_tiles:
            chains, scores = next_chains, next_scores


def _diff_kernel(slopes_ref, q_ref, k_ref, v_ref, z_ref, lam_ref, og_ref,
                 o_ref, ka_ref, vt_ref, *, lam_init, group):
    t = ATT_T
    hw = A_V_DIM
    seq = k_ref.shape[0]
    hg = pl.program_id(1)

    k_pos = lax.broadcasted_iota(jnp.int32, (1, seq), 1).astype(F32)
    for g in range(group):
        cs = slice(g * hw, (g + 1) * hw)
        ka_ref[g, :, :hw] = k_ref[:, cs]
        ka_ref[g, :, hw:] = _bias_lanes(LOG2E * slopes_ref[hg * group + g] * k_pos, True)
        vt_ref[g, :hw] = v_ref[:, cs].astype(F32).T.astype(BF16)
        vt_ref[g, hw:] = jnp.ones((ONES_ROWS, seq), BF16)

    lo = lax.broadcasted_iota(jnp.int32, (t, hw), 1) < A_QK_DIM
    lp = lam_ref[...]
    lam = (jnp.exp(jnp.sum(lp[0:1, :] * lp[1:2, :], axis=-1, keepdims=True))
           - jnp.exp(jnp.sum(lp[2:3, :] * lp[3:4, :], axis=-1, keepdims=True)) + lam_init)

    def make_chains(c):
        rows = slice(c * t, (c + 1) * t)
        q_pos = (c * t + lax.broadcasted_iota(jnp.int32, (1, t), 1)).astype(F32)
        chains = []
        for g in range(group):
            qn = q_ref[rows, g * hw:(g + 1) * hw]
            qb = _bias_lanes(-LOG2E * slopes_ref[hg * group + g] * q_pos, False)
            zero = jnp.zeros_like(qn)
            q1 = jnp.concatenate([jnp.where(lo, qn, zero), qb], axis=1)
            q2 = jnp.concatenate([jnp.where(lo, zero, qn), qb], axis=1)
            chains += [(q1, ka_ref.at[g], vt_ref.at[g]), (q2, ka_ref.at[g], vt_ref.at[g])]
        return chains

    def finish(c, outs):
        rows = slice(c * t, (c + 1) * t)
        for g in range(group):
            cs = slice(g * hw, (g + 1) * hw)
            (a1, l1), (a2, l2) = outs[2 * g], outs[2 * g + 1]
            o = (a1 / l1 - lam * (a2 / l2)).T
            ms = jnp.mean(o * o, axis=-1, keepdims=True)
            on = o * lax.rsqrt(ms + EPS) * og_ref[...] * (1.0 - lam_init)
            o_ref[rows, cs] = (on * z_ref[rows, cs].astype(F32)).astype(o_ref.dtype)

    _attend_all_tiles(make_chains, finish, seq // t, t)


def _diff_attention(proj, lam_params, out_g, lam_init, bsz, seq):
    hw = A_V_DIM
    grp = A_GROUP
    gw = grp * hw
    ng = A_HEADS // grp
    slopes = jnp.asarray(np.array([2.0 ** (-8.0 * (i + 1) / A_HEADS) for i in range(A_HEADS)],
                                  dtype=np.float32))
    blocks = 5 * seq * gw * 2
    scratch = grp * seq * 3 * hw * 2
    small = lambda shape: pl.BlockSpec(shape, lambda b, h: (0, 0))
    cols = lambda first: pl.BlockSpec((seq, gw), lambda b, h: (b, first * ng + h))
    return pl.pallas_call(
        functools.partial(_diff_kernel, lam_init=lam_init, group=grp),
        grid=(bsz, ng),
        in_specs=[
            pl.BlockSpec(memory_space=pltpu.SMEM),
            cols(0), cols(1), cols(2), cols(3),
            small((4, A_QK_DIM)), small((1, hw)),
        ],
        out_specs=cols(0),
        out_shape=jax.ShapeDtypeStruct((bsz * seq, A_WIDTH), BF16),
        scratch_shapes=[pltpu.VMEM((grp, seq, 2 * hw), BF16), pltpu.VMEM((grp, hw + ONES_ROWS, seq), BF16)],
        compiler_params=pltpu.CompilerParams(
            dimension_semantics=("parallel", "parallel"),
            vmem_limit_bytes=_vmem_limit(blocks, scratch, ATT_TEMP_BYTES)),
        name="diff_attn",
    )(slopes, proj, proj, proj, proj, lam_params, out_g.reshape(1, hw))


def _conv_gated(h_ref, c_ref, hp_ref, cp_ref, b_ref, z_ref, w_ref, bias_ref, starts_sequence):
    u = c_ref[...].astype(F32) * h_ref[...].astype(F32)
    up = cp_ref[...].astype(F32) * hp_ref[...].astype(F32)
    up = jnp.where(starts_sequence, 0.0, up)
    row = lax.broadcasted_iota(jnp.int32, u.shape, 0)
    u1 = jnp.where(row >= 1, pltpu.roll(u, 1, 0), up[HALO - 1:HALO, :])
    u2 = jnp.where(row >= 2, pltpu.roll(u, 2, 0),
                   jnp.where(row == 1, up[HALO - 1:HALO, :], up[HALO - 2:HALO - 1, :]))
    w = w_ref[...]
    y = w[0:1, :] * u2 + w[1:2, :] * u1 + w[2:3, :] * u + bias_ref[...]
    return (b_ref[...].astype(F32) * y * z_ref[...].astype(F32)).astype(BF16)


def _mem_kv_kernel(mem_ref, g_ref, w_ref, kg_ref, o_ref):
    x = mem_ref[...]
    ms = jnp.mean(x * x, axis=-1, keepdims=True)
    xn = (x * lax.rsqrt(ms + EPS) * g_ref[...]).astype(BF16)
    kv = jnp.dot(xn, w_ref[...], preferred_element_type=F32)
    for hd in range(X_HEADS):
        k = kv[:, hd * X_DIM:(hd + 1) * X_DIM]
        kms = jnp.mean(k * k, axis=-1, keepdims=True)
        o_ref[:, hd * X_DIM:(hd + 1) * X_DIM] = (k * lax.rsqrt(kms + EPS) * kg_ref[...]).astype(BF16)
    o_ref[:, X_WIDTH:] = kv[:, X_WIDTH:].astype(BF16)


def _mem_kv(mem2d, g, w, k_g, bsz):
    d = mem2d.shape[1]
    blocks = MEM_LEN * d * 4 + w.size * 2 + MEM_LEN * 2 * X_WIDTH * 2
    return pl.pallas_call(
        _mem_kv_kernel,
        grid=(bsz,),
        in_specs=[pl.BlockSpec((MEM_LEN, d), lambda b: (b, 0)),
                  pl.BlockSpec((1, d), lambda b: (0, 0)),
                  pl.BlockSpec(w.shape, lambda b: (0, 0)),
                  pl.BlockSpec((1, X_DIM), lambda b: (0, 0))],
        out_specs=pl.BlockSpec((MEM_LEN, 2 * X_WIDTH), lambda b: (b, 0)),
        out_shape=jax.ShapeDtypeStruct((bsz * MEM_LEN, 2 * X_WIDTH), BF16),
        compiler_params=pltpu.CompilerParams(
            dimension_semantics=("parallel",),
            vmem_limit_bytes=_vmem_limit(blocks, 0, 4 * MEM_LEN * d * 4)),
        name="mem_kv",
    )(mem2d, g.reshape(1, d), w, k_g.reshape(1, X_DIM))


def _xattn_scores(q_ref, kv_ref):
    return [_nt_dot(q_ref[:, hd * X_DIM:(hd + 1) * X_DIM], kv_ref[:, hd * X_DIM:(hd + 1) * X_DIM])
            for hd in range(X_HEADS)]


def _xattn_gated(scores, z_ref, kv_ref):
    heads = []
    for hd, s in enumerate(scores):
        cs = slice(hd * X_DIM, (hd + 1) * X_DIM)
        m = jnp.max(s, axis=-1, keepdims=True)
        p = jnp.exp(s - m)
        l = jnp.sum(p, axis=-1, keepdims=True)
        vs = slice(X_WIDTH + hd * X_DIM, X_WIDTH + (hd + 1) * X_DIM)
        o = jnp.dot(p.astype(BF16), kv_ref[:, vs], preferred_element_type=F32) / l
        heads.append((o * z_ref[:, cs].astype(F32)).astype(BF16))
    return jnp.concatenate(heads, axis=1)


def _forget_cum_kernel(f_ref, b_ref, cum_ref):
    x = f_ref[...] + b_ref[...]
    c = jnp.minimum(x, 0.0) - jnp.log1p(jnp.exp(-jnp.abs(x)))
    n = c.shape[0]
    row = lax.broadcasted_iota(jnp.int32, c.shape, 0)
    shift = 1
    while shift < n:
        c = c + jnp.where(row >= shift, pltpu.roll(c, shift, 0), 0.0)
        shift *= 2
    cum_ref[...] = c.T[:cum_ref.shape[0], :]


def _forget_cum(f_logits, f_bias, bsz, seq):
    bias = jnp.zeros((1, LANES), F32).at[0, :C_HEADS].set(f_bias)
    blocks = seq * LANES * 4 + CUM_ROWS * seq * 4
    return pl.pallas_call(
        _forget_cum_kernel,
        grid=(bsz,),
        in_specs=[pl.BlockSpec((seq, LANES), lambda b: (b, 0)),
                  pl.BlockSpec((1, LANES), lambda b: (0, 0))],
        out_specs=pl.BlockSpec((CUM_ROWS, seq), lambda b: (b, 0)),
        out_shape=jax.ShapeDtypeStruct((bsz * CUM_ROWS, seq), F32),
        compiler_params=pltpu.CompilerParams(
            dimension_semantics=("parallel",),
            vmem_limit_bytes=_vmem_limit(blocks, 0, 6 * seq * LANES * 4)),
        name="forget_cum",
    )(f_logits, bias)


def _fox_kernel(q_ref, k_ref, v_ref, z_ref, cum_ref, o_ref, ka_ref, vt_ref, *, group):
    t = ATT_T
    hw = C_DIM
    seq = k_ref.shape[0]

    for g in range(group):
        cs = slice(g * hw, (g + 1) * hw)
        ka_ref[g, :, :hw] = k_ref[:, cs]
        ka_ref[g, :, hw:] = _bias_lanes(-LOG2E * cum_ref[g], True)
        vt_ref[g, :hw] = v_ref[:, cs].astype(F32).T.astype(BF16)
        vt_ref[g, hw:] = jnp.ones((ONES_ROWS, seq), BF16)

    def make_chains(c):
        rows = slice(c * t, (c + 1) * t)
        chains = []
        for g in range(group):
            qb = _bias_lanes(LOG2E * cum_ref[g, :, rows], False)
            qa = jnp.concatenate([q_ref[rows, g * hw:(g + 1) * hw], qb], axis=1)
            chains.append((qa, ka_ref.at[g], vt_ref.at[g]))
        return chains

    def finish(c, outs):
        rows = slice(c * t, (c + 1) * t)
        for g, (acc, l) in enumerate(outs):
            cs = slice(g * hw, (g + 1) * hw)
            o_ref[rows, cs] = ((acc / l).T * z_ref[rows, cs].astype(F32)).astype(o_ref.dtype)

    _attend_all_tiles(make_chains, finish, seq // t, t)


def _fox_attention(proj, cum, bsz, seq):
    hw = C_DIM
    grp = C_GROUP
    gw = grp * hw
    ng = C_HEADS // grp
    blocks = 5 * seq * gw * 2 + grp * 8 * seq * 4
    scratch = grp * seq * 3 * hw * 2
    cum3 = cum.reshape(bsz * CUM_ROWS, 1, seq)
    cols = lambda first: pl.BlockSpec((seq, gw), lambda b, h: (b, first * ng + h))
    return pl.pallas_call(
        functools.partial(_fox_kernel, group=grp),
        grid=(bsz, ng),
        in_specs=[
            cols(0), cols(1), cols(2), cols(3),
            pl.BlockSpec((grp, 1, seq), lambda b, h: (b * (CUM_ROWS // grp) + h, 0, 0)),
        ],
        out_specs=cols(0),
        out_shape=jax.ShapeDtypeStruct((bsz * seq, C_WIDTH), BF16),
        scratch_shapes=[pltpu.VMEM((grp, seq, 2 * hw), BF16), pltpu.VMEM((grp, hw + ONES_ROWS, seq), BF16)],
        compiler_params=pltpu.CompilerParams(
            dimension_semantics=("parallel", "parallel"),
            vmem_limit_bytes=_vmem_limit(blocks, scratch, ATT_TEMP_BYTES)),
        name="fox_attn",
    )(proj, proj, proj, proj, cum3)


def _mix_out_kernel(*refs, has_conv, tiles_per_seq):
    refs = list(refs)
    attn_ref = refs.pop(0)
    conv_refs = [refs.pop(0) for _ in range(8)] if has_conv else None
    xq_ref, xz_ref, kv_ref, w_ref, h_ref, o_ref = refs

    scores = _xattn_scores(xq_ref, kv_ref)
    width = attn_ref.shape[1]
    acc = h_ref[...] + jnp.dot(attn_ref[...], w_ref[:width, :], preferred_element_type=F32)
    if has_conv:
        starts_sequence = pl.program_id(0) % tiles_per_seq == 0
        yb = _conv_gated(*conv_refs, starts_sequence)
        acc = acc + jnp.dot(yb, w_ref[width:width + B_WIDTH, :], preferred_element_type=F32)
        width += B_WIDTH
    yx = _xattn_gated(scores, xz_ref, kv_ref)
    o_ref[...] = acc + jnp.dot(yx, w_ref[width:, :], preferred_element_type=F32)


def _mix_out(y_attn, proj, kv, w_out, h, seq, x_col_block, conv=None):
    t, d = h.shape
    tm = OUT_TM
    tiles_per_seq = seq // tm
    aw = y_attn.shape[1]
    rows = lambda width, col: pl.BlockSpec((tm, width), lambda i: (i, col))
    in_specs = [rows(aw, 0)]
    args = [y_attn]
    blocks = tm * aw * 2 + 2 * tm * X_WIDTH * 2 + MEM_LEN * 2 * X_WIDTH * 2 + w_out.size * 2 + 2 * tm * d * 4
    if conv is not None:
        cw, cb, base = conv
        prev = lambda col: pl.BlockSpec(
            (HALO, B_WIDTH), lambda i: (jnp.maximum(i * (tm // HALO) - 1, 0), col))
        in_specs += [rows(B_WIDTH, base), rows(B_WIDTH, base + 1), prev(base), prev(base + 1),
                     rows(B_WIDTH, base + 2), rows(B_WIDTH, base + 3),
                     pl.BlockSpec((CONV_W, B_WIDTH), lambda i: (0, 0)),
                     pl.BlockSpec((1, B_WIDTH), lambda i: (0, 0))]
        args += [proj] * 6 + [cw, cb.reshape(1, B_WIDTH)]
        blocks += 4 * tm * B_WIDTH * 2
    in_specs += [rows(X_WIDTH, x_col_block), rows(X_WIDTH, x_col_block + 1),
                 pl.BlockSpec((MEM_LEN, 2 * X_WIDTH), lambda i: (i // tiles_per_seq, 0)),
                 pl.BlockSpec(w_out.shape, lambda i: (0, 0)),
                 pl.BlockSpec((tm, d), lambda i: (i, 0))]
    args += [proj, proj, kv, w_out, h]
    return pl.pallas_call(
        functools.partial(_mix_out_kernel, has_conv=conv is not None, tiles_per_seq=tiles_per_seq),
        grid=(t // tm,),
        in_specs=in_specs,
        out_specs=pl.BlockSpec((tm, d), lambda i: (i, 0)),
        out_shape=jax.ShapeDtypeStruct((t, d), F32),
        compiler_params=pltpu.CompilerParams(
            dimension_semantics=("parallel",),
            vmem_limit_bytes=_vmem_limit(blocks, 0, 3 * tm * d * 4 + 8 * tm * B_WIDTH * 4)),
        name="mix_out",
    )(*args)


def _even_layer(h, mem2d, layer, bsz, seq, norm_g, w_in, w_out, a_qn, a_kn, a_lam, a_on,
                b_cw, b_cb, x_qn, x_kn, mem_g, w_mem_kv):
    lam_init = 0.8 - 0.6 * math.exp(-0.3 * layer)
    conv0 = 4 * A_WIDTH
    x0 = conv0 + 4 * B_WIDTH
    chunks = [(0, A_WIDTH, A_QK_DIM), (A_WIDTH, A_WIDTH, A_QK_DIM), (x0, X_WIDTH, X_DIM),
              (3 * A_WIDTH, A_WIDTH, "silu"), (conv0 + 3 * B_WIDTH, B_WIDTH, "silu"),
              (x0 + X_WIDTH, X_WIDTH, "silu"), (2 * A_WIDTH, A_WIDTH, "plain"), (conv0, 3 * B_WIDTH, "plain")]
    chunks = [(0, c0, c0, width, kind) for c0, width, kind in chunks]
    gains = jnp.ones((1, w_in.shape[1]), F32)
    gains = gains.at[0, :A_WIDTH].set(jnp.tile(a_qn, 2 * A_HEADS) * (LOG2E * A_QK_DIM ** -0.5))
    gains = gains.at[0, A_WIDTH:2 * A_WIDTH].set(jnp.tile(a_kn, 2 * A_HEADS))
    gains = gains.at[0, x0:x0 + X_WIDTH].set(jnp.tile(x_qn, X_HEADS) * (X_DIM ** -0.5))
    proj = _norm_proj(h, norm_g, [w_in.astype(BF16)], chunks, gains)
    ya = _diff_attention(proj, a_lam, a_on, lam_init, bsz, seq)
    kv = _mem_kv(mem2d, mem_g, w_mem_kv.astype(BF16), x_kn, bsz)
    return _mix_out(ya, proj, kv, w_out.astype(BF16), h, seq, x_col_block=x0 // X_WIDTH,
                    conv=(b_cw, b_cb, conv0 // B_WIDTH))


def _odd_layer(h, mem2d, bsz, seq, norm_g, w_in, w_out, c_qn, c_kn, c_fb, x_qn, x_kn,
               mem_g, w_mem_kv):
    main = 4 * C_WIDTH
    w_t = w_in.T.astype(BF16)
    w_x = w_t[main + C_HEADS:]
    w_f = jnp.pad(w_t[main:main + C_HEADS], ((0, LANES - C_HEADS), (0, 0)))
    chunks = [(0, 0, 0, C_WIDTH, C_DIM), (0, C_WIDTH, C_WIDTH, C_WIDTH, C_DIM),
              (1, 0, main, X_WIDTH, X_DIM), (0, 3 * C_WIDTH, 3 * C_WIDTH, C_WIDTH, "silu"),
              (1, X_WIDTH, main + X_WIDTH, X_WIDTH, "silu"), (0, 2 * C_WIDTH, 2 * C_WIDTH, C_WIDTH, "plain")]
    gains = jnp.ones((1, main + 2 * X_WIDTH), F32)
    gains = gains.at[0, :C_WIDTH].set(jnp.tile(c_qn, C_HEADS) * (LOG2E * C_DIM ** -0.5))
    gains = gains.at[0, C_WIDTH:2 * C_WIDTH].set(jnp.tile(c_kn, C_HEADS))
    gains = gains.at[0, main:main + X_WIDTH].set(jnp.tile(x_qn, X_HEADS) * (X_DIM ** -0.5))
    proj, f_logits = _norm_proj(h, norm_g, [w_t, w_x], chunks, gains, w_f, w_rows_are_outputs=True)
    cum = _forget_cum(f_logits, c_fb, bsz, seq)
    yc = _fox_attention(proj, cum, bsz, seq)
    kv = _mem_kv(mem2d, mem_g, w_mem_kv.astype(BF16), x_kn, bsz)
    return _mix_out(yc, proj, kv, w_out.astype(BF16), h, seq, x_col_block=main // X_WIDTH)


def kernel(x, mem, e_norm_g, e_w_in, e_w_out, e_a_q_norm_g, e_a_k_norm_g, e_a_lambda,
           e_a_out_norm_g, e_b_conv_w, e_b_conv_b, e_x_q_norm_g, e_x_k_norm_g, e_mem_norm_g,
           e_w_mem_kv, o_norm_g, o_w_in, o_w_out, o_c_q_norm_g, o_c_k_norm_g, o_c_forget_b,
           o_x_q_norm_g, o_x_k_norm_g, o_mem_norm_g, o_w_mem_kv):
    bsz, seq, d = x.shape
    h = x.reshape(bsz * seq, d)
    mem2d = mem.reshape(bsz * MEM_LEN, d)
    depth = e_w_in.shape[0] + o_w_in.shape[0]
    for layer in range(depth):
        i = layer // 2
        if layer % 2 == 0:
            h = _even_layer(h, mem2d, layer, bsz, seq, e_norm_g[i], e_w_in[i], e_w_out[i],
                            e_a_q_norm_g[i], e_a_k_norm_g[i], e_a_lambda[i], e_a_out_norm_g[i],
                            e_b_conv_w[i], e_b_conv_b[i], e_x_q_norm_g[i], e_x_k_norm_g[i],
                            e_mem_norm_g[i], e_w_mem_kv[i])
        else:
            h = _odd_layer(h, mem2d, bsz, seq, o_norm_g[i], o_w_in[i], o_w_out[i],
                           o_c_q_norm_g[i], o_c_k_norm_g[i], o_c_forget_b[i],
                           o_x_q_norm_g[i], o_x_k_norm_g[i], o_mem_norm_g[i], o_w_mem_kv[i])
    return h.reshape(bsz, seq, d)
```

```python
import functools
import math

import numpy as np
import jax
import jax.numpy as jnp
from jax import lax
from jax.experimental import pallas as pl
from jax.experimental.pallas import tpu as pltpu

F32 = jnp.float32
BF16 = jnp.bfloat16

D_MODEL = 1024
MEM_LEN = 256
EPS = 1e-6
NEG = -1e30
LOG2E = math.log2(math.e)

A_HEADS = 8
A_QK_DIM = 64
A_V_DIM = 128
A_WIDTH = 1024
B_WIDTH = 512
CONV_W = 3
C_HEADS = 12
C_DIM = 128
C_WIDTH = 1536
X_HEADS = 4
X_DIM = 128
X_WIDTH = 512
MIX = 2048

LANES = 128
ONES_ROWS = 16
CUM_ROWS = 16
VMEM_CAP = 56 * 1024 * 1024

PROJ_TM = 512
OUT_TM = 512
HALO = 8
ATT_T = 256
A_GROUP = 4
C_GROUP = 4
ATT_TEMP_BYTES = 26 << 20


def _vmem_limit(block_bytes, scratch_bytes=0, temp_bytes=0):
    need = 2 * block_bytes + scratch_bytes + temp_bytes + (4 << 20)
    return int(min(max(need, 16 << 20), VMEM_CAP))


def _silu(z):
    return z * jax.nn.sigmoid(z)


def _nt_dot(a, b):
    return lax.dot_general(a, b, (((1,), (1,)), ((), ())), preferred_element_type=F32)


def _group_rms_inv(x, group):
    x2 = x * x
    if group == LANES:
        return lax.rsqrt(jnp.mean(x2, axis=-1, keepdims=True) + EPS)
    assert 2 * group == LANES
    first = lax.broadcasted_iota(jnp.int32, x.shape, 1) < group
    ms_a = jnp.sum(jnp.where(first, x2, 0.0), axis=-1, keepdims=True) * (1.0 / group)
    ms_b = jnp.sum(jnp.where(first, 0.0, x2), axis=-1, keepdims=True) * (1.0 / group)
    return lax.rsqrt(jnp.where(first, ms_a, ms_b) + EPS)


def _proj_kernel(*refs, n_weights, has_side, w_rows_are_outputs, chunks):
    x_ref, g_ref, gain_ref = refs[:3]
    w_refs = refs[3:3 + n_weights]
    if has_side:
        ws_ref, o_ref, side_ref = refs[3 + n_weights:]
    else:
        (o_ref,) = refs[3 + n_weights:]
    matmul = _nt_dot if w_rows_are_outputs else functools.partial(jnp.dot, preferred_element_type=F32)

    x = x_ref[...]
    ms = jnp.mean(x * x, axis=-1, keepdims=True)
    xn = (x * lax.rsqrt(ms + EPS) * g_ref[...]).astype(BF16)
    if has_side:
        side_ref[...] = matmul(xn, ws_ref[...])
    for src, w0, c0, width, kind in chunks:
        cols = slice(c0, c0 + width)
        w = w_refs[src][w0:w0 + width, :] if w_rows_are_outputs else w_refs[src][:, w0:w0 + width]
        y = matmul(xn, w)
        if kind == "silu":
            o_ref[:, cols] = _silu(y).astype(o_ref.dtype)
        elif kind == "plain":
            o_ref[:, cols] = y.astype(o_ref.dtype)
        else:
            for h0 in range(0, width, LANES):
                yh = y[:, h0:h0 + LANES]
                gain = gain_ref[:, c0 + h0:c0 + h0 + LANES]
                o_ref[:, c0 + h0:c0 + h0 + LANES] = (yh * _group_rms_inv(yh, kind) * gain).astype(o_ref.dtype)


def _norm_proj(h, g, weights, chunks, gains, w_side=None, w_rows_are_outputs=False):
    t, d = h.shape
    n = gains.shape[1]
    assert sum(width for _, _, _, width, _ in chunks) == n
    tm = PROJ_TM
    resident = lambda shape: pl.BlockSpec(shape, lambda i: (0, 0), pipeline_mode=pl.Buffered(1))
    in_specs = [
        pl.BlockSpec((tm, d), lambda i: (i, 0)),
        pl.BlockSpec((1, d), lambda i: (0, 0)),
        pl.BlockSpec((1, n), lambda i: (0, 0)),
    ] + [resident(w.shape) for w in weights]
    out_specs = pl.BlockSpec((tm, n), lambda i: (i, 0))
    out_shape = jax.ShapeDtypeStruct((t, n), BF16)
    args = [h, g.reshape(1, d), gains, *weights]
    blocks = tm * d * 4 + tm * n * 2 + 8 * n * 4
    single = sum(w.size * 2 for w in weights)
    if w_side is not None:
        ns = w_side.shape[0] if w_rows_are_outputs else w_side.shape[1]
        in_specs.append(resident(w_side.shape))
        out_specs = [out_specs, pl.BlockSpec((tm, ns), lambda i: (i, 0))]
        out_shape = [out_shape, jax.ShapeDtypeStruct((t, ns), F32)]
        args.append(w_side)
        blocks += tm * ns * 4
        single += w_side.size * 2
    return pl.pallas_call(
        functools.partial(_proj_kernel, n_weights=len(weights), has_side=w_side is not None,
                          w_rows_are_outputs=w_rows_are_outputs, chunks=tuple(chunks)),
        grid=(t // tm,),
        in_specs=in_specs,
        out_specs=out_specs,
        out_shape=out_shape,
        compiler_params=pltpu.CompilerParams(
            dimension_semantics=("parallel",),
            vmem_limit_bytes=_vmem_limit(
                blocks, single, 3 * tm * max(c[3] for c in chunks) * 4 + 2 * tm * d * 4)),
        name="norm_proj",
    )(*args)


def _split3(c):
    c1 = c.astype(BF16).astype(F32)
    r = c - c1
    c2 = r.astype(BF16).astype(F32)
    c3 = (r - c2).astype(BF16).astype(F32)
    return c1, c2, c3


def _bias_lanes(term, key_side):
    n = term.shape[1]
    parts = jnp.concatenate(_split3(term), axis=0)
    ones = jnp.ones((3, n), F32)
    pad = jnp.zeros((10, n), F32)
    rows = jnp.concatenate([parts, ones, pad] if key_side else [ones, parts, pad], axis=0)
    sel = (lax.broadcasted_iota(jnp.int32, (16, LANES), 0)
           == lax.broadcasted_iota(jnp.int32, (16, LANES), 1)).astype(BF16)
    lanes = lax.dot_general(rows.astype(BF16), sel, (((0,), (0,)), ((), ())),
                            preferred_element_type=F32)
    return lanes.astype(BF16)


def _score_phase(chains, c, t):
    krow = lax.broadcasted_iota(jnp.int32, (t, t), 0)
    qcol = lax.broadcasted_iota(jnp.int32, (t, t), 1)
    lo, hi = c * t, (c + 1) * t
    scores = []
    for qa, ka_ref, _ in chains:
        s_d = jnp.where(krow <= qcol, _nt_dot(ka_ref[lo:hi, :], qa), NEG)
        m = jnp.max(s_d, axis=0, keepdims=True)
        s_f = None
        if c > 0:
            s_f = _nt_dot(ka_ref[0:lo, :], qa)
            m = jnp.maximum(m, jnp.max(s_f, axis=0, keepdims=True))
        scores.append((s_d, s_f, m))
    return scores


def _value_phase(chains, scores, c, t):
    lo, hi = c * t, (c + 1) * t
    outs = []
    for (_, _, vt_ref), (s_d, s_f, m) in zip(chains, scores):
        dv = vt_ref.shape[0] - ONES_ROWS
        p_d = jnp.exp2(s_d - m).astype(BF16)
        acc = jnp.dot(vt_ref[:, lo:hi], p_d, preferred_element_type=F32)
        if c > 0:
            p_f = jnp.exp2(s_f - m).astype(BF16)
            acc = acc + jnp.dot(vt_ref[:, 0:lo], p_f, preferred_element_type=F32)
        outs.append((acc[:dv], acc[dv:dv + 1]))
    return outs


def _attend_all_tiles(make_chains, finish, n_tiles, t):
    order = list(range(1, n_tiles, 2)) + list(range(n_tiles - 1 - (n_tiles % 2 == 0), -1, -2))
    assert sorted(order) == list(range(n_tiles))
    chains = make_chains(order[0])
    scores = _score_phase(chains, order[0], t)
    for pos, c in enumerate(order):
        if pos + 1 < n_tiles:
            nxt = order[pos + 1]
            next_chains = make_chains(nxt)
            next_scores = _score_phase(next_chains, nxt, t)
        finish(c, _value_phase(chains, scores, c, t))
        if pos + 1 < n_tiles:
            chains, scores = next_chains, next_scores


def _diff_kernel(slopes_ref, q_ref, k_ref, v_ref, z_ref, lam_ref, og_ref,
                 o_ref, ka_ref, vt_ref, *, lam_init, group):
    t = ATT_T
    hw = A_V_DIM
    seq = k_ref.shape[0]
    hg = pl.program_id(1)

    k_pos = lax.broadcasted_iota(jnp.int32, (1, seq), 1).astype(F32)
    for g in range(group):
        cs = slice(g * hw, (g + 1) * hw)
        ka_ref[g, :, :hw] = k_ref[:, cs]
        ka_ref[g, :, hw:] = _bias_lanes(LOG2E * slopes_ref[hg * group + g] * k_pos, True)
        vt_ref[g, :hw] = v_ref[:, cs].astype(F32).T.astype(BF16)
        vt_ref[g, hw:] = jnp.ones((ONES_ROWS, seq), BF16)

    lo = lax.broadcasted_iota(jnp.int32, (t, hw), 1) < A_QK_DIM
    lp = lam_ref[...]
    lam = (jnp.exp(jnp.sum(lp[0:1, :] * lp[1:2, :], axis=-1, keepdims=True))
           - jnp.exp(jnp.sum(lp[2:3, :] * lp[3:4, :], axis=-1, keepdims=True)) + lam_init)

    def make_chains(c):
        rows = slice(c * t, (c + 1) * t)
        q_pos = (c * t + lax.broadcasted_iota(jnp.int32, (1, t), 1)).astype(F32)
        chains = []
        for g in range(group):
            qn = q_ref[rows, g * hw:(g + 1) * hw]
            qb = _bias_lanes(-LOG2E * slopes_ref[hg * group + g] * q_pos, False)
            zero = jnp.zeros_like(qn)
            q1 = jnp.concatenate([jnp.where(lo, qn, zero), qb], axis=1)
            q2 = jnp.concatenate([jnp.where(lo, zero, qn), qb], axis=1)
            chains += [(q1, ka_ref.at[g], vt_ref.at[g]), (q2, ka_ref.at[g], vt_ref.at[g])]
        return chains

    def finish(c, outs):
        rows = slice(c * t, (c + 1) * t)
        for g in range(group):
            cs = slice(g * hw, (g + 1) * hw)
            (a1, l1), (a2, l2) = outs[2 * g], outs[2 * g + 1]
            o = (a1 / l1 - lam * (a2 / l2)).T
            ms = jnp.mean(o * o, axis=-1, keepdims=True)
            on = o * lax.rsqrt(ms + EPS) * og_ref[...] * (1.0 - lam_init)
            o_ref[rows, cs] = (on * z_ref[rows, cs].astype(F32)).astype(o_ref.dtype)

    _attend_all_tiles(make_chains, finish, seq // t, t)


def _diff_attention(proj, lam_params, out_g, lam_init, bsz, seq):
    hw = A_V_DIM
    grp = A_GROUP
    gw = grp * hw
    ng = A_HEADS // grp
    slopes = jnp.asarray(np.array([2.0 ** (-8.0 * (i + 1) / A_HEADS) for i in range(A_HEADS)],
                                  dtype=np.float32))
    blocks = 5 * seq * gw * 2
    scratch = grp * seq * 3 * hw * 2
    small = lambda shape: pl.BlockSpec(shape, lambda b, h: (0, 0))
    cols = lambda first: pl.BlockSpec((seq, gw), lambda b, h: (b, first * ng + h))
    return pl.pallas_call(
        functools.partial(_diff_kernel, lam_init=lam_init, group=grp),
        grid=(bsz, ng),
        in_specs=[
            pl.BlockSpec(memory_space=pltpu.SMEM),
            cols(0), cols(1), cols(2), cols(3),
            small((4, A_QK_DIM)), small((1, hw)),
        ],
        out_specs=cols(0),
        out_shape=jax.ShapeDtypeStruct((bsz * seq, A_WIDTH), BF16),
        scratch_shapes=[pltpu.VMEM((grp, seq, 2 * hw), BF16), pltpu.VMEM((grp, hw + ONES_ROWS, seq), BF16)],
        compiler_params=pltpu.CompilerParams(
            dimension_semantics=("parallel", "parallel"),
            vmem_limit_bytes=_vmem_limit(blocks, scratch, ATT_TEMP_BYTES)),
        name="diff_attn",
    )(slopes, proj, proj, proj, proj, lam_params, out_g.reshape(1, hw))


def _conv_gated(h_ref, c_ref, hp_ref, cp_ref, b_ref, z_ref, w_ref, bias_ref, starts_sequence):
    u = c_ref[...].astype(F32) * h_ref[...].astype(F32)
    up = cp_ref[...].astype(F32) * hp_ref[...].astype(F32)
    up = jnp.where(starts_sequence, 0.0, up)
    row = lax.broadcasted_iota(jnp.int32, u.shape, 0)
    u1 = jnp.where(row >= 1, pltpu.roll(u, 1, 0), up[HALO - 1:HALO, :])
    u2 = jnp.where(row >= 2, pltpu.roll(u, 2, 0),
                   jnp.where(row == 1, up[HALO - 1:HALO, :], up[HALO - 2:HALO - 1, :]))
    w = w_ref[...]
    y = w[0:1, :] * u2 + w[1:2, :] * u1 + w[2:3, :] * u + bias_ref[...]
    return (b_ref[...].astype(F32) * y * z_ref[...].astype(F32)).astype(BF16)


def _mem_kv_kernel(mem_ref, g_ref, w_ref, kg_ref, o_ref):
    x = mem_ref[...]
    ms = jnp.mean(x * x, axis=-1, keepdims=True)
    xn = (x * lax.rsqrt(ms + EPS) * g_ref[...]).astype(BF16)
    kv = jnp.dot(xn, w_ref[...], preferred_element_type=F32)
    for hd in range(X_HEADS):
        k = kv[:, hd * X_DIM:(hd + 1) * X_DIM]
        kms = jnp.mean(k * k, axis=-1, keepdims=True)
        o_ref[:, hd * X_DIM:(hd + 1) * X_DIM] = (k * lax.rsqrt(kms + EPS) * kg_ref[...]).astype(BF16)
    o_ref[:, X_WIDTH:] = kv[:, X_WIDTH:].astype(BF16)


def _mem_kv(mem2d, g, w, k_g, bsz):
    d = mem2d.shape[1]
    blocks = MEM_LEN * d * 4 + w.size * 2 + MEM_LEN * 2 * X_WIDTH * 2
    return pl.pallas_call(
        _mem_kv_kernel,
        grid=(bsz,),
        in_specs=[pl.BlockSpec((MEM_LEN, d), lambda b: (b, 0)),
                  pl.BlockSpec((1, d), lambda b: (0, 0)),
                  pl.BlockSpec(w.shape, lambda b: (0, 0)),
                  pl.BlockSpec((1, X_DIM), lambda b: (0, 0))],
        out_specs=pl.BlockSpec((MEM_LEN, 2 * X_WIDTH), lambda b: (b, 0)),
        out_shape=jax.ShapeDtypeStruct((bsz * MEM_LEN, 2 * X_WIDTH), BF16),
        compiler_params=pltpu.CompilerParams(
            dimension_semantics=("parallel",),
            vmem_limit_bytes=_vmem_limit(blocks, 0, 4 * MEM_LEN * d * 4)),
        name="mem_kv",
    )(mem2d, g.reshape(1, d), w, k_g.reshape(1, X_DIM))


def _xattn_scores(q_ref, kv_ref):
    return [_nt_dot(q_ref[:, hd * X_DIM:(hd + 1) * X_DIM], kv_ref[:, hd * X_DIM:(hd + 1) * X_DIM])
            for hd in range(X_HEADS)]


def _xattn_gated(scores, z_ref, kv_ref):
    heads = []
    for hd, s in enumerate(scores):
        cs = slice(hd * X_DIM, (hd + 1) * X_DIM)
        m = jnp.max(s, axis=-1, keepdims=True)
        p = jnp.exp(s - m)
        l = jnp.sum(p, axis=-1, keepdims=True)
        vs = slice(X_WIDTH + hd * X_DIM, X_WIDTH + (hd + 1) * X_DIM)
        o = jnp.dot(p.astype(BF16), kv_ref[:, vs], preferred_element_type=F32) / l
        heads.append((o * z_ref[:, cs].astype(F32)).astype(BF16))
    return jnp.concatenate(heads, axis=1)


def _forget_cum_kernel(f_ref, b_ref, cum_ref):
    x = f_ref[...] + b_ref[...]
    c = jnp.minimum(x, 0.0) - jnp.log1p(jnp.exp(-jnp.abs(x)))
    n = c.shape[0]
    row = lax.broadcasted_iota(jnp.int32, c.shape, 0)
    shift = 1
    while shift < n:
        c = c + jnp.where(row >= shift, pltpu.roll(c, shift, 0), 0.0)
        shift *= 2
    cum_ref[...] = c.T[:cum_ref.shape[0], :]


def _forget_cum(f_logits, f_bias, bsz, seq):
    bias = jnp.zeros((1, LANES), F32).at[0, :C_HEADS].set(f_bias)
    blocks = seq * LANES * 4 + CUM_ROWS * seq * 4
    return pl.pallas_call(
        _forget_cum_kernel,
        grid=(bsz,),
        in_specs=[pl.BlockSpec((seq, LANES), lambda b: (b, 0)),
                  pl.BlockSpec((1, LANES), lambda b: (0, 0))],
        out_specs=pl.BlockSpec((CUM_ROWS, seq), lambda b: (b, 0)),
        out_shape=jax.ShapeDtypeStruct((bsz * CUM_ROWS, seq), F32),
        compiler_params=pltpu.CompilerParams(
            dimension_semantics=("parallel",),
            vmem_limit_bytes=_vmem_limit(blocks, 0, 6 * seq * LANES * 4)),
        name="forget_cum",
    )(f_logits, bias)


def _fox_kernel(q_ref, k_ref, v_ref, z_ref, cum_ref, o_ref, ka_ref, vt_ref, *, group):
    t = ATT_T
    hw = C_DIM
    seq = k_ref.shape[0]

    for g in range(group):
        cs = slice(g * hw, (g + 1) * hw)
        ka_ref[g, :, :hw] = k_ref[:, cs]
        ka_ref[g, :, hw:] = _bias_lanes(-LOG2E * cum_ref[g], True)
        vt_ref[g, :hw] = v_ref[:, cs].astype(F32).T.astype(BF16)
        vt_ref[g, hw:] = jnp.ones((ONES_ROWS, seq), BF16)

    def make_chains(c):
        rows = slice(c * t, (c + 1) * t)
        chains = []
        for g in range(group):
            qb = _bias_lanes(LOG2E * cum_ref[g, :, rows], False)
            qa = jnp.concatenate([q_ref[rows, g * hw:(g + 1) * hw], qb], axis=1)
            chains.append((qa, ka_ref.at[g], vt_ref.at[g]))
        return chains

    def finish(c, outs):
        rows = slice(c * t, (c + 1) * t)
        for g, (acc, l) in enumerate(outs):
            cs = slice(g * hw, (g + 1) * hw)
            o_ref[rows, cs] = ((acc / l).T * z_ref[rows, cs].astype(F32)).astype(o_ref.dtype)

    _attend_all_tiles(make_chains, finish, seq // t, t)


def _fox_attention(proj, cum, bsz, seq):
    hw = C_DIM
    grp = C_GROUP
    gw = grp * hw
    ng = C_HEADS // grp
    blocks = 5 * seq * gw * 2 + grp * 8 * seq * 4
    scratch = grp * seq * 3 * hw * 2
    cum3 = cum.reshape(bsz * CUM_ROWS, 1, seq)
    cols = lambda first: pl.BlockSpec((seq, gw), lambda b, h: (b, first * ng + h))
    return pl.pallas_call(
        functools.partial(_fox_kernel, group=grp),
        grid=(bsz, ng),
        in_specs=[
            cols(0), cols(1), cols(2), cols(3),
            pl.BlockSpec((grp, 1, seq), lambda b, h: (b * (CUM_ROWS // grp) + h, 0, 0)),
        ],
        out_specs=cols(0),
        out_shape=jax.ShapeDtypeStruct((bsz * seq, C_WIDTH), BF16),
        scratch_shapes=[pltpu.VMEM((grp, seq, 2 * hw), BF16), pltpu.VMEM((grp, hw + ONES_ROWS, seq), BF16)],
        compiler_params=pltpu.CompilerParams(
            dimension_semantics=("parallel", "parallel"),
            vmem_limit_bytes=_vmem_limit(blocks, scratch, ATT_TEMP_BYTES)),
        name="fox_attn",
    )(proj, proj, proj, proj, cum3)


def _mix_out_kernel(*refs, has_conv, tiles_per_seq):
    refs = list(refs)
    attn_ref = refs.pop(0)
    conv_refs = [refs.pop(0) for _ in range(8)] if has_conv else None
    xq_ref, xz_ref, kv_ref, w_ref, h_ref, o_ref = refs

    scores = _xattn_scores(xq_ref, kv_ref)
    width = attn_ref.shape[1]
    acc = h_ref[...] + jnp.dot(attn_ref[...], w_ref[:width, :], preferred_element_type=F32)
    if has_conv:
        starts_sequence = pl.program_id(0) % tiles_per_seq == 0
        yb = _conv_gated(*conv_refs, starts_sequence)
        acc = acc + jnp.dot(yb, w_ref[width:width + B_WIDTH, :], preferred_element_type=F32)
        width += B_WIDTH
    yx = _xattn_gated(scores, xz_ref, kv_ref)
    o_ref[...] = acc + jnp.dot(yx, w_ref[width:, :], preferred_element_type=F32)


def _mix_out(y_attn, proj, kv, w_out, h, seq, x_col_block, conv=None):
    t, d = h.shape
    tm = OUT_TM
    tiles_per_seq = seq // tm
    aw = y_attn.shape[1]
    rows = lambda width, col: pl.BlockSpec((tm, width), lambda i: (i, col))
    in_specs = [rows(aw, 0)]
    args = [y_attn]
    blocks = tm * aw * 2 + 2 * tm * X_WIDTH * 2 + MEM_LEN * 2 * X_WIDTH * 2 + w_out.size * 2 + 2 * tm * d * 4
    if conv is not None:
        cw, cb, base = conv
        prev = lambda col: pl.BlockSpec(
            (HALO, B_WIDTH), lambda i: (jnp.maximum(i * (tm // HALO) - 1, 0), col))
        in_specs += [rows(B_WIDTH, base), rows(B_WIDTH, base + 1), prev(base), prev(base + 1),
                     rows(B_WIDTH, base + 2), rows(B_WIDTH, base + 3),
                     pl.BlockSpec((CONV_W, B_WIDTH), lambda i: (0, 0)),
                     pl.BlockSpec((1, B_WIDTH), lambda i: (0, 0))]
        args += [proj] * 6 + [cw, cb.reshape(1, B_WIDTH)]
        blocks += 4 * tm * B_WIDTH * 2
    in_specs += [rows(X_WIDTH, x_col_block), rows(X_WIDTH, x_col_block + 1),
                 pl.BlockSpec((MEM_LEN, 2 * X_WIDTH), lambda i: (i // tiles_per_seq, 0)),
                 pl.BlockSpec(w_out.shape, lambda i: (0, 0)),
                 pl.BlockSpec((tm, d), lambda i: (i, 0))]
    args += [proj, proj, kv, w_out, h]
    return pl.pallas_call(
        functools.partial(_mix_out_kernel, has_conv=conv is not None, tiles_per_seq=tiles_per_seq),
        grid=(t // tm,),
        in_specs=in_specs,
        out_specs=pl.BlockSpec((tm, d), lambda i: (i, 0)),
        out_shape=jax.ShapeDtypeStruct((t, d), F32),
        compiler_params=pltpu.CompilerParams(
            dimension_semantics=("parallel",),
            vmem_limit_bytes=_vmem_limit(blocks, 0, 3 * tm * d * 4 + 8 * tm * B_WIDTH * 4)),
        name="mix_out",
    )(*args)


def _even_layer(h, mem2d, layer, bsz, seq, norm_g, w_in, w_out, a_qn, a_kn, a_lam, a_on,
                b_cw, b_cb, x_qn, x_kn, mem_g, w_mem_kv):
    lam_init = 0.8 - 0.6 * math.exp(-0.3 * layer)
    conv0 = 4 * A_WIDTH
    x0 = conv0 + 4 * B_WIDTH
    chunks = [(0, A_WIDTH, A_QK_DIM), (A_WIDTH, A_WIDTH, A_QK_DIM), (x0, X_WIDTH, X_DIM),
              (3 * A_WIDTH, A_WIDTH, "silu"), (conv0 + 3 * B_WIDTH, B_WIDTH, "silu"),
              (x0 + X_WIDTH, X_WIDTH, "silu"), (2 * A_WIDTH, A_WIDTH, "plain"), (conv0, 3 * B_WIDTH, "plain")]
    chunks = [(0, c0, c0, width, kind) for c0, width, kind in chunks]
    gains = jnp.ones((1, w_in.shape[1]), F32)
    gains = gains.at[0, :A_WIDTH].set(jnp.tile(a_qn, 2 * A_HEADS) * (LOG2E * A_QK_DIM ** -0.5))
    gains = gains.at[0, A_WIDTH:2 * A_WIDTH].set(jnp.tile(a_kn, 2 * A_HEADS))
    gains = gains.at[0, x0:x0 + X_WIDTH].set(jnp.tile(x_qn, X_HEADS) * (X_DIM ** -0.5))
    proj = _norm_proj(h, norm_g, [w_in.astype(BF16)], chunks, gains)
    ya = _diff_attention(proj, a_lam, a_on, lam_init, bsz, seq)
    kv = _mem_kv(mem2d, mem_g, w_mem_kv.astype(BF16), x_kn, bsz)
    return _mix_out(ya, proj, kv, w_out.astype(BF16), h, seq, x_col_block=x0 // X_WIDTH,
                    conv=(b_cw, b_cb, conv0 // B_WIDTH))


def _odd_layer(h, mem2d, bsz, seq, norm_g, w_in, w_out, c_qn, c_kn, c_fb, x_qn, x_kn,
               mem_g, w_mem_kv):
    main = 4 * C_WIDTH
    w_t = w_in.T.astype(BF16)
    w_x = w_t[main + C_HEADS:]
    w_f = jnp.pad(w_t[main:main + C_HEADS], ((0, LANES - C_HEADS), (0, 0)))
    chunks = [(0, 0, 0, C_WIDTH, C_DIM), (0, C_WIDTH, C_WIDTH, C_WIDTH, C_DIM),
              (1, 0, main, X_WIDTH, X_DIM), (0, 3 * C_WIDTH, 3 * C_WIDTH, C_WIDTH, "silu"),
              (1, X_WIDTH, main + X_WIDTH, X_WIDTH, "silu"), (0, 2 * C_WIDTH, 2 * C_WIDTH, C_WIDTH, "plain")]
    gains = jnp.ones((1, main + 2 * X_WIDTH), F32)
    gains = gains.at[0, :C_WIDTH].set(jnp.tile(c_qn, C_HEADS) * (LOG2E * C_DIM ** -0.5))
    gains = gains.at[0, C_WIDTH:2 * C_WIDTH].set(jnp.tile(c_kn, C_HEADS))
    gains = gains.at[0, main:main + X_WIDTH].set(jnp.tile(x_qn, X_HEADS) * (X_DIM ** -0.5))
    proj, f_logits = _norm_proj(h, norm_g, [w_t, w_x], chunks, gains, w_f, w_rows_are_outputs=True)
    cum = _forget_cum(f_logits, c_fb, bsz, seq)
    yc = _fox_attention(proj, cum, bsz, seq)
    kv = _mem_kv(mem2d, mem_g, w_mem_kv.astype(BF16), x_kn, bsz)
    return _mix_out(yc, proj, kv, w_out.astype(BF16), h, seq, x_col_block=main // X_WIDTH)


def kernel(x, mem, e_norm_g, e_w_in, e_w_out, e_a_q_norm_g, e_a_k_norm_g, e_a_lambda,
           e_a_out_norm_g, e_b_conv_w, e_b_conv_b, e_x_q_norm_g, e_x_k_norm_g, e_mem_norm_g,
           e_w_mem_kv, o_norm_g, o_w_in, o_w_out, o_c_q_norm_g, o_c_k_norm_g, o_c_forget_b,
           o_x_q_norm_g, o_x_k_norm_g, o_mem_norm_g, o_w_mem_kv):
    bsz, seq, d = x.shape
    h = x.reshape(bsz * seq, d)
    mem2d = mem.reshape(bsz * MEM_LEN, d)
    depth = e_w_in.shape[0] + o_w_in.shape[0]
    for layer in range(depth):
        i = layer // 2
        if layer % 2 == 0:
            h = _even_layer(h, mem2d, layer, bsz, seq, e_norm_g[i], e_w_in[i], e_w_out[i],
                            e_a_q_norm_g[i], e_a_k_norm_g[i], e_a_lambda[i], e_a_out_norm_g[i],
                            e_b_conv_w[i], e_b_conv_b[i], e_x_q_norm_g[i], e_x_k_norm_g[i],
                            e_mem_norm_g[i], e_w_mem_kv[i])
        else:
            h = _odd_layer(h, mem2d, bsz, seq, o_norm_g[i], o_w_in[i], o_w_out[i],
                           o_c_q_norm_g[i], o_c_k_norm_g[i], o_c_forget_b[i],
                           o_x_q_norm_g[i], o_x_k_norm_g[i], o_mem_norm_g[i], o_w_mem_kv[i])
    return h.reshape(bsz, seq, d)
```

```python
import functools
import math

import numpy as np
import jax
import jax.numpy as jnp
from jax import lax
from jax.experimental import pallas as pl
from jax.experimental.pallas import tpu as pltpu

F32 = jnp.float32
BF16 = jnp.bfloat16

D_MODEL = 1024
MEM_LEN = 256
EPS = 1e-6
NEG = -1e30
LOG2E = math.log2(math.e)

A_HEADS = 8
A_QK_DIM = 64
A_V_DIM = 128
A_WIDTH = 1024
B_WIDTH = 512
CONV_W = 3
C_HEADS = 12
C_DIM = 128
C_WIDTH = 1536
X_HEADS = 4
X_DIM = 128
X_WIDTH = 512
MIX = 2048

LANES = 128
ONES_ROWS = 16
CUM_ROWS = 16
VMEM_CAP = 56 * 1024 * 1024

PROJ_TM = 512
OUT_TM = 512
HALO = 8
ATT_T = 256
A_GROUP = 4
C_GROUP = 4
ATT_TEMP_BYTES = 26 << 20


def _vmem_limit(block_bytes, scratch_bytes=0, temp_bytes=0):
    need = 2 * block_bytes + scratch_bytes + temp_bytes + (4 << 20)
    return int(min(max(need, 16 << 20), VMEM_CAP))


def _silu(z):
    return z * jax.nn.sigmoid(z)


def _nt_dot(a, b):
    return lax.dot_general(a, b, (((1,), (1,)), ((), ())), preferred_element_type=F32)


def _group_rms_inv(x, group):
    x2 = x * x
    if group == LANES:
        return lax.rsqrt(jnp.mean(x2, axis=-1, keepdims=True) + EPS)
    assert 2 * group == LANES
    first = lax.broadcasted_iota(jnp.int32, x.shape, 1) < group
    ms_a = jnp.sum(jnp.where(first, x2, 0.0), axis=-1, keepdims=True) * (1.0 / group)
    ms_b = jnp.sum(jnp.where(first, 0.0, x2), axis=-1, keepdims=True) * (1.0 / group)
    return lax.rsqrt(jnp.where(first, ms_a, ms_b) + EPS)


def _proj_kernel(*refs, n_weights, has_side, w_rows_are_outputs, chunks):
    x_ref, g_ref, gain_ref = refs[:3]
    w_refs = refs[3:3 + n_weights]
    if has_side:
        ws_ref, o_ref, side_ref = refs[3 + n_weights:]
    else:
        (o_ref,) = refs[3 + n_weights:]
    matmul = _nt_dot if w_rows_are_outputs else functools.partial(jnp.dot, preferred_element_type=F32)

    x = x_ref[...]
    ms = jnp.mean(x * x, axis=-1, keepdims=True)
    xn = (x * lax.rsqrt(ms + EPS) * g_ref[...]).astype(BF16)
    if has_side:
        side_ref[...] = matmul(xn, ws_ref[...])
    for src, w0, c0, width, kind in chunks:
        cols = slice(c0, c0 + width)
        w = w_refs[src][w0:w0 + width, :] if w_rows_are_outputs else w_refs[src][:, w0:w0 + width]
        y = matmul(xn, w)
        if kind == "silu":
            o_ref[:, cols] = _silu(y).astype(o_ref.dtype)
        elif kind == "plain":
            o_ref[:, cols] = y.astype(o_ref.dtype)
        else:
            for h0 in range(0, width, LANES):
                yh = y[:, h0:h0 + LANES]
                gain = gain_ref[:, c0 + h0:c0 + h0 + LANES]
                o_ref[:, c0 + h0:c0 + h0 + LANES] = (yh * _group_rms_inv(yh, kind) * gain).astype(o_ref.dtype)


def _norm_proj(h, g, weights, chunks, gains, w_side=None, w_rows_are_outputs=False):
    t, d = h.shape
    n = gains.shape[1]
    assert sum(width for _, _, _, width, _ in chunks) == n
    tm = PROJ_TM
    resident = lambda shape: pl.BlockSpec(shape, lambda i: (0, 0), pipeline_mode=pl.Buffered(1))
    in_specs = [
        pl.BlockSpec((tm, d), lambda i: (i, 0)),
        pl.BlockSpec((1, d), lambda i: (0, 0)),
        pl.BlockSpec((1, n), lambda i: (0, 0)),
    ] + [resident(w.shape) for w in weights]
    out_specs = pl.BlockSpec((tm, n), lambda i: (i, 0))
    out_shape = jax.ShapeDtypeStruct((t, n), BF16)
    args = [h, g.reshape(1, d), gains, *weights]
    blocks = tm * d * 4 + tm * n * 2 + 8 * n * 4
    single = sum(w.size * 2 for w in weights)
    if w_side is not None:
        ns = w_side.shape[0] if w_rows_are_outputs else w_side.shape[1]
        in_specs.append(resident(w_side.shape))
        out_specs = [out_specs, pl.BlockSpec((tm, ns), lambda i: (i, 0))]
        out_shape = [out_shape, jax.ShapeDtypeStruct((t, ns), F32)]
        args.append(w_side)
        blocks += tm * ns * 4
        single += w_side.size * 2
    return pl.pallas_call(
        functools.partial(_proj_kernel, n_weights=len(weights), has_side=w_side is not None,
                          w_rows_are_outputs=w_rows_are_outputs, chunks=tuple(chunks)),
        grid=(t // tm,),
        in_specs=in_specs,
        out_specs=out_specs,
        out_shape=out_shape,
        compiler_params=pltpu.CompilerParams(
            dimension_semantics=("parallel",),
            vmem_limit_bytes=_vmem_limit(
                blocks, single, 3 * tm * max(c[3] for c in chunks) * 4 + 2 * tm * d * 4)),
        name="norm_proj",
    )(*args)


def _split3(c):
    c1 = c.astype(BF16).astype(F32)
    r = c - c1
    c2 = r.astype(BF16).astype(F32)
    c3 = (r - c2).astype(BF16).astype(F32)
    return c1, c2, c3


def _bias_lanes(term, key_side):
    n = term.shape[1]
    parts = jnp.concatenate(_split3(term), axis=0)
    ones = jnp.ones((3, n), F32)
    pad = jnp.zeros((10, n), F32)
    rows = jnp.concatenate([parts, ones, pad] if key_side else [ones, parts, pad], axis=0)
    sel = (lax.broadcasted_iota(jnp.int32, (16, LANES), 0)
           == lax.broadcasted_iota(jnp.int32, (16, LANES), 1)).astype(BF16)
    lanes = lax.dot_general(rows.astype(BF16), sel, (((0,), (0,)), ((), ())),
                            preferred_element_type=F32)
    return lanes.astype(BF16)


def _score_phase(chains, c, t):
    krow = lax.broadcasted_iota(jnp.int32, (t, t), 0)
    qcol = lax.broadcasted_iota(jnp.int32, (t, t), 1)
    lo, hi = c * t, (c + 1) * t
    scores = []
    for qa, ka_ref, _ in chains:
        s_d = jnp.where(krow <= qcol, _nt_dot(ka_ref[lo:hi, :], qa), NEG)
        m = jnp.max(s_d, axis=0, keepdims=True)
        s_f = None
        if c > 0:
            s_f = _nt_dot(ka_ref[0:lo, :], qa)
            m = jnp.maximum(m, jnp.max(s_f, axis=0, keepdims=True))
        scores.append((s_d, s_f, m))
    return scores


def _value_phase(chains, scores, c, t):
    lo, hi = c * t, (c + 1) * t
    outs = []
    for (_, _, vt_ref), (s_d, s_f, m) in zip(chains, scores):
        dv = vt_ref.shape[0] - ONES_ROWS
        p_d = jnp.exp2(s_d - m).astype(BF16)
        acc = jnp.dot(vt_ref[:, lo:hi], p_d, preferred_element_type=F32)
        if c > 0:
            p_f = jnp.exp2(s_f - m).astype(BF16)
            acc = acc + jnp.dot(vt_ref[:, 0:lo], p_f, preferred_element_type=F32)
        outs.append((acc[:dv], acc[dv:dv + 1]))
    return outs


def _attend_all_tiles(make_chains, finish, n_tiles, t):
    order = list(range(1, n_tiles, 2)) + list(range(n_tiles - 1 - (n_tiles % 2 == 0), -1, -2))
    assert sorted(order) == list(range(n_tiles))
    chains = make_chains(order[0])
    scores = _score_phase(chains, order[0], t)
    for pos, c in enumerate(order):
        if pos + 1 < n_tiles:
            nxt = order[pos + 1]
            next_chains = make_chains(nxt)
            next_scores = _score_phase(next_chains, nxt, t)
        finish(c, _value_phase(chains, scores, c, t))
        if pos + 1 < n_tiles:
            chains, scores = next_chains, next_scores


def _diff_kernel(slopes_ref, q_ref, k_ref, v_ref, z_ref, lam_ref, og_ref,
                 o_ref, ka_ref, vt_ref, *, lam_init, group):
    t = ATT_T
    hw = A_V_DIM
    seq = k_ref.shape[0]
    hg = pl.program_id(1)

    k_pos = lax.broadcasted_iota(jnp.int32, (1, seq), 1).astype(F32)
    for g in range(group):
        cs = slice(g * hw, (g + 1) * hw)
        ka_ref[g, :, :hw] = k_ref[:, cs]
        ka_ref[g, :, hw:] = _bias_lanes(LOG2E * slopes_ref[hg * group + g] * k_pos, True)
        vt_ref[g, :hw] = v_ref[:, cs].astype(F32).T.astype(BF16)
        vt_ref[g, hw:] = jnp.ones((ONES_ROWS, seq), BF16)

    lo = lax.broadcasted_iota(jnp.int32, (t, hw), 1) < A_QK_DIM
    lp = lam_ref[...]
    lam = (jnp.exp(jnp.sum(lp[0:1, :] * lp[1:2, :], axis=-1, keepdims=True))
           - jnp.exp(jnp.sum(lp[2:3, :] * lp[3:4, :], axis=-1, keepdims=True)) + lam_init)

    def make_chains(c):
        rows = slice(c * t, (c + 1) * t)
        q_pos = (c * t + lax.broadcasted_iota(jnp.int32, (1, t), 1)).astype(F32)
        chains = []
        for g in range(group):
            qn = q_ref[rows, g * hw:(g + 1) * hw]
            qb = _bias_lanes(-LOG2E * slopes_ref[hg * group + g] * q_pos, False)
            zero = jnp.zeros_like(qn)
            q1 = jnp.concatenate([jnp.where(lo, qn, zero), qb], axis=1)
            q2 = jnp.concatenate([jnp.where(lo, zero, qn), qb], axis=1)
            chains += [(q1, ka_ref.at[g], vt_ref.at[g]), (q2, ka_ref.at[g], vt_ref.at[g])]
        return chains

    def finish(c, outs):
        rows = slice(c * t, (c + 1) * t)
        for g in range(group):
            cs = slice(g * hw, (g + 1) * hw)
            (a1, l1), (a2, l2) = outs[2 * g], outs[2 * g + 1]
            o = (a1 / l1 - lam * (a2 / l2)).T
            ms = jnp.mean(o * o, axis=-1, keepdims=True)
            on = o * lax.rsqrt(ms + EPS) * og_ref[...] * (1.0 - lam_init)
            o_ref[rows, cs] = (on * z_ref[rows, cs].astype(F32)).astype(o_ref.dtype)

    _attend_all_tiles(make_chains, finish, seq // t, t)


def _diff_attention(proj, lam_params, out_g, lam_init, bsz, seq):
    hw = A_V_DIM
    grp = A_GROUP
    gw = grp * hw
    ng = A_HEADS // grp
    slopes = jnp.asarray(np.array([2.0 ** (-8.0 * (i + 1) / A_HEADS) for i in range(A_HEADS)],
                                  dtype=np.float32))
    blocks = 5 * seq * gw * 2
    scratch = grp * seq * 3 * hw * 2
    small = lambda shape: pl.BlockSpec(shape, lambda b, h: (0, 0))
    cols = lambda first: pl.BlockSpec((seq, gw), lambda b, h: (b, first * ng + h))
    return pl.pallas_call(
        functools.partial(_diff_kernel, lam_init=lam_init, group=grp),
        grid=(bsz, ng),
        in_specs=[
            pl.BlockSpec(memory_space=pltpu.SMEM),
            cols(0), cols(1), cols(2), cols(3),
            small((4, A_QK_DIM)), small((1, hw)),
        ],
        out_specs=cols(0),
        out_shape=jax.ShapeDtypeStruct((bsz * seq, A_WIDTH), BF16),
        scratch_shapes=[pltpu.VMEM((grp, seq, 2 * hw), BF16), pltpu.VMEM((grp, hw + ONES_ROWS, seq), BF16)],
        compiler_params=pltpu.CompilerParams(
            dimension_semantics=("parallel", "parallel"),
            vmem_limit_bytes=_vmem_limit(blocks, scratch, ATT_TEMP_BYTES)),
        name="diff_attn",
    )(slopes, proj, proj, proj, proj, lam_params, out_g.reshape(1, hw))


def _conv_gated(h_ref, c_ref, hp_ref, cp_ref, b_ref, z_ref, w_ref, bias_ref, starts_sequence):
    u = c_ref[...].astype(F32) * h_ref[...].astype(F32)
    up = cp_ref[...].astype(F32) * hp_ref[...].astype(F32)
    up = jnp.where(starts_sequence, 0.0, up)
    row = lax.broadcasted_iota(jnp.int32, u.shape, 0)
    u1 = jnp.where(row >= 1, pltpu.roll(u, 1, 0), up[HALO - 1:HALO, :])
    u2 = jnp.where(row >= 2, pltpu.roll(u, 2, 0),
                   jnp.where(row == 1, up[HALO - 1:HALO, :], up[HALO - 2:HALO - 1, :]))
    w = w_ref[...]
    y = w[0:1, :] * u2 + w[1:2, :] * u1 + w[2:3, :] * u + bias_ref[...]
    return (b_ref[...].astype(F32) * y * z_ref[...].astype(F32)).astype(BF16)


def _mem_kv_kernel(mem_ref, g_ref, w_ref, kg_ref, o_ref):
    x = mem_ref[...]
    ms = jnp.mean(x * x, axis=-1, keepdims=True)
    xn = (x * lax.rsqrt(ms + EPS) * g_ref[...]).astype(BF16)
    kv = jnp.dot(xn, w_ref[...], preferred_element_type=F32)
    for hd in range(X_HEADS):
        k = kv[:, hd * X_DIM:(hd + 1) * X_DIM]
        kms = jnp.mean(k * k, axis=-1, keepdims=True)
        o_ref[:, hd * X_DIM:(hd + 1) * X_DIM] = (k * lax.rsqrt(kms + EPS) * kg_ref[...]).astype(BF16)
    o_ref[:, X_WIDTH:] = kv[:, X_WIDTH:].astype(BF16)


def _mem_kv(mem2d, g, w, k_g, bsz):
    d = mem2d.shape[1]
    blocks = MEM_LEN * d * 4 + w.size * 2 + MEM_LEN * 2 * X_WIDTH * 2
    return pl.pallas_call(
        _mem_kv_kernel,
        grid=(bsz,),
        in_specs=[pl.BlockSpec((MEM_LEN, d), lambda b: (b, 0)),
                  pl.BlockSpec((1, d), lambda b: (0, 0)),
                  pl.BlockSpec(w.shape, lambda b: (0, 0)),
                  pl.BlockSpec((1, X_DIM), lambda b: (0, 0))],
        out_specs=pl.BlockSpec((MEM_LEN, 2 * X_WIDTH), lambda b: (b, 0)),
        out_shape=jax.ShapeDtypeStruct((bsz * MEM_LEN, 2 * X_WIDTH), BF16),
        compiler_params=pltpu.CompilerParams(
            dimension_semantics=("parallel",),
            vmem_limit_bytes=_vmem_limit(blocks, 0, 4 * MEM_LEN * d * 4)),
        name="mem_kv",
    )(mem2d, g.reshape(1, d), w, k_g.reshape(1, X_DIM))


def _xattn_scores(q_ref, kv_ref):
    return [_nt_dot(q_ref[:, hd * X_DIM:(hd + 1) * X_DIM], kv_ref[:, hd * X_DIM:(hd + 1) * X_DIM])
            for hd in range(X_HEADS)]


def _xattn_gated(scores, z_ref, kv_ref):
    heads = []
    for hd, s in enumerate(scores):
        cs = slice(hd * X_DIM, (hd + 1) * X_DIM)
        m = jnp.max(s, axis=-1, keepdims=True)
        p = jnp.exp(s - m)
        l = jnp.sum(p, axis=-1, keepdims=True)
        vs = slice(X_WIDTH + hd * X_DIM, X_WIDTH + (hd + 1) * X_DIM)
        o = jnp.dot(p.astype(BF16), kv_ref[:, vs], preferred_element_type=F32) / l
        heads.append((o * z_ref[:, cs].astype(F32)).astype(BF16))
    return jnp.concatenate(heads, axis=1)


def _forget_cum_kernel(f_ref, b_ref, cum_ref):
    x = f_ref[...] + b_ref[...]
    c = jnp.minimum(x, 0.0) - jnp.log1p(jnp.exp(-jnp.abs(x)))
    n = c.shape[0]
    row = lax.broadcasted_iota(jnp.int32, c.shape, 0)
    shift = 1
    while shift < n:
        c = c + jnp.where(row >= shift, pltpu.roll(c, shift, 0), 0.0)
        shift *= 2
    cum_ref[...] = c.T[:cum_ref.shape[0], :]


def _forget_cum(f_logits, f_bias, bsz, seq):
    bias = jnp.pad(f_bias.astype(F32), (0, LANES - C_HEADS)).reshape(1, LANES)
    blocks = seq * LANES * 4 + CUM_ROWS * seq * 4
    return pl.pallas_call(
        _forget_cum_kernel,
        grid=(bsz,),
        in_specs=[pl.BlockSpec((seq, LANES), lambda b: (b, 0)),
                  pl.BlockSpec((1, LANES), lambda b: (0, 0))],
        out_specs=pl.BlockSpec((CUM_ROWS, seq), lambda b: (b, 0)),
        out_shape=jax.ShapeDtypeStruct((bsz * CUM_ROWS, seq), F32),
        compiler_params=pltpu.CompilerParams(
            dimension_semantics=("parallel",),
            vmem_limit_bytes=_vmem_limit(blocks, 0, 6 * seq * LANES * 4)),
        name="forget_cum",
    )(f_logits, bias)


def _fox_kernel(q_ref, k_ref, v_ref, z_ref, cum_ref, o_ref, ka_ref, vt_ref, *, group):
    t = ATT_T
    hw = C_DIM
    seq = k_ref.shape[0]

    for g in range(group):
        cs = slice(g * hw, (g + 1) * hw)
        ka_ref[g, :, :hw] = k_ref[:, cs]
        ka_ref[g, :, hw:] = _bias_lanes(-LOG2E * cum_ref[g], True)
        vt_ref[g, :hw] = v_ref[:, cs].astype(F32).T.astype(BF16)
        vt_ref[g, hw:] = jnp.ones((ONES_ROWS, seq), BF16)

    def make_chains(c):
        rows = slice(c * t, (c + 1) * t)
        chains = []
        for g in range(group):
            qb = _bias_lanes(LOG2E * cum_ref[g, :, rows], False)
            qa = jnp.concatenate([q_ref[rows, g * hw:(g + 1) * hw], qb], axis=1)
            chains.append((qa, ka_ref.at[g], vt_ref.at[g]))
        return chains

    def finish(c, outs):
        rows = slice(c * t, (c + 1) * t)
        for g, (acc, l) in enumerate(outs):
            cs = slice(g * hw, (g + 1) * hw)
            o_ref[rows, cs] = ((acc / l).T * z_ref[rows, cs].astype(F32)).astype(o_ref.dtype)

    _attend_all_tiles(make_chains, finish, seq // t, t)


def _fox_attention(proj, cum, bsz, seq):
    hw = C_DIM
    grp = C_GROUP
    gw = grp * hw
    ng = C_HEADS // grp
    blocks = 5 * seq * gw * 2 + grp * 8 * seq * 4
    scratch = grp * seq * 3 * hw * 2
    cum3 = cum.reshape(bsz * CUM_ROWS, 1, seq)
    cols = lambda first: pl.BlockSpec((seq, gw), lambda b, h: (b, first * ng + h))
    return pl.pallas_call(
        functools.partial(_fox_kernel, group=grp),
        grid=(bsz, ng),
        in_specs=[
            cols(0), cols(1), cols(2), cols(3),
            pl.BlockSpec((grp, 1, seq), lambda b, h: (b * (CUM_ROWS // grp) + h, 0, 0)),
        ],
        out_specs=cols(0),
        out_shape=jax.ShapeDtypeStruct((bsz * seq, C_WIDTH), BF16),
        scratch_shapes=[pltpu.VMEM((grp, seq, 2 * hw), BF16), pltpu.VMEM((grp, hw + ONES_ROWS, seq), BF16)],
        compiler_params=pltpu.CompilerParams(
            dimension_semantics=("parallel", "parallel"),
            vmem_limit_bytes=_vmem_limit(blocks, scratch, ATT_TEMP_BYTES)),
        name="fox_attn",
    )(proj, proj, proj, proj, cum3)


def _mix_out_kernel(*refs, has_conv, tiles_per_seq):
    refs = list(refs)
    attn_ref = refs.pop(0)
    conv_refs = [refs.pop(0) for _ in range(8)] if has_conv else None
    xq_ref, xz_ref, kv_ref, w_ref, h_ref, o_ref = refs

    scores = _xattn_scores(xq_ref, kv_ref)
    width = attn_ref.shape[1]
    acc = h_ref[...] + jnp.dot(attn_ref[...], w_ref[:width, :], preferred_element_type=F32)
    if has_conv:
        starts_sequence = pl.program_id(0) % tiles_per_seq == 0
        yb = _conv_gated(*conv_refs, starts_sequence)
        acc = acc + jnp.dot(yb, w_ref[width:width + B_WIDTH, :], preferred_element_type=F32)
        width += B_WIDTH
    yx = _xattn_gated(scores, xz_ref, kv_ref)
    o_ref[...] = acc + jnp.dot(yx, w_ref[width:, :], preferred_element_type=F32)


def _mix_out(y_attn, proj, kv, w_out, h, seq, x_col_block, conv=None):
    t, d = h.shape
    tm = OUT_TM
    tiles_per_seq = seq // tm
    aw = y_attn.shape[1]
    rows = lambda width, col: pl.BlockSpec((tm, width), lambda i: (i, col))
    in_specs = [rows(aw, 0)]
    args = [y_attn]
    blocks = tm * aw * 2 + 2 * tm * X_WIDTH * 2 + MEM_LEN * 2 * X_WIDTH * 2 + w_out.size * 2 + 2 * tm * d * 4
    if conv is not None:
        cw, cb, base = conv
        prev = lambda col: pl.BlockSpec(
            (HALO, B_WIDTH), lambda i: (jnp.maximum(i * (tm // HALO) - 1, 0), col))
        in_specs += [rows(B_WIDTH, base), rows(B_WIDTH, base + 1), prev(base), prev(base + 1),
                     rows(B_WIDTH, base + 2), rows(B_WIDTH, base + 3),
                     pl.BlockSpec((CONV_W, B_WIDTH), lambda i: (0, 0)),
                     pl.BlockSpec((1, B_WIDTH), lambda i: (0, 0))]
        args += [proj] * 6 + [cw, cb.reshape(1, B_WIDTH)]
        blocks += 4 * tm * B_WIDTH * 2
    in_specs += [rows(X_WIDTH, x_col_block), rows(X_WIDTH, x_col_block + 1),
                 pl.BlockSpec((MEM_LEN, 2 * X_WIDTH), lambda i: (i // tiles_per_seq, 0)),
                 pl.BlockSpec(w_out.shape, lambda i: (0, 0)),
                 pl.BlockSpec((tm, d), lambda i: (i, 0))]
    args += [proj, proj, kv, w_out, h]
    return pl.pallas_call(
        functools.partial(_mix_out_kernel, has_conv=conv is not None, tiles_per_seq=tiles_per_seq),
        grid=(t // tm,),
        in_specs=in_specs,
        out_specs=pl.BlockSpec((tm, d), lambda i: (i, 0)),
        out_shape=jax.ShapeDtypeStruct((t, d), F32),
        compiler_params=pltpu.CompilerParams(
            dimension_semantics=("parallel",),
            vmem_limit_bytes=_vmem_limit(blocks, 0, 3 * tm * d * 4 + 8 * tm * B_WIDTH * 4)),
        name="mix_out",
    )(*args)


def _even_layer(h, mem2d, layer, bsz, seq, norm_g, w_in, w_out, a_qn, a_kn, a_lam, a_on,
                b_cw, b_cb, x_qn, x_kn, mem_g, w_mem_kv):
    lam_init = 0.8 - 0.6 * math.exp(-0.3 * layer)
    conv0 = 4 * A_WIDTH
    x0 = conv0 + 4 * B_WIDTH
    chunks = [(0, A_WIDTH, A_QK_DIM), (A_WIDTH, A_WIDTH, A_QK_DIM), (x0, X_WIDTH, X_DIM),
              (3 * A_WIDTH, A_WIDTH, "silu"), (conv0 + 3 * B_WIDTH, B_WIDTH, "silu"),
              (x0 + X_WIDTH, X_WIDTH, "silu"), (2 * A_WIDTH, A_WIDTH, "plain"), (conv0, 3 * B_WIDTH, "plain")]
    chunks = [(0, c0, c0, width, kind) for c0, width, kind in chunks]
    gains = jnp.concatenate([
        jnp.tile(a_qn, 2 * A_HEADS) * (LOG2E * A_QK_DIM ** -0.5), jnp.tile(a_kn, 2 * A_HEADS),
        jnp.ones((x0 - 2 * A_WIDTH,), F32), jnp.tile(x_qn, X_HEADS) * (X_DIM ** -0.5),
        jnp.ones((X_WIDTH,), F32)]).reshape(1, -1)
    proj = _norm_proj(h, norm_g, [w_in.astype(BF16)], chunks, gains)
    ya = _diff_attention(proj, a_lam, a_on, lam_init, bsz, seq)
    kv = _mem_kv(mem2d, mem_g, w_mem_kv.astype(BF16), x_kn, bsz)
    return _mix_out(ya, proj, kv, w_out.astype(BF16), h, seq, x_col_block=x0 // X_WIDTH,
                    conv=(b_cw, b_cb, conv0 // B_WIDTH))


def _odd_layer(h, mem2d, bsz, seq, norm_g, w_in, w_out, c_qn, c_kn, c_fb, x_qn, x_kn,
               mem_g, w_mem_kv):
    main = 4 * C_WIDTH
    w_t = w_in.T.astype(BF16)
    w_x = w_t[main + C_HEADS:]
    w_f = jnp.pad(w_t[main:main + C_HEADS], ((0, LANES - C_HEADS), (0, 0)))
    chunks = [(0, 0, 0, C_WIDTH, C_DIM), (0, C_WIDTH, C_WIDTH, C_WIDTH, C_DIM),
              (1, 0, main, X_WIDTH, X_DIM), (0, 3 * C_WIDTH, 3 * C_WIDTH, C_WIDTH, "silu"),
              (1, X_WIDTH, main + X_WIDTH, X_WIDTH, "silu"), (0, 2 * C_WIDTH, 2 * C_WIDTH, C_WIDTH, "plain")]
    gains = jnp.concatenate([
        jnp.tile(c_qn, C_HEADS) * (LOG2E * C_DIM ** -0.5), jnp.tile(c_kn, C_HEADS),
        jnp.ones((main - 2 * C_WIDTH,), F32), jnp.tile(x_qn, X_HEADS) * (X_DIM ** -0.5),
        jnp.ones((X_WIDTH,), F32)]).reshape(1, -1)
    proj, f_logits = _norm_proj(h, norm_g, [w_t, w_x], chunks, gains, w_f, w_rows_are_outputs=True)
    cum = _forget_cum(f_logits, c_fb, bsz, seq)
    yc = _fox_attention(proj, cum, bsz, seq)
    kv = _mem_kv(mem2d, mem_g, w_mem_kv.astype(BF16), x_kn, bsz)
    return _mix_out(yc, proj, kv, w_out.astype(BF16), h, seq, x_col_block=main // X_WIDTH)


def kernel(x, mem, e_norm_g, e_w_in, e_w_out, e_a_q_norm_g, e_a_k_norm_g, e_a_lambda,
           e_a_out_norm_g, e_b_conv_w, e_b_conv_b, e_x_q_norm_g, e_x_k_norm_g, e_mem_norm_g,
           e_w_mem_kv, o_norm_g, o_w_in, o_w_out, o_c_q_norm_g, o_c_k_norm_g, o_c_forget_b,
           o_x_q_norm_g, o_x_k_norm_g, o_mem_norm_g, o_w_mem_kv):
    bsz, seq, d = x.shape
    h = x.reshape(bsz * seq, d)
    mem2d = mem.reshape(bsz * MEM_LEN, d)
    depth = e_w_in.shape[0] + o_w_in.shape[0]
    for layer in range(depth):
        i = layer // 2
        if layer % 2 == 0:
            h = _even_layer(h, mem2d, layer, bsz, seq, e_norm_g[i], e_w_in[i], e_w_out[i],
                            e_a_q_norm_g[i], e_a_k_norm_g[i], e_a_lambda[i], e_a_out_norm_g[i],
                            e_b_conv_w[i], e_b_conv_b[i], e_x_q_norm_g[i], e_x_k_norm_g[i],
                            e_mem_norm_g[i], e_w_mem_kv[i])
        else:
            h = _odd_layer(h, mem2d, bsz, seq, o_norm_g[i], o_w_in[i], o_w_out[i],
                           o_c_q_norm_g[i], o_c_k_norm_g[i], o_c_forget_b[i],
                           o_x_q_norm_g[i], o_x_k_norm_g[i], o_mem_norm_g[i], o_w_mem_kv[i])
    return h.reshape(bsz, seq, d)
```

```python
import functools
import math

import numpy as np
import jax
import jax.numpy as jnp
from jax import lax
from jax.experimental import pallas as pl
from jax.experimental.pallas import tpu as pltpu

F32 = jnp.float32
BF16 = jnp.bfloat16

D_MODEL = 1024
MEM_LEN = 256
EPS = 1e-6
NEG = -1e30
LOG2E = math.log2(math.e)

A_HEADS = 8
A_QK_DIM = 64
A_V_DIM = 128
A_WIDTH = 1024
B_WIDTH = 512
CONV_W = 3
C_HEADS = 12
C_DIM = 128
C_WIDTH = 1536
X_HEADS = 4
X_DIM = 128
X_WIDTH = 512
MIX = 2048

LANES = 128
ONES_ROWS = 16
CUM_ROWS = 16
VMEM_CAP = 56 * 1024 * 1024

PROJ_TM = 512
OUT_TM = 512
HALO = 8
ATT_T = 256
A_GROUP = 4
C_GROUP = 4
ATT_TEMP_BYTES = 26 << 20


def _vmem_limit(block_bytes, scratch_bytes=0, temp_bytes=0):
    need = 2 * block_bytes + scratch_bytes + temp_bytes + (4 << 20)
    return int(min(max(need, 16 << 20), VMEM_CAP))


def _silu(z):
    half = 0.5 * z
    return half + half * jnp.tanh(half)


def _nt_dot(a, b):
    return lax.dot_general(a, b, (((1,), (1,)), ((), ())), preferred_element_type=F32)


def _group_rms_inv(x, group):
    x2 = x * x
    if group == LANES:
        return lax.rsqrt(jnp.mean(x2, axis=-1, keepdims=True) + EPS)
    assert 2 * group == LANES
    first = lax.broadcasted_iota(jnp.int32, x.shape, 1) < group
    ms_a = jnp.sum(jnp.where(first, x2, 0.0), axis=-1, keepdims=True) * (1.0 / group)
    ms_b = jnp.sum(jnp.where(first, 0.0, x2), axis=-1, keepdims=True) * (1.0 / group)
    return lax.rsqrt(jnp.where(first, ms_a, ms_b) + EPS)


def _proj_kernel(*refs, n_weights, has_side, w_rows_are_outputs, chunks):
    x_ref, g_ref, gain_ref = refs[:3]
    w_refs = refs[3:3 + n_weights]
    if has_side:
        ws_ref, o_ref, side_ref = refs[3 + n_weights:]
    else:
        (o_ref,) = refs[3 + n_weights:]
    matmul = _nt_dot if w_rows_are_outputs else functools.partial(jnp.dot, preferred_element_type=F32)

    x = x_ref[...]
    ms = jnp.mean(x * x, axis=-1, keepdims=True)
    xn = (x * lax.rsqrt(ms + EPS) * g_ref[...]).astype(BF16)
    if has_side:
        side_ref[...] = matmul(xn, ws_ref[...])
    for src, w0, c0, width, kind in chunks:
        cols = slice(c0, c0 + width)
        w = w_refs[src][w0:w0 + width, :] if w_rows_are_outputs else w_refs[src][:, w0:w0 + width]
        y = matmul(xn, w)
        if kind == "silu":
            o_ref[:, cols] = _silu(y).astype(o_ref.dtype)
        elif kind == "plain":
            o_ref[:, cols] = y.astype(o_ref.dtype)
        else:
            for h0 in range(0, width, LANES):
                yh = y[:, h0:h0 + LANES]
                gain = gain_ref[:, c0 + h0:c0 + h0 + LANES]
                o_ref[:, c0 + h0:c0 + h0 + LANES] = (yh * _group_rms_inv(yh, kind) * gain).astype(o_ref.dtype)


def _norm_proj(h, g, weights, chunks, gains, w_side=None, w_rows_are_outputs=False):
    t, d = h.shape
    n = gains.shape[1]
    assert sum(width for _, _, _, width, _ in chunks) == n
    tm = PROJ_TM
    resident = lambda shape: pl.BlockSpec(shape, lambda i: (0, 0), pipeline_mode=pl.Buffered(1))
    in_specs = [
        pl.BlockSpec((tm, d), lambda i: (i, 0)),
        pl.BlockSpec((1, d), lambda i: (0, 0)),
        pl.BlockSpec((1, n), lambda i: (0, 0)),
    ] + [resident(w.shape) for w in weights]
    out_specs = pl.BlockSpec((tm, n), lambda i: (i, 0))
    out_shape = jax.ShapeDtypeStruct((t, n), BF16)
    args = [h, g.reshape(1, d), gains, *weights]
    blocks = tm * d * 4 + tm * n * 2 + 8 * n * 4
    single = sum(w.size * 2 for w in weights)
    if w_side is not None:
        ns = w_side.shape[0] if w_rows_are_outputs else w_side.shape[1]
        in_specs.append(resident(w_side.shape))
        out_specs = [out_specs, pl.BlockSpec((tm, ns), lambda i: (i, 0))]
        out_shape = [out_shape, jax.ShapeDtypeStruct((t, ns), F32)]
        args.append(w_side)
        blocks += tm * ns * 4
        single += w_side.size * 2
    return pl.pallas_call(
        functools.partial(_proj_kernel, n_weights=len(weights), has_side=w_side is not None,
                          w_rows_are_outputs=w_rows_are_outputs, chunks=tuple(chunks)),
        grid=(t // tm,),
        in_specs=in_specs,
        out_specs=out_specs,
        out_shape=out_shape,
        compiler_params=pltpu.CompilerParams(
            dimension_semantics=("parallel",),
            vmem_limit_bytes=_vmem_limit(
                blocks, single, 3 * tm * max(c[3] for c in chunks) * 4 + 2 * tm * d * 4)),
        name="norm_proj",
    )(*args)


def _split3(c):
    c1 = c.astype(BF16).astype(F32)
    r = c - c1
    c2 = r.astype(BF16).astype(F32)
    c3 = (r - c2).astype(BF16).astype(F32)
    return c1, c2, c3


def _bias_lanes(term, key_side):
    n = term.shape[1]
    parts = jnp.concatenate(_split3(term), axis=0)
    ones = jnp.ones((3, n), F32)
    pad = jnp.zeros((10, n), F32)
    rows = jnp.concatenate([parts, ones, pad] if key_side else [ones, parts, pad], axis=0)
    sel = (lax.broadcasted_iota(jnp.int32, (16, LANES), 0)
           == lax.broadcasted_iota(jnp.int32, (16, LANES), 1)).astype(BF16)
    lanes = lax.dot_general(rows.astype(BF16), sel, (((0,), (0,)), ((), ())),
                            preferred_element_type=F32)
    return lanes.astype(BF16)


def _score_phase(chains, c, t):
    krow = lax.broadcasted_iota(jnp.int32, (t, t), 0)
    qcol = lax.broadcasted_iota(jnp.int32, (t, t), 1)
    lo, hi = c * t, (c + 1) * t
    scores = []
    for qa, ka_ref, _ in chains:
        s_d = jnp.where(krow <= qcol, _nt_dot(ka_ref[lo:hi, :], qa), NEG)
        m = jnp.max(s_d, axis=0, keepdims=True)
        s_f = None
        if c > 0:
            s_f = _nt_dot(ka_ref[0:lo, :], qa)
            m = jnp.maximum(m, jnp.max(s_f, axis=0, keepdims=True))
        scores.append((s_d, s_f, m))
    return scores


def _value_phase(chains, scores, c, t):
    lo, hi = c * t, (c + 1) * t
    outs = []
    for (_, _, vt_ref), (s_d, s_f, m) in zip(chains, scores):
        dv = vt_ref.shape[0] - ONES_ROWS
        p_d = jnp.exp2(s_d - m).astype(BF16)
        acc = jnp.dot(vt_ref[:, lo:hi], p_d, preferred_element_type=F32)
        if c > 0:
            p_f = jnp.exp2(s_f - m).astype(BF16)
            acc = acc + jnp.dot(vt_ref[:, 0:lo], p_f, preferred_element_type=F32)
        outs.append((acc[:dv], acc[dv:dv + 1]))
    return outs


def _attend_all_tiles(make_chains, finish, n_tiles, t):
    order = list(range(1, n_tiles, 2)) + list(range(n_tiles - 1 - (n_tiles % 2 == 0), -1, -2))
    assert sorted(order) == list(range(n_tiles))
    chains = make_chains(order[0])
    scores = _score_phase(chains, order[0], t)
    for pos, c in enumerate(order):
        if pos + 1 < n_tiles:
            nxt = order[pos + 1]
            next_chains = make_chains(nxt)
            next_scores = _score_phase(next_chains, nxt, t)
        finish(c, _value_phase(chains, scores, c, t))
        if pos + 1 < n_tiles:
            chains, scores = next_chains, next_scores


def _diff_kernel(slopes_ref, q_ref, k_ref, v_ref, z_ref, lam_ref, og_ref,
                 o_ref, ka_ref, vt_ref, *, lam_init, group):
    t = ATT_T
    hw = A_V_DIM
    seq = k_ref.shape[0]
    hg = pl.program_id(1)

    k_pos = lax.broadcasted_iota(jnp.int32, (1, seq), 1).astype(F32)
    for g in range(group):
        cs = slice(g * hw, (g + 1) * hw)
        ka_ref[g, :, :hw] = k_ref[:, cs]
        ka_ref[g, :, hw:] = _bias_lanes(LOG2E * slopes_ref[hg * group + g] * k_pos, True)
        vt_ref[g, :hw] = v_ref[:, cs].astype(F32).T.astype(BF16)
        vt_ref[g, hw:] = jnp.ones((ONES_ROWS, seq), BF16)

    lo = lax.broadcasted_iota(jnp.int32, (t, hw), 1) < A_QK_DIM
    lp = lam_ref[...]
    lam = (jnp.exp(jnp.sum(lp[0:1, :] * lp[1:2, :], axis=-1, keepdims=True))
           - jnp.exp(jnp.sum(lp[2:3, :] * lp[3:4, :], axis=-1, keepdims=True)) + lam_init)

    def make_chains(c):
        rows = slice(c * t, (c + 1) * t)
        q_pos = (c * t + lax.broadcasted_iota(jnp.int32, (1, t), 1)).astype(F32)
        chains = []
        for g in range(group):
            qn = q_ref[rows, g * hw:(g + 1) * hw]
            qb = _bias_lanes(-LOG2E * slopes_ref[hg * group + g] * q_pos, False)
            zero = jnp.zeros_like(qn)
            q1 = jnp.concatenate([jnp.where(lo, qn, zero), qb], axis=1)
            q2 = jnp.concatenate([jnp.where(lo, zero, qn), qb], axis=1)
            chains += [(q1, ka_ref.at[g], vt_ref.at[g]), (q2, ka_ref.at[g], vt_ref.at[g])]
        return chains

    def finish(c, outs):
        rows = slice(c * t, (c + 1) * t)
        for g in range(group):
            cs = slice(g * hw, (g + 1) * hw)
            (a1, l1), (a2, l2) = outs[2 * g], outs[2 * g + 1]
            o = (a1 / l1 - lam * (a2 / l2)).T
            ms = jnp.mean(o * o, axis=-1, keepdims=True)
            on = o * lax.rsqrt(ms + EPS) * og_ref[...] * (1.0 - lam_init)
            o_ref[rows, cs] = (on * z_ref[rows, cs].astype(F32)).astype(o_ref.dtype)

    _attend_all_tiles(make_chains, finish, seq // t, t)


def _diff_attention(proj, lam_params, out_g, lam_init, bsz, seq):
    hw = A_V_DIM
    grp = A_GROUP
    gw = grp * hw
    ng = A_HEADS // grp
    slopes = jnp.asarray(np.array([2.0 ** (-8.0 * (i + 1) / A_HEADS) for i in range(A_HEADS)],
                                  dtype=np.float32))
    blocks = 5 * seq * gw * 2
    scratch = grp * seq * 3 * hw * 2
    small = lambda shape: pl.BlockSpec(shape, lambda b, h: (0, 0))
    cols = lambda first: pl.BlockSpec((seq, gw), lambda b, h: (b, first * ng + h))
    return pl.pallas_call(
        functools.partial(_diff_kernel, lam_init=lam_init, group=grp),
        grid=(bsz, ng),
        in_specs=[
            pl.BlockSpec(memory_space=pltpu.SMEM),
            cols(0), cols(1), cols(2), cols(3),
            small((4, A_QK_DIM)), small((1, hw)),
        ],
        out_specs=cols(0),
        out_shape=jax.ShapeDtypeStruct((bsz * seq, A_WIDTH), BF16),
        scratch_shapes=[pltpu.VMEM((grp, seq, 2 * hw), BF16), pltpu.VMEM((grp, hw + ONES_ROWS, seq), BF16)],
        compiler_params=pltpu.CompilerParams(
            dimension_semantics=("parallel", "parallel"),
            vmem_limit_bytes=_vmem_limit(blocks, scratch, ATT_TEMP_BYTES)),
        name="diff_attn",
    )(slopes, proj, proj, proj, proj, lam_params, out_g.reshape(1, hw))


def _conv_gated(h_ref, c_ref, hp_ref, cp_ref, b_ref, z_ref, w_ref, bias_ref, starts_sequence):
    u = c_ref[...].astype(F32) * h_ref[...].astype(F32)
    up = cp_ref[...].astype(F32) * hp_ref[...].astype(F32)
    up = jnp.where(starts_sequence, 0.0, up)
    row = lax.broadcasted_iota(jnp.int32, u.shape, 0)
    u1 = jnp.where(row >= 1, pltpu.roll(u, 1, 0), up[HALO - 1:HALO, :])
    u2 = jnp.where(row >= 2, pltpu.roll(u, 2, 0),
                   jnp.where(row == 1, up[HALO - 1:HALO, :], up[HALO - 2:HALO - 1, :]))
    w = w_ref[...]
    y = w[0:1, :] * u2 + w[1:2, :] * u1 + w[2:3, :] * u + bias_ref[...]
    return (b_ref[...].astype(F32) * y * z_ref[...].astype(F32)).astype(BF16)


def _mem_kv_kernel(mem_ref, g_ref, w_ref, kg_ref, o_ref):
    x = mem_ref[...]
    ms = jnp.mean(x * x, axis=-1, keepdims=True)
    xn = (x * lax.rsqrt(ms + EPS) * g_ref[...]).astype(BF16)
    kv = jnp.dot(xn, w_ref[...], preferred_element_type=F32)
    for hd in range(X_HEADS):
        k = kv[:, hd * X_DIM:(hd + 1) * X_DIM]
        kms = jnp.mean(k * k, axis=-1, keepdims=True)
        o_ref[:, hd * X_DIM:(hd + 1) * X_DIM] = (k * lax.rsqrt(kms + EPS) * kg_ref[...]).astype(BF16)
    o_ref[:, X_WIDTH:] = kv[:, X_WIDTH:].astype(BF16)


def _mem_kv(mem2d, g, w, k_g, bsz):
    d = mem2d.shape[1]
    blocks = MEM_LEN * d * 4 + w.size * 2 + MEM_LEN * 2 * X_WIDTH * 2
    return pl.pallas_call(
        _mem_kv_kernel,
        grid=(bsz,),
        in_specs=[pl.BlockSpec((MEM_LEN, d), lambda b: (b, 0)),
                  pl.BlockSpec((1, d), lambda b: (0, 0)),
                  pl.BlockSpec(w.shape, lambda b: (0, 0)),
                  pl.BlockSpec((1, X_DIM), lambda b: (0, 0))],
        out_specs=pl.BlockSpec((MEM_LEN, 2 * X_WIDTH), lambda b: (b, 0)),
        out_shape=jax.ShapeDtypeStruct((bsz * MEM_LEN, 2 * X_WIDTH), BF16),
        compiler_params=pltpu.CompilerParams(
            dimension_semantics=("parallel",),
            vmem_limit_bytes=_vmem_limit(blocks, 0, 4 * MEM_LEN * d * 4)),
        name="mem_kv",
    )(mem2d, g.reshape(1, d), w, k_g.reshape(1, X_DIM))


def _xattn_scores(q_ref, kv_ref):
    return [_nt_dot(q_ref[:, hd * X_DIM:(hd + 1) * X_DIM], kv_ref[:, hd * X_DIM:(hd + 1) * X_DIM])
            for hd in range(X_HEADS)]


def _xattn_gated(scores, z_ref, kv_ref):
    heads = []
    for hd, s in enumerate(scores):
        cs = slice(hd * X_DIM, (hd + 1) * X_DIM)
        m = jnp.max(s, axis=-1, keepdims=True)
        p = jnp.exp(s - m)
        l = jnp.sum(p, axis=-1, keepdims=True)
        vs = slice(X_WIDTH + hd * X_DIM, X_WIDTH + (hd + 1) * X_DIM)
        o = jnp.dot(p.astype(BF16), kv_ref[:, vs], preferred_element_type=F32) / l
        heads.append((o * z_ref[:, cs].astype(F32)).astype(BF16))
    return jnp.concatenate(heads, axis=1)


def _forget_cum_kernel(f_ref, b_ref, cum_ref):
    x = f_ref[...] + b_ref[...]
    c = jnp.minimum(x, 0.0) - jnp.log1p(jnp.exp(-jnp.abs(x)))
    n = c.shape[0]
    row = lax.broadcasted_iota(jnp.int32, c.shape, 0)
    shift = 1
    while shift < n:
        c = c + jnp.where(row >= shift, pltpu.roll(c, shift, 0), 0.0)
        shift *= 2
    cum_ref[...] = c.T[:cum_ref.shape[0], :]


def _forget_cum(f_logits, f_bias, bsz, seq):
    bias = jnp.pad(f_bias.astype(F32), (0, LANES - C_HEADS)).reshape(1, LANES)
    blocks = seq * LANES * 4 + CUM_ROWS * seq * 4
    return pl.pallas_call(
        _forget_cum_kernel,
        grid=(bsz,),
        in_specs=[pl.BlockSpec((seq, LANES), lambda b: (b, 0)),
                  pl.BlockSpec((1, LANES), lambda b: (0, 0))],
        out_specs=pl.BlockSpec((CUM_ROWS, seq), lambda b: (b, 0)),
        out_shape=jax.ShapeDtypeStruct((bsz * CUM_ROWS, seq), F32),
        compiler_params=pltpu.CompilerParams(
            dimension_semantics=("parallel",),
            vmem_limit_bytes=_vmem_limit(blocks, 0, 6 * seq * LANES * 4)),
        name="forget_cum",
    )(f_logits, bias)


def _fox_kernel(q_ref, k_ref, v_ref, z_ref, cum_ref, o_ref, ka_ref, vt_ref, *, group):
    t = ATT_T
    hw = C_DIM
    seq = k_ref.shape[0]

    for g in range(group):
        cs = slice(g * hw, (g + 1) * hw)
        ka_ref[g, :, :hw] = k_ref[:, cs]
        ka_ref[g, :, hw:] = _bias_lanes(-LOG2E * cum_ref[g], True)
        vt_ref[g, :hw] = v_ref[:, cs].astype(F32).T.astype(BF16)
        vt_ref[g, hw:] = jnp.ones((ONES_ROWS, seq), BF16)

    def make_chains(c):
        rows = slice(c * t, (c + 1) * t)
        chains = []
        for g in range(group):
            qb = _bias_lanes(LOG2E * cum_ref[g, :, rows], False)
            qa = jnp.concatenate([q_ref[rows, g * hw:(g + 1) * hw], qb], axis=1)
            chains.append((qa, ka_ref.at[g], vt_ref.at[g]))
        return chains

    def finish(c, outs):
        rows = slice(c * t, (c + 1) * t)
        for g, (acc, l) in enumerate(outs):
            cs = slice(g * hw, (g + 1) * hw)
            o_ref[rows, cs] = ((acc / l).T * z_ref[rows, cs].astype(F32)).astype(o_ref.dtype)

    _attend_all_tiles(make_chains, finish, seq // t, t)


def _fox_attention(proj, cum, bsz, seq):
    hw = C_DIM
    grp = C_GROUP
    gw = grp * hw
    ng = C_HEADS // grp
    blocks = 5 * seq * gw * 2 + grp * 8 * seq * 4
    scratch = grp * seq * 3 * hw * 2
    cum3 = cum.reshape(bsz * CUM_ROWS, 1, seq)
    cols = lambda first: pl.BlockSpec((seq, gw), lambda b, h: (b, first * ng + h))
    return pl.pallas_call(
        functools.partial(_fox_kernel, group=grp),
        grid=(bsz, ng),
        in_specs=[
            cols(0), cols(1), cols(2), cols(3),
            pl.BlockSpec((grp, 1, seq), lambda b, h: (b * (CUM_ROWS // grp) + h, 0, 0)),
        ],
        out_specs=cols(0),
        out_shape=jax.ShapeDtypeStruct((bsz * seq, C_WIDTH), BF16),
        scratch_shapes=[pltpu.VMEM((grp, seq, 2 * hw), BF16), pltpu.VMEM((grp, hw + ONES_ROWS, seq), BF16)],
        compiler_params=pltpu.CompilerParams(
            dimension_semantics=("parallel", "parallel"),
            vmem_limit_bytes=_vmem_limit(blocks, scratch, ATT_TEMP_BYTES)),
        name="fox_attn",
    )(proj, proj, proj, proj, cum3)


def _mix_out_kernel(*refs, has_conv, tiles_per_seq):
    refs = list(refs)
    attn_ref = refs.pop(0)
    conv_refs = [refs.pop(0) for _ in range(8)] if has_conv else None
    xq_ref, xz_ref, kv_ref, w_ref, h_ref, o_ref = refs

    scores = _xattn_scores(xq_ref, kv_ref)
    width = attn_ref.shape[1]
    acc = h_ref[...] + jnp.dot(attn_ref[...], w_ref[:width, :], preferred_element_type=F32)
    if has_conv:
        starts_sequence = pl.program_id(0) % tiles_per_seq == 0
        yb = _conv_gated(*conv_refs, starts_sequence)
        acc = acc + jnp.dot(yb, w_ref[width:width + B_WIDTH, :], preferred_element_type=F32)
        width += B_WIDTH
    yx = _xattn_gated(scores, xz_ref, kv_ref)
    o_ref[...] = acc + jnp.dot(yx, w_ref[width:, :], preferred_element_type=F32)


def _mix_out(y_attn, proj, kv, w_out, h, seq, x_col_block, conv=None):
    t, d = h.shape
    tm = OUT_TM
    tiles_per_seq = seq // tm
    aw = y_attn.shape[1]
    rows = lambda width, col: pl.BlockSpec((tm, width), lambda i: (i, col))
    in_specs = [rows(aw, 0)]
    args = [y_attn]
    blocks = tm * aw * 2 + 2 * tm * X_WIDTH * 2 + MEM_LEN * 2 * X_WIDTH * 2 + w_out.size * 2 + 2 * tm * d * 4
    if conv is not None:
        cw, cb, base = conv
        prev = lambda col: pl.BlockSpec(
            (HALO, B_WIDTH), lambda i: (jnp.maximum(i * (tm // HALO) - 1, 0), col))
        in_specs += [rows(B_WIDTH, base), rows(B_WIDTH, base + 1), prev(base), prev(base + 1),
                     rows(B_WIDTH, base + 2), rows(B_WIDTH, base + 3),
                     pl.BlockSpec((CONV_W, B_WIDTH), lambda i: (0, 0)),
                     pl.BlockSpec((1, B_WIDTH), lambda i: (0, 0))]
        args += [proj] * 6 + [cw, cb.reshape(1, B_WIDTH)]
        blocks += 4 * tm * B_WIDTH * 2
    in_specs += [rows(X_WIDTH, x_col_block), rows(X_WIDTH, x_col_block + 1),
                 pl.BlockSpec((MEM_LEN, 2 * X_WIDTH), lambda i: (i // tiles_per_seq, 0)),
                 pl.BlockSpec(w_out.shape, lambda i: (0, 0)),
                 pl.BlockSpec((tm, d), lambda i: (i, 0))]
    args += [proj, proj, kv, w_out, h]
    return pl.pallas_call(
        functools.partial(_mix_out_kernel, has_conv=conv is not None, tiles_per_seq=tiles_per_seq),
        grid=(t // tm,),
        in_specs=in_specs,
        out_specs=pl.BlockSpec((tm, d), lambda i: (i, 0)),
        out_shape=jax.ShapeDtypeStruct((t, d), F32),
        compiler_params=pltpu.CompilerParams(
            dimension_semantics=("parallel",),
            vmem_limit_bytes=_vmem_limit(blocks, 0, 3 * tm * d * 4 + 8 * tm * B_WIDTH * 4)),
        name="mix_out",
    )(*args)


def _even_layer(h, mem2d, layer, bsz, seq, norm_g, w_in, w_out, a_qn, a_kn, a_lam, a_on,
                b_cw, b_cb, x_qn, x_kn, mem_g, w_mem_kv):
    lam_init = 0.8 - 0.6 * math.exp(-0.3 * layer)
    conv0 = 4 * A_WIDTH
    x0 = conv0 + 4 * B_WIDTH
    chunks = [(0, A_WIDTH, A_QK_DIM), (A_WIDTH, A_WIDTH, A_QK_DIM), (x0, X_WIDTH, X_DIM),
              (3 * A_WIDTH, A_WIDTH, "silu"), (conv0 + 3 * B_WIDTH, B_WIDTH, "silu"),
              (x0 + X_WIDTH, X_WIDTH, "silu"), (2 * A_WIDTH, A_WIDTH, "plain"), (conv0, 3 * B_WIDTH, "plain")]
    chunks = [(0, c0, c0, width, kind) for c0, width, kind in chunks]
    gains = jnp.concatenate([
        jnp.tile(a_qn, 2 * A_HEADS) * (LOG2E * A_QK_DIM ** -0.5), jnp.tile(a_kn, 2 * A_HEADS),
        jnp.ones((x0 - 2 * A_WIDTH,), F32), jnp.tile(x_qn, X_HEADS) * (X_DIM ** -0.5),
        jnp.ones((X_WIDTH,), F32)]).reshape(1, -1)
    proj = _norm_proj(h, norm_g, [w_in.astype(BF16)], chunks, gains)
    ya = _diff_attention(proj, a_lam, a_on, lam_init, bsz, seq)
    kv = _mem_kv(mem2d, mem_g, w_mem_kv.astype(BF16), x_kn, bsz)
    return _mix_out(ya, proj, kv, w_out.astype(BF16), h, seq, x_col_block=x0 // X_WIDTH,
                    conv=(b_cw, b_cb, conv0 // B_WIDTH))


def _odd_layer(h, mem2d, bsz, seq, norm_g, w_in, w_out, c_qn, c_kn, c_fb, x_qn, x_kn,
               mem_g, w_mem_kv):
    main = 4 * C_WIDTH
    w_t = w_in.T.astype(BF16)
    w_x = w_t[main + C_HEADS:]
    w_f = jnp.pad(w_t[main:main + C_HEADS], ((0, LANES - C_HEADS), (0, 0)))
    chunks = [(0, 0, 0, C_WIDTH, C_DIM), (0, C_WIDTH, C_WIDTH, C_WIDTH, C_DIM),
              (1, 0, main, X_WIDTH, X_DIM), (0, 3 * C_WIDTH, 3 * C_WIDTH, C_WIDTH, "silu"),
              (1, X_WIDTH, main + X_WIDTH, X_WIDTH, "silu"), (0, 2 * C_WIDTH, 2 * C_WIDTH, C_WIDTH, "plain")]
    gains = jnp.concatenate([
        jnp.tile(c_qn, C_HEADS) * (LOG2E * C_DIM ** -0.5), jnp.tile(c_kn, C_HEADS),
        jnp.ones((main - 2 * C_WIDTH,), F32), jnp.tile(x_qn, X_HEADS) * (X_DIM ** -0.5),
        jnp.ones((X_WIDTH,), F32)]).reshape(1, -1)
    proj, f_logits = _norm_proj(h, norm_g, [w_t, w_x], chunks, gains, w_f, w_rows_are_outputs=True)
    cum = _forget_cum(f_logits, c_fb, bsz, seq)
    yc = _fox_attention(proj, cum, bsz, seq)
    kv = _mem_kv(mem2d, mem_g, w_mem_kv.astype(BF16), x_kn, bsz)
    return _mix_out(yc, proj, kv, w_out.astype(BF16), h, seq, x_col_block=main // X_WIDTH)


def kernel(x, mem, e_norm_g, e_w_in, e_w_out, e_a_q_norm_g, e_a_k_norm_g, e_a_lambda,
           e_a_out_norm_g, e_b_conv_w, e_b_conv_b, e_x_q_norm_g, e_x_k_norm_g, e_mem_norm_g,
           e_w_mem_kv, o_norm_g, o_w_in, o_w_out, o_c_q_norm_g, o_c_k_norm_g, o_c_forget_b,
           o_x_q_norm_g, o_x_k_norm_g, o_mem_norm_g, o_w_mem_kv):
    bsz, seq, d = x.shape
    h = x.reshape(bsz * seq, d)
    mem2d = mem.reshape(bsz * MEM_LEN, d)
    depth = e_w_in.shape[0] + o_w_in.shape[0]
    for layer in range(depth):
        i = layer // 2
        if layer % 2 == 0:
            h = _even_layer(h, mem2d, layer, bsz, seq, e_norm_g[i], e_w_in[i], e_w_out[i],
                            e_a_q_norm_g[i], e_a_k_norm_g[i], e_a_lambda[i], e_a_out_norm_g[i],
                            e_b_conv_w[i], e_b_conv_b[i], e_x_q_norm_g[i], e_x_k_norm_g[i],
                            e_mem_norm_g[i], e_w_mem_kv[i])
        else:
            h = _odd_layer(h, mem2d, bsz, seq, o_norm_g[i], o_w_in[i], o_w_out[i],
                           o_c_q_norm_g[i], o_c_k_norm_g[i], o_c_forget_b[i],
                           o_x_q_norm_g[i], o_x_k_norm_g[i], o_mem_norm_g[i], o_w_mem_kv[i])
    return h.reshape(bsz, seq, d)
```

```python
import functools
import math

import numpy as np
import jax
import jax.numpy as jnp
from jax import lax
from jax.experimental import pallas as pl
from jax.experimental.pallas import tpu as pltpu

F32 = jnp.float32
BF16 = jnp.bfloat16

D_MODEL = 1024
MEM_LEN = 256
EPS = 1e-6
NEG = -1e30
LOG2E = math.log2(math.e)

A_HEADS = 8
A_QK_DIM = 64
A_V_DIM = 128
A_WIDTH = 1024
B_WIDTH = 512
CONV_W = 3
C_HEADS = 12
C_DIM = 128
C_WIDTH = 1536
X_HEADS = 4
X_DIM = 128
X_WIDTH = 512
MIX = 2048

LANES = 128
ONES_ROWS = 16
CUM_ROWS = 16
VMEM_CAP = 56 * 1024 * 1024

PROJ_TM = 512
OUT_TM = 512
HALO = 8
ATT_T = 256
A_GROUP = 4
C_GROUP = 4
ATT_TEMP_BYTES = 26 << 20


def _vmem_limit(block_bytes, scratch_bytes=0, temp_bytes=0):
    need = 2 * block_bytes + scratch_bytes + temp_bytes + (4 << 20)
    return int(min(max(need, 16 << 20), VMEM_CAP))


def _silu(z):
    half = 0.5 * z
    return half + half * jnp.tanh(half)


def _nt_dot(a, b):
    return lax.dot_general(a, b, (((1,), (1,)), ((), ())), preferred_element_type=F32)


def _group_rms_inv(x, group):
    x2 = x * x
    if group == LANES:
        return lax.rsqrt(jnp.mean(x2, axis=-1, keepdims=True) + EPS)
    assert 2 * group == LANES
    first = lax.broadcasted_iota(jnp.int32, x.shape, 1) < group
    ms_a = jnp.sum(jnp.where(first, x2, 0.0), axis=-1, keepdims=True) * (1.0 / group)
    ms_b = jnp.sum(jnp.where(first, 0.0, x2), axis=-1, keepdims=True) * (1.0 / group)
    return lax.rsqrt(jnp.where(first, ms_a, ms_b) + EPS)


def _proj_kernel(*refs, n_weights, has_side, w_rows_are_outputs, chunks):
    x_ref, g_ref, gain_ref = refs[:3]
    w_refs = refs[3:3 + n_weights]
    if has_side:
        ws_ref, o_ref, side_ref = refs[3 + n_weights:]
    else:
        (o_ref,) = refs[3 + n_weights:]
    matmul = _nt_dot if w_rows_are_outputs else functools.partial(jnp.dot, preferred_element_type=F32)

    x = x_ref[...]
    ms = jnp.mean(x * x, axis=-1, keepdims=True)
    xn = (x * lax.rsqrt(ms + EPS) * g_ref[...]).astype(BF16)
    if has_side:
        side_ref[...] = matmul(xn, ws_ref[...])
    for src, w0, c0, width, kind in chunks:
        cols = slice(c0, c0 + width)
        w = w_refs[src][w0:w0 + width, :] if w_rows_are_outputs else w_refs[src][:, w0:w0 + width]
        y = matmul(xn, w)
        if kind == "silu":
            o_ref[:, cols] = _silu(y).astype(o_ref.dtype)
        elif kind == "plain":
            o_ref[:, cols] = y.astype(o_ref.dtype)
        else:
            for h0 in range(0, width, LANES):
                yh = y[:, h0:h0 + LANES]
                gain = gain_ref[:, c0 + h0:c0 + h0 + LANES]
                o_ref[:, c0 + h0:c0 + h0 + LANES] = (yh * _group_rms_inv(yh, kind) * gain).astype(o_ref.dtype)


def _norm_proj(h, g, weights, chunks, gains, w_side=None, w_rows_are_outputs=False):
    t, d = h.shape
    n = gains.shape[1]
    assert sum(width for _, _, _, width, _ in chunks) == n
    tm = PROJ_TM
    resident = lambda shape: pl.BlockSpec(shape, lambda i: (0, 0), pipeline_mode=pl.Buffered(1))
    in_specs = [
        pl.BlockSpec((tm, d), lambda i: (i, 0)),
        pl.BlockSpec((1, d), lambda i: (0, 0)),
        pl.BlockSpec((1, n), lambda i: (0, 0)),
    ] + [resident(w.shape) for w in weights]
    out_specs = pl.BlockSpec((tm, n), lambda i: (i, 0))
    out_shape = jax.ShapeDtypeStruct((t, n), BF16)
    args = [h, g.reshape(1, d), gains, *weights]
    blocks = tm * d * 4 + tm * n * 2 + 8 * n * 4
    single = sum(w.size * 2 for w in weights)
    if w_side is not None:
        ns = w_side.shape[0] if w_rows_are_outputs else w_side.shape[1]
        in_specs.append(resident(w_side.shape))
        out_specs = [out_specs, pl.BlockSpec((tm, ns), lambda i: (i, 0))]
        out_shape = [out_shape, jax.ShapeDtypeStruct((t, ns), F32)]
        args.append(w_side)
        blocks += tm * ns * 4
        single += w_side.size * 2
    return pl.pallas_call(
        functools.partial(_proj_kernel, n_weights=len(weights), has_side=w_side is not None,
                          w_rows_are_outputs=w_rows_are_outputs, chunks=tuple(chunks)),
        grid=(t // tm,),
        in_specs=in_specs,
        out_specs=out_specs,
        out_shape=out_shape,
        compiler_params=pltpu.CompilerParams(
            dimension_semantics=("parallel",),
            vmem_limit_bytes=_vmem_limit(
                blocks, single, 3 * tm * max(c[3] for c in chunks) * 4 + 2 * tm * d * 4)),
        name="norm_proj",
    )(*args)


def _split3(c):
    c1 = c.astype(BF16).astype(F32)
    r = c - c1
    c2 = r.astype(BF16).astype(F32)
    c3 = (r - c2).astype(BF16).astype(F32)
    return c1, c2, c3


def _bias_lanes(term, key_side):
    n = term.shape[1]
    parts = jnp.concatenate(_split3(term), axis=0)
    ones = jnp.ones((3, n), F32)
    pad = jnp.zeros((10, n), F32)
    rows = jnp.concatenate([parts, ones, pad] if key_side else [ones, parts, pad], axis=0)
    sel = (lax.broadcasted_iota(jnp.int32, (16, LANES), 0)
           == lax.broadcasted_iota(jnp.int32, (16, LANES), 1)).astype(BF16)
    lanes = lax.dot_general(rows.astype(BF16), sel, (((0,), (0,)), ((), ())),
                            preferred_element_type=F32)
    return lanes.astype(BF16)


def _score_phase(chains, c, t):
    krow = lax.broadcasted_iota(jnp.int32, (t, t), 0)
    qcol = lax.broadcasted_iota(jnp.int32, (t, t), 1)
    lo, hi = c * t, (c + 1) * t
    scores = []
    for qa, ka_ref, _ in chains:
        s_d = jnp.where(krow <= qcol, _nt_dot(ka_ref[lo:hi, :], qa), NEG)
        m = jnp.max(s_d, axis=0, keepdims=True)
        s_f = None
        if c > 0:
            s_f = _nt_dot(ka_ref[0:lo, :], qa)
            m = jnp.maximum(m, jnp.max(s_f, axis=0, keepdims=True))
        scores.append((s_d, s_f, m))
    return scores


def _value_phase(chains, scores, c, t):
    lo, hi = c * t, (c + 1) * t
    outs = []
    for (_, _, vt_ref), (s_d, s_f, m) in zip(chains, scores):
        dv = vt_ref.shape[0] - ONES_ROWS
        p_d = jnp.exp2(s_d - m).astype(BF16)
        acc = jnp.dot(vt_ref[:, lo:hi], p_d, preferred_element_type=F32)
        if c > 0:
            p_f = jnp.exp2(s_f - m).astype(BF16)
            acc = acc + jnp.dot(vt_ref[:, 0:lo], p_f, preferred_element_type=F32)
        outs.append((acc[:dv], acc[dv:dv + 1]))
    return outs


def _attend_all_tiles(make_chains, finish, n_tiles, t):
    order = list(range(1, n_tiles, 2)) + list(range(n_tiles - 1 - (n_tiles % 2 == 0), -1, -2))
    assert sorted(order) == list(range(n_tiles))
    chains = make_chains(order[0])
    scores = _score_phase(chains, order[0], t)
    for pos, c in enumerate(order):
        if pos + 1 < n_tiles:
            nxt = order[pos + 1]
            next_chains = make_chains(nxt)
            next_scores = _score_phase(next_chains, nxt, t)
        finish(c, _value_phase(chains, scores, c, t))
        if pos + 1 < n_tiles:
            chains, scores = next_chains, next_scores


def _diff_kernel(slopes_ref, q_ref, k_ref, v_ref, z_ref, lam_ref, og_ref,
                 o_ref, ka_ref, vt_ref, *, lam_init, group):
    t = ATT_T
    hw = A_V_DIM
    seq = k_ref.shape[0]
    hg = pl.program_id(1)

    k_pos = lax.broadcasted_iota(jnp.int32, (1, seq), 1).astype(F32)
    for g in range(group):
        cs = slice(g * hw, (g + 1) * hw)
        ka_ref[g, :, :hw] = k_ref[:, cs]
        ka_ref[g, :, hw:] = _bias_lanes(LOG2E * slopes_ref[hg * group + g] * k_pos, True)
        vt_ref[g, :hw] = v_ref[:, cs].astype(F32).T.astype(BF16)
        vt_ref[g, hw:] = jnp.ones((ONES_ROWS, seq), BF16)

    lo = lax.broadcasted_iota(jnp.int32, (t, hw), 1) < A_QK_DIM
    lp = lam_ref[...]
    lam = (jnp.exp(jnp.sum(lp[0:1, :] * lp[1:2, :], axis=-1, keepdims=True))
           - jnp.exp(jnp.sum(lp[2:3, :] * lp[3:4, :], axis=-1, keepdims=True)) + lam_init)

    def make_chains(c):
        rows = slice(c * t, (c + 1) * t)
        q_pos = (c * t + lax.broadcasted_iota(jnp.int32, (1, t), 1)).astype(F32)
        chains = []
        for g in range(group):
            qn = q_ref[rows, g * hw:(g + 1) * hw]
            qb = _bias_lanes(-LOG2E * slopes_ref[hg * group + g] * q_pos, False)
            zero = jnp.zeros_like(qn)
            q1 = jnp.concatenate([jnp.where(lo, qn, zero), qb], axis=1)
            q2 = jnp.concatenate([jnp.where(lo, zero, qn), qb], axis=1)
            chains += [(q1, ka_ref.at[g], vt_ref.at[g]), (q2, ka_ref.at[g], vt_ref.at[g])]
        return chains

    def finish(c, outs):
        rows = slice(c * t, (c + 1) * t)
        for g in range(group):
            cs = slice(g * hw, (g + 1) * hw)
            (a1, l1), (a2, l2) = outs[2 * g], outs[2 * g + 1]
            o = (a1 / l1 - lam * (a2 / l2)).T
            ms = jnp.mean(o * o, axis=-1, keepdims=True)
            on = o * lax.rsqrt(ms + EPS) * og_ref[...] * (1.0 - lam_init)
            o_ref[rows, cs] = (on * z_ref[rows, cs].astype(F32)).astype(o_ref.dtype)

    _attend_all_tiles(make_chains, finish, seq // t, t)


def _diff_attention(proj, lam_params, out_g, lam_init, bsz, seq):
    hw = A_V_DIM
    grp = A_GROUP
    gw = grp * hw
    ng = A_HEADS // grp
    slopes = jnp.asarray(np.array([2.0 ** (-8.0 * (i + 1) / A_HEADS) for i in range(A_HEADS)],
                                  dtype=np.float32))
    blocks = 5 * seq * gw * 2
    scratch = grp * seq * 3 * hw * 2
    small = lambda shape: pl.BlockSpec(shape, lambda b, h: (0, 0))
    cols = lambda first: pl.BlockSpec((seq, gw), lambda b, h: (b, first * ng + h))
    return pl.pallas_call(
        functools.partial(_diff_kernel, lam_init=lam_init, group=grp),
        grid=(bsz, ng),
        in_specs=[
            pl.BlockSpec(memory_space=pltpu.SMEM),
            cols(0), cols(1), cols(2), cols(3),
            small((4, A_QK_DIM)), small((1, hw)),
        ],
        out_specs=cols(0),
        out_shape=jax.ShapeDtypeStruct((bsz * seq, A_WIDTH), BF16),
        scratch_shapes=[pltpu.VMEM((grp, seq, 2 * hw), BF16), pltpu.VMEM((grp, hw + ONES_ROWS, seq), BF16)],
        compiler_params=pltpu.CompilerParams(
            dimension_semantics=("parallel", "parallel"),
            vmem_limit_bytes=_vmem_limit(blocks, scratch, ATT_TEMP_BYTES)),
        name="diff_attn",
    )(slopes, proj, proj, proj, proj, lam_params, out_g.reshape(1, hw))


def _conv_gated(h_ref, c_ref, hp_ref, cp_ref, b_ref, z_ref, w_ref, bias_ref, starts_sequence):
    u = c_ref[...].astype(F32) * h_ref[...].astype(F32)
    up = cp_ref[...].astype(F32) * hp_ref[...].astype(F32)
    up = jnp.where(starts_sequence, 0.0, up)
    row = lax.broadcasted_iota(jnp.int32, u.shape, 0)
    u1 = jnp.where(row >= 1, pltpu.roll(u, 1, 0), up[HALO - 1:HALO, :])
    u2 = jnp.where(row >= 2, pltpu.roll(u, 2, 0),
                   jnp.where(row == 1, up[HALO - 1:HALO, :], up[HALO - 2:HALO - 1, :]))
    w = w_ref[...]
    y = w[0:1, :] * u2 + w[1:2, :] * u1 + w[2:3, :] * u + bias_ref[...]
    return (b_ref[...].astype(F32) * y * z_ref[...].astype(F32)).astype(BF16)


def _mem_kv_kernel(mem_ref, g_ref, w_ref, kg_ref, o_ref):
    x = mem_ref[...]
    ms = jnp.mean(x * x, axis=-1, keepdims=True)
    xn = (x * lax.rsqrt(ms + EPS) * g_ref[...]).astype(BF16)
    kv = jnp.dot(xn, w_ref[...], preferred_element_type=F32)
    for hd in range(X_HEADS):
        k = kv[:, hd * X_DIM:(hd + 1) * X_DIM]
        kms = jnp.mean(k * k, axis=-1, keepdims=True)
        o_ref[:, hd * X_DIM:(hd + 1) * X_DIM] = (k * lax.rsqrt(kms + EPS) * kg_ref[...]).astype(BF16)
    o_ref[:, X_WIDTH:] = kv[:, X_WIDTH:].astype(BF16)


def _mem_kv(mem2d, g, w, k_g, bsz):
    d = mem2d.shape[1]
    blocks = MEM_LEN * d * 4 + w.size * 2 + MEM_LEN * 2 * X_WIDTH * 2
    return pl.pallas_call(
        _mem_kv_kernel,
        grid=(bsz,),
        in_specs=[pl.BlockSpec((MEM_LEN, d), lambda b: (b, 0)),
                  pl.BlockSpec((1, d), lambda b: (0, 0)),
                  pl.BlockSpec(w.shape, lambda b: (0, 0)),
                  pl.BlockSpec((1, X_DIM), lambda b: (0, 0))],
        out_specs=pl.BlockSpec((MEM_LEN, 2 * X_WIDTH), lambda b: (b, 0)),
        out_shape=jax.ShapeDtypeStruct((bsz * MEM_LEN, 2 * X_WIDTH), BF16),
        compiler_params=pltpu.CompilerParams(
            dimension_semantics=("parallel",),
            vmem_limit_bytes=_vmem_limit(blocks, 0, 4 * MEM_LEN * d * 4)),
        name="mem_kv",
    )(mem2d, g.reshape(1, d), w, k_g.reshape(1, X_DIM))


def _xattn_scores(q_ref, kv_ref):
    return [_nt_dot(q_ref[:, hd * X_DIM:(hd + 1) * X_DIM], kv_ref[:, hd * X_DIM:(hd + 1) * X_DIM])
            for hd in range(X_HEADS)]


def _xattn_gated(scores, z_ref, kv_ref):
    heads = []
    for hd, s in enumerate(scores):
        cs = slice(hd * X_DIM, (hd + 1) * X_DIM)
        m = jnp.max(s, axis=-1, keepdims=True)
        p = jnp.exp(s - m)
        l = jnp.sum(p, axis=-1, keepdims=True)
        vs = slice(X_WIDTH + hd * X_DIM, X_WIDTH + (hd + 1) * X_DIM)
        o = jnp.dot(p.astype(BF16), kv_ref[:, vs], preferred_element_type=F32) / l
        heads.append((o * z_ref[:, cs].astype(F32)).astype(BF16))
    return jnp.concatenate(heads, axis=1)


def _forget_cum_kernel(f_ref, b_ref, cum_ref):
    x = f_ref[...] + b_ref[...]
    c = jnp.minimum(x, 0.0) - jnp.log1p(jnp.exp(-jnp.abs(x)))
    n = c.shape[0]
    row = lax.broadcasted_iota(jnp.int32, c.shape, 0)
    shift = 1
    while shift < n:
        c = c + jnp.where(row >= shift, pltpu.roll(c, shift, 0), 0.0)
        shift *= 2
    cum_ref[...] = c.T[:cum_ref.shape[0], :]


def _forget_cum(f_logits, f_bias, bsz, seq):
    bias = jnp.pad(f_bias.astype(F32), (0, LANES - C_HEADS)).reshape(1, LANES)
    blocks = seq * LANES * 4 + CUM_ROWS * seq * 4
    return pl.pallas_call(
        _forget_cum_kernel,
        grid=(bsz,),
        in_specs=[pl.BlockSpec((seq, LANES), lambda b: (b, 0)),
                  pl.BlockSpec((1, LANES), lambda b: (0, 0))],
        out_specs=pl.BlockSpec((CUM_ROWS, seq), lambda b: (b, 0)),
        out_shape=jax.ShapeDtypeStruct((bsz * CUM_ROWS, seq), F32),
        compiler_params=pltpu.CompilerParams(
            dimension_semantics=("parallel",),
            vmem_limit_bytes=_vmem_limit(blocks, 0, 6 * seq * LANES * 4)),
        name="forget_cum",
    )(f_logits, bias)


def _fox_kernel(q_ref, k_ref, v_ref, z_ref, cum_ref, o_ref, ka_ref, vt_ref, *, group):
    t = ATT_T
    hw = C_DIM
    seq = k_ref.shape[0]

    for g in range(group):
        cs = slice(g * hw, (g + 1) * hw)
        ka_ref[g, :, :hw] = k_ref[:, cs]
        ka_ref[g, :, hw:] = _bias_lanes(-LOG2E * cum_ref[g], True)
        vt_ref[g, :hw] = v_ref[:, cs].astype(F32).T.astype(BF16)
        vt_ref[g, hw:] = jnp.ones((ONES_ROWS, seq), BF16)

    def make_chains(c):
        rows = slice(c * t, (c + 1) * t)
        chains = []
        for g in range(group):
            qb = _bias_lanes(LOG2E * cum_ref[g, :, rows], False)
            qa = jnp.concatenate([q_ref[rows, g * hw:(g + 1) * hw], qb], axis=1)
            chains.append((qa, ka_ref.at[g], vt_ref.at[g]))
        return chains

    def finish(c, outs):
        rows = slice(c * t, (c + 1) * t)
        for g, (acc, l) in enumerate(outs):
            cs = slice(g * hw, (g + 1) * hw)
            o_ref[rows, cs] = ((acc / l).T * z_ref[rows, cs].astype(F32)).astype(o_ref.dtype)

    _attend_all_tiles(make_chains, finish, seq // t, t)


def _fox_attention(proj, cum, bsz, seq):
    hw = C_DIM
    grp = C_GROUP
    gw = grp * hw
    ng = C_HEADS // grp
    blocks = 5 * seq * gw * 2 + grp * 8 * seq * 4
    scratch = grp * seq * 3 * hw * 2
    cum3 = cum.reshape(bsz * CUM_ROWS, 1, seq)
    cols = lambda first: pl.BlockSpec((seq, gw), lambda b, h: (b, first * ng + h))
    return pl.pallas_call(
        functools.partial(_fox_kernel, group=grp),
        grid=(bsz, ng),
        in_specs=[
            cols(0), cols(1), cols(2), cols(3),
            pl.BlockSpec((grp, 1, seq), lambda b, h: (b * (CUM_ROWS // grp) + h, 0, 0)),
        ],
        out_specs=cols(0),
        out_shape=jax.ShapeDtypeStruct((bsz * seq, C_WIDTH), BF16),
        scratch_shapes=[pltpu.VMEM((grp, seq, 2 * hw), BF16), pltpu.VMEM((grp, hw + ONES_ROWS, seq), BF16)],
        compiler_params=pltpu.CompilerParams(
            dimension_semantics=("parallel", "parallel"),
            vmem_limit_bytes=_vmem_limit(blocks, scratch, ATT_TEMP_BYTES)),
        name="fox_attn",
    )(proj, proj, proj, proj, cum3)


def _mix_out_kernel(*refs, has_conv, tiles_per_seq):
    refs = list(refs)
    attn_ref = refs.pop(0)
    conv_refs = [refs.pop(0) for _ in range(8)] if has_conv else None
    xq_ref, xz_ref, kv_ref, w_ref, h_ref, o_ref = refs

    scores = _xattn_scores(xq_ref, kv_ref)
    width = attn_ref.shape[1]
    acc = h_ref[...] + jnp.dot(attn_ref[...], w_ref[:width, :], preferred_element_type=F32)
    if has_conv:
        starts_sequence = pl.program_id(0) % tiles_per_seq == 0
        yb = _conv_gated(*conv_refs, starts_sequence)
        acc = acc + jnp.dot(yb, w_ref[width:width + B_WIDTH, :], preferred_element_type=F32)
        width += B_WIDTH
    yx = _xattn_gated(scores, xz_ref, kv_ref)
    o_ref[...] = acc + jnp.dot(yx, w_ref[width:, :], preferred_element_type=F32)


def _mix_out(y_attn, proj, kv, w_out, h, seq, x_col_block, conv=None):
    t, d = h.shape
    tm = OUT_TM
    tiles_per_seq = seq // tm
    aw = y_attn.shape[1]
    rows = lambda width, col: pl.BlockSpec((tm, width), lambda i: (i, col))
    in_specs = [rows(aw, 0)]
    args = [y_attn]
    blocks = tm * aw * 2 + 2 * tm * X_WIDTH * 2 + MEM_LEN * 2 * X_WIDTH * 2 + w_out.size * 2 + 2 * tm * d * 4
    if conv is not None:
        cw, cb, base = conv
        prev = lambda col: pl.BlockSpec(
            (HALO, B_WIDTH), lambda i: (jnp.maximum(i * (tm // HALO) - 1, 0), col))
        in_specs += [rows(B_WIDTH, base), rows(B_WIDTH, base + 1), prev(base), prev(base + 1),
                     rows(B_WIDTH, base + 2), rows(B_WIDTH, base + 3),
                     pl.BlockSpec((CONV_W, B_WIDTH), lambda i: (0, 0)),
                     pl.BlockSpec((1, B_WIDTH), lambda i: (0, 0))]
        args += [proj] * 6 + [cw, cb.reshape(1, B_WIDTH)]
        blocks += 4 * tm * B_WIDTH * 2
    in_specs += [rows(X_WIDTH, x_col_block), rows(X_WIDTH, x_col_block + 1),
                 pl.BlockSpec((MEM_LEN, 2 * X_WIDTH), lambda i: (i // tiles_per_seq, 0)),
                 pl.BlockSpec(w_out.shape, lambda i: (0, 0)),
                 pl.BlockSpec((tm, d), lambda i: (i, 0))]
    args += [proj, proj, kv, w_out, h]
    return pl.pallas_call(
        functools.partial(_mix_out_kernel, has_conv=conv is not None, tiles_per_seq=tiles_per_seq),
        grid=(t // tm,),
        in_specs=in_specs,
        out_specs=pl.BlockSpec((tm, d), lambda i: (i, 0)),
        out_shape=jax.ShapeDtypeStruct((t, d), F32),
        compiler_params=pltpu.CompilerParams(
            dimension_semantics=("parallel",),
            vmem_limit_bytes=_vmem_limit(blocks, 0, 3 * tm * d * 4 + 8 * tm * B_WIDTH * 4)),
        name="mix_out",
    )(*args)


def _even_layer(h, mem2d, layer, bsz, seq, norm_g, w_in, w_out, a_qn, a_kn, a_lam, a_on,
                b_cw, b_cb, x_qn, x_kn, mem_g, w_mem_kv):
    lam_init = 0.8 - 0.6 * math.exp(-0.3 * layer)
    conv0 = 4 * A_WIDTH
    x0 = conv0 + 4 * B_WIDTH
    chunks = [(0, A_WIDTH, A_QK_DIM), (A_WIDTH, A_WIDTH, A_QK_DIM), (x0, X_WIDTH, X_DIM),
              (3 * A_WIDTH, A_WIDTH, "silu"), (conv0 + 3 * B_WIDTH, B_WIDTH, "silu"),
              (x0 + X_WIDTH, X_WIDTH, "silu"), (2 * A_WIDTH, A_WIDTH, "plain"),
              (conv0, 2 * B_WIDTH, "plain"), (conv0 + 2 * B_WIDTH, B_WIDTH, "plain")]
    chunks = [(0, c0, c0, width, kind) for c0, width, kind in chunks]
    gains = jnp.concatenate([
        jnp.tile(a_qn, 2 * A_HEADS) * (LOG2E * A_QK_DIM ** -0.5), jnp.tile(a_kn, 2 * A_HEADS),
        jnp.ones((x0 - 2 * A_WIDTH,), F32), jnp.tile(x_qn, X_HEADS) * (X_DIM ** -0.5),
        jnp.ones((X_WIDTH,), F32)]).reshape(1, -1)
    proj = _norm_proj(h, norm_g, [w_in.astype(BF16)], chunks, gains)
    ya = _diff_attention(proj, a_lam, a_on, lam_init, bsz, seq)
    kv = _mem_kv(mem2d, mem_g, w_mem_kv.astype(BF16), x_kn, bsz)
    return _mix_out(ya, proj, kv, w_out.astype(BF16), h, seq, x_col_block=x0 // X_WIDTH,
                    conv=(b_cw, b_cb, conv0 // B_WIDTH))


def _odd_layer(h, mem2d, bsz, seq, norm_g, w_in, w_out, c_qn, c_kn, c_fb, x_qn, x_kn,
               mem_g, w_mem_kv):
    main = 4 * C_WIDTH
    w_t = w_in.T.astype(BF16)
    w_x = w_t[main + C_HEADS:]
    w_f = jnp.pad(w_t[main:main + C_HEADS], ((0, LANES - C_HEADS), (0, 0)))
    chunks = [(0, 0, 0, C_WIDTH, C_DIM), (0, C_WIDTH, C_WIDTH, C_WIDTH, C_DIM),
              (1, 0, main, X_WIDTH, X_DIM), (0, 3 * C_WIDTH, 3 * C_WIDTH, C_WIDTH, "silu"),
              (1, X_WIDTH, main + X_WIDTH, X_WIDTH, "silu"),
              (0, 2 * C_WIDTH, 2 * C_WIDTH, 2 * X_WIDTH, "plain"),
              (0, 2 * C_WIDTH + 2 * X_WIDTH, 2 * C_WIDTH + 2 * X_WIDTH, X_WIDTH, "plain")]
    gains = jnp.concatenate([
        jnp.tile(c_qn, C_HEADS) * (LOG2E * C_DIM ** -0.5), jnp.tile(c_kn, C_HEADS),
        jnp.ones((main - 2 * C_WIDTH,), F32), jnp.tile(x_qn, X_HEADS) * (X_DIM ** -0.5),
        jnp.ones((X_WIDTH,), F32)]).reshape(1, -1)
    proj, f_logits = _norm_proj(h, norm_g, [w_t, w_x], chunks, gains, w_f, w_rows_are_outputs=True)
    cum = _forget_cum(f_logits, c_fb, bsz, seq)
    yc = _fox_attention(proj, cum, bsz, seq)
    kv = _mem_kv(mem2d, mem_g, w_mem_kv.astype(BF16), x_kn, bsz)
    return _mix_out(yc, proj, kv, w_out.astype(BF16), h, seq, x_col_block=main // X_WIDTH)


def kernel(x, mem, e_norm_g, e_w_in, e_w_out, e_a_q_norm_g, e_a_k_norm_g, e_a_lambda,
           e_a_out_norm_g, e_b_conv_w, e_b_conv_b, e_x_q_norm_g, e_x_k_norm_g, e_mem_norm_g,
           e_w_mem_kv, o_norm_g, o_w_in, o_w_out, o_c_q_norm_g, o_c_k_norm_g, o_c_forget_b,
           o_x_q_norm_g, o_x_k_norm_g, o_mem_norm_g, o_w_mem_kv):
    bsz, seq, d = x.shape
    h = x.reshape(bsz * seq, d)
    mem2d = mem.reshape(bsz * MEM_LEN, d)
    depth = e_w_in.shape[0] + o_w_in.shape[0]
    for layer in range(depth):
        i = layer // 2
        if layer % 2 == 0:
            h = _even_layer(h, mem2d, layer, bsz, seq, e_norm_g[i], e_w_in[i], e_w_out[i],
                            e_a_q_norm_g[i], e_a_k_norm_g[i], e_a_lambda[i], e_a_out_norm_g[i],
                            e_b_conv_w[i], e_b_conv_b[i], e_x_q_norm_g[i], e_x_k_norm_g[i],
                            e_mem_norm_g[i], e_w_mem_kv[i])
        else:
            h = _odd_layer(h, mem2d, bsz, seq, o_norm_g[i], o_w_in[i], o_w_out[i],
                           o_c_q_norm_g[i], o_c_k_norm_g[i], o_c_forget_b[i],
                           o_x_q_norm_g[i], o_x_k_norm_g[i], o_mem_norm_g[i], o_w_mem_kv[i])
    return h.reshape(bsz, seq, d)
```

```python
import functools
import math

import numpy as np
import jax
import jax.numpy as jnp
from jax import lax
from jax.experimental import pallas as pl
from jax.experimental.pallas import tpu as pltpu

F32 = jnp.float32
BF16 = jnp.bfloat16

D_MODEL = 1024
MEM_LEN = 256
EPS = 1e-6
NEG = -1e30
LOG2E = math.log2(math.e)

A_HEADS = 8
A_QK_DIM = 64
A_V_DIM = 128
A_WIDTH = 1024
B_WIDTH = 512
CONV_W = 3
C_HEADS = 12
C_DIM = 128
C_WIDTH = 1536
X_HEADS = 4
X_DIM = 128
X_WIDTH = 512
MIX = 2048

LANES = 128
ONES_ROWS = 16
CUM_ROWS = 16
VMEM_CAP = 56 * 1024 * 1024

PROJ_TM = 512
OUT_TM = 512
HALO = 8
ATT_T = 256
A_GROUP = 4
C_GROUP = 4
ATT_TEMP_BYTES = 26 << 20


def _vmem_limit(block_bytes, scratch_bytes=0, temp_bytes=0):
    need = 2 * block_bytes + scratch_bytes + temp_bytes + (4 << 20)
    return int(min(max(need, 16 << 20), VMEM_CAP))


def _silu(z):
    half = 0.5 * z
    return half + half * jnp.tanh(half)


def _nt_dot(a, b):
    return lax.dot_general(a, b, (((1,), (1,)), ((), ())), preferred_element_type=F32)


def _group_rms_inv(x, group):
    x2 = x * x
    if group == LANES:
        return lax.rsqrt(jnp.mean(x2, axis=-1, keepdims=True) + EPS)
    assert 2 * group == LANES
    first = lax.broadcasted_iota(jnp.int32, x.shape, 1) < group
    ms_a = jnp.sum(jnp.where(first, x2, 0.0), axis=-1, keepdims=True) * (1.0 / group)
    ms_b = jnp.sum(jnp.where(first, 0.0, x2), axis=-1, keepdims=True) * (1.0 / group)
    return lax.rsqrt(jnp.where(first, ms_a, ms_b) + EPS)


def _proj_kernel(*refs, n_weights, has_side, w_rows_are_outputs, chunks):
    x_ref, g_ref, gain_ref = refs[:3]
    w_refs = refs[3:3 + n_weights]
    if has_side:
        ws_ref, o_ref, side_ref = refs[3 + n_weights:]
    else:
        (o_ref,) = refs[3 + n_weights:]
    matmul = _nt_dot if w_rows_are_outputs else functools.partial(jnp.dot, preferred_element_type=F32)

    x = x_ref[...]
    ms = jnp.mean(x * x, axis=-1, keepdims=True)
    xn = (x * lax.rsqrt(ms + EPS) * g_ref[...]).astype(BF16)
    if has_side:
        side_ref[...] = matmul(xn, ws_ref[...])
    for src, w0, c0, width, kind in chunks:
        cols = slice(c0, c0 + width)
        w = w_refs[src][w0:w0 + width, :] if w_rows_are_outputs else w_refs[src][:, w0:w0 + width]
        y = matmul(xn, w)
        if kind == "silu":
            o_ref[:, cols] = _silu(y).astype(o_ref.dtype)
        elif kind == "plain":
            o_ref[:, cols] = y.astype(o_ref.dtype)
        else:
            for h0 in range(0, width, LANES):
                yh = y[:, h0:h0 + LANES]
                gain = gain_ref[:, c0 + h0:c0 + h0 + LANES]
                o_ref[:, c0 + h0:c0 + h0 + LANES] = (yh * _group_rms_inv(yh, kind) * gain).astype(o_ref.dtype)


def _norm_proj(h, g, weights, chunks, gains, w_side=None, w_rows_are_outputs=False):
    t, d = h.shape
    n = gains.shape[1]
    assert sum(width for _, _, _, width, _ in chunks) == n
    tm = PROJ_TM
    resident = lambda shape: pl.BlockSpec(shape, lambda i: (0, 0), pipeline_mode=pl.Buffered(1))
    in_specs = [
        pl.BlockSpec((tm, d), lambda i: (i, 0)),
        pl.BlockSpec((1, d), lambda i: (0, 0)),
        pl.BlockSpec((1, n), lambda i: (0, 0)),
    ] + [resident(w.shape) for w in weights]
    out_specs = pl.BlockSpec((tm, n), lambda i: (i, 0))
    out_shape = jax.ShapeDtypeStruct((t, n), BF16)
    args = [h, g.reshape(1, d), gains, *weights]
    blocks = tm * d * 4 + tm * n * 2 + 8 * n * 4
    single = sum(w.size * 2 for w in weights)
    if w_side is not None:
        ns = w_side.shape[0] if w_rows_are_outputs else w_side.shape[1]
        in_specs.append(resident(w_side.shape))
        out_specs = [out_specs, pl.BlockSpec((tm, ns), lambda i: (i, 0))]
        out_shape = [out_shape, jax.ShapeDtypeStruct((t, ns), F32)]
        args.append(w_side)
        blocks += tm * ns * 4
        single += w_side.size * 2
    return pl.pallas_call(
        functools.partial(_proj_kernel, n_weights=len(weights), has_side=w_side is not None,
                          w_rows_are_outputs=w_rows_are_outputs, chunks=tuple(chunks)),
        grid=(t // tm,),
        in_specs=in_specs,
        out_specs=out_specs,
        out_shape=out_shape,
        compiler_params=pltpu.CompilerParams(
            dimension_semantics=("parallel",),
            vmem_limit_bytes=_vmem_limit(
                blocks, single, 3 * tm * max(c[3] for c in chunks) * 4 + 2 * tm * d * 4)),
        name="norm_proj",
    )(*args)


def _split3(c):
    c1 = c.astype(BF16).astype(F32)
    r = c - c1
    c2 = r.astype(BF16).astype(F32)
    c3 = (r - c2).astype(BF16).astype(F32)
    return c1, c2, c3


def _bias_lanes(term, key_side):
    n = term.shape[1]
    parts = jnp.concatenate(_split3(term), axis=0)
    ones = jnp.ones((3, n), F32)
    pad = jnp.zeros((10, n), F32)
    rows = jnp.concatenate([parts, ones, pad] if key_side else [ones, parts, pad], axis=0)
    sel = (lax.broadcasted_iota(jnp.int32, (16, LANES), 0)
           == lax.broadcasted_iota(jnp.int32, (16, LANES), 1)).astype(BF16)
    lanes = lax.dot_general(rows.astype(BF16), sel, (((0,), (0,)), ((), ())),
                            preferred_element_type=F32)
    return lanes.astype(BF16)


def _score_phase(chains, c, t):
    krow = lax.broadcasted_iota(jnp.int32, (t, t), 0)
    qcol = lax.broadcasted_iota(jnp.int32, (t, t), 1)
    lo, hi = c * t, (c + 1) * t
    scores = []
    for qa, ka_ref, _ in chains:
        s_d = jnp.where(krow <= qcol, _nt_dot(ka_ref[lo:hi, :], qa), NEG)
        m = jnp.max(s_d, axis=0, keepdims=True)
        s_f = None
        if c > 0:
            s_f = _nt_dot(ka_ref[0:lo, :], qa)
            m = jnp.maximum(m, jnp.max(s_f, axis=0, keepdims=True))
        scores.append((s_d, s_f, m))
    return scores


def _value_phase(chains, scores, c, t):
    lo, hi = c * t, (c + 1) * t
    outs = []
    for (_, _, vt_ref), (s_d, s_f, m) in zip(chains, scores):
        dv = vt_ref.shape[0] - ONES_ROWS
        p_d = jnp.exp2(s_d - m).astype(BF16)
        acc = jnp.dot(vt_ref[:, lo:hi], p_d, preferred_element_type=F32)
        if c > 0:
            p_f = jnp.exp2(s_f - m).astype(BF16)
            acc = acc + jnp.dot(vt_ref[:, 0:lo], p_f, preferred_element_type=F32)
        outs.append((acc[:dv], acc[dv:dv + 1]))
    return outs


def _attend_all_tiles(make_chains, finish, n_tiles, t):
    order = list(range(1, n_tiles, 2)) + list(range(n_tiles - 1 - (n_tiles % 2 == 0), -1, -2))
    assert sorted(order) == list(range(n_tiles))
    chains = make_chains(order[0])
    scores = _score_phase(chains, order[0], t)
    for pos, c in enumerate(order):
        if pos + 1 < n_tiles:
            nxt = order[pos + 1]
            next_chains = make_chains(nxt)
            next_scores = _score_phase(next_chains, nxt, t)
        finish(c, _value_phase(chains, scores, c, t))
        if pos + 1 < n_tiles:
            chains, scores = next_chains, next_scores


def _diff_kernel(slopes_ref, q_ref, k_ref, v_ref, z_ref, lam_ref, og_ref,
                 o_ref, ka_ref, vt_ref, *, lam_init, group):
    t = ATT_T
    hw = A_V_DIM
    seq = k_ref.shape[0]
    hg = pl.program_id(1)

    k_pos = lax.broadcasted_iota(jnp.int32, (1, seq), 1).astype(F32)
    for g in range(group):
        cs = slice(g * hw, (g + 1) * hw)
        ka_ref[g, :, :hw] = k_ref[:, cs]
        ka_ref[g, :, hw:] = _bias_lanes(LOG2E * slopes_ref[hg * group + g] * k_pos, True)
        vt_ref[g, :hw] = v_ref[:, cs].astype(F32).T.astype(BF16)
        vt_ref[g, hw:] = jnp.ones((ONES_ROWS, seq), BF16)

    lo = lax.broadcasted_iota(jnp.int32, (t, hw), 1) < A_QK_DIM
    lp = lam_ref[...]
    lam = (jnp.exp(jnp.sum(lp[0:1, :] * lp[1:2, :], axis=-1, keepdims=True))
           - jnp.exp(jnp.sum(lp[2:3, :] * lp[3:4, :], axis=-1, keepdims=True)) + lam_init)

    def make_chains(c):
        rows = slice(c * t, (c + 1) * t)
        q_pos = (c * t + lax.broadcasted_iota(jnp.int32, (1, t), 1)).astype(F32)
        chains = []
        for g in range(group):
            qn = q_ref[rows, g * hw:(g + 1) * hw]
            qb = _bias_lanes(-LOG2E * slopes_ref[hg * group + g] * q_pos, False)
            zero = jnp.zeros_like(qn)
            q1 = jnp.concatenate([jnp.where(lo, qn, zero), qb], axis=1)
            q2 = jnp.concatenate([jnp.where(lo, zero, qn), qb], axis=1)
            chains += [(q1, ka_ref.at[g], vt_ref.at[g]), (q2, ka_ref.at[g], vt_ref.at[g])]
        return chains

    def finish(c, outs):
        rows = slice(c * t, (c + 1) * t)
        for g in range(group):
            cs = slice(g * hw, (g + 1) * hw)
            (a1, l1), (a2, l2) = outs[2 * g], outs[2 * g + 1]
            o = (a1 / l1 - lam * (a2 / l2)).T
            ms = jnp.mean(o * o, axis=-1, keepdims=True)
            on = o * lax.rsqrt(ms + EPS) * og_ref[...] * (1.0 - lam_init)
            o_ref[rows, cs] = (on * z_ref[rows, cs].astype(F32)).astype(o_ref.dtype)

    _attend_all_tiles(make_chains, finish, seq // t, t)


def _diff_attention(proj, lam_params, out_g, lam_init, bsz, seq):
    hw = A_V_DIM
    grp = A_GROUP
    gw = grp * hw
    ng = A_HEADS // grp
    slopes = jnp.asarray(np.array([2.0 ** (-8.0 * (i + 1) / A_HEADS) for i in range(A_HEADS)],
                                  dtype=np.float32))
    blocks = 5 * seq * gw * 2
    scratch = grp * seq * 3 * hw * 2
    small = lambda shape: pl.BlockSpec(shape, lambda b, h: (0, 0))
    cols = lambda first: pl.BlockSpec((seq, gw), lambda b, h: (b, first * ng + h))
    return pl.pallas_call(
        functools.partial(_diff_kernel, lam_init=lam_init, group=grp),
        grid=(bsz, ng),
        in_specs=[
            pl.BlockSpec(memory_space=pltpu.SMEM),
            cols(0), cols(1), cols(2), cols(3),
            small((4, A_QK_DIM)), small((1, hw)),
        ],
        out_specs=cols(0),
        out_shape=jax.ShapeDtypeStruct((bsz * seq, A_WIDTH), BF16),
        scratch_shapes=[pltpu.VMEM((grp, seq, 2 * hw), BF16), pltpu.VMEM((grp, hw + ONES_ROWS, seq), BF16)],
        compiler_params=pltpu.CompilerParams(
            dimension_semantics=("parallel", "parallel"),
            vmem_limit_bytes=_vmem_limit(blocks, scratch, ATT_TEMP_BYTES)),
        name="diff_attn",
    )(slopes, proj, proj, proj, proj, lam_params, out_g.reshape(1, hw))


def _conv_gated(h_ref, c_ref, hp_ref, cp_ref, b_ref, z_ref, w_ref, bias_ref, starts_sequence):
    u = c_ref[...].astype(F32) * h_ref[...].astype(F32)
    up = cp_ref[...].astype(F32) * hp_ref[...].astype(F32)
    up = jnp.where(starts_sequence, 0.0, up)
    row = lax.broadcasted_iota(jnp.int32, u.shape, 0)
    u1 = jnp.where(row >= 1, pltpu.roll(u, 1, 0), up[HALO - 1:HALO, :])
    u2 = jnp.where(row >= 2, pltpu.roll(u, 2, 0),
                   jnp.where(row == 1, up[HALO - 1:HALO, :], up[HALO - 2:HALO - 1, :]))
    w = w_ref[...]
    y = w[0:1, :] * u2 + w[1:2, :] * u1 + w[2:3, :] * u + bias_ref[...]
    return (b_ref[...].astype(F32) * y * z_ref[...].astype(F32)).astype(BF16)


def _mem_kv_kernel(mem_ref, g_ref, w_ref, kg_ref, o_ref):
    x = mem_ref[...]
    ms = jnp.mean(x * x, axis=-1, keepdims=True)
    xn = (x * lax.rsqrt(ms + EPS) * g_ref[...]).astype(BF16)
    kv = jnp.dot(xn, w_ref[...], preferred_element_type=F32)
    for hd in range(X_HEADS):
        k = kv[:, hd * X_DIM:(hd + 1) * X_DIM]
        kms = jnp.mean(k * k, axis=-1, keepdims=True)
        o_ref[:, hd * X_DIM:(hd + 1) * X_DIM] = (k * lax.rsqrt(kms + EPS) * kg_ref[...]).astype(BF16)
    o_ref[:, X_WIDTH:] = kv[:, X_WIDTH:].astype(BF16)


def _mem_kv(mem2d, g, w, k_g, bsz):
    d = mem2d.shape[1]
    blocks = MEM_LEN * d * 4 + w.size * 2 + MEM_LEN * 2 * X_WIDTH * 2
    return pl.pallas_call(
        _mem_kv_kernel,
        grid=(bsz,),
        in_specs=[pl.BlockSpec((MEM_LEN, d), lambda b: (b, 0)),
                  pl.BlockSpec((1, d), lambda b: (0, 0)),
                  pl.BlockSpec(w.shape, lambda b: (0, 0)),
                  pl.BlockSpec((1, X_DIM), lambda b: (0, 0))],
        out_specs=pl.BlockSpec((MEM_LEN, 2 * X_WIDTH), lambda b: (b, 0)),
        out_shape=jax.ShapeDtypeStruct((bsz * MEM_LEN, 2 * X_WIDTH), BF16),
        compiler_params=pltpu.CompilerParams(
            dimension_semantics=("parallel",),
            vmem_limit_bytes=_vmem_limit(blocks, 0, 4 * MEM_LEN * d * 4)),
        name="mem_kv",
    )(mem2d, g.reshape(1, d), w, k_g.reshape(1, X_DIM))


def _xattn_scores(q_ref, kv_ref):
    return [_nt_dot(q_ref[:, hd * X_DIM:(hd + 1) * X_DIM], kv_ref[:, hd * X_DIM:(hd + 1) * X_DIM])
            for hd in range(X_HEADS)]


def _xattn_gated(scores, z_ref, kv_ref):
    heads = []
    for hd, s in enumerate(scores):
        cs = slice(hd * X_DIM, (hd + 1) * X_DIM)
        m = jnp.max(s, axis=-1, keepdims=True)
        p = jnp.exp(s - m)
        l = jnp.sum(p, axis=-1, keepdims=True)
        vs = slice(X_WIDTH + hd * X_DIM, X_WIDTH + (hd + 1) * X_DIM)
        o = jnp.dot(p.astype(BF16), kv_ref[:, vs], preferred_element_type=F32) / l
        heads.append((o * z_ref[:, cs].astype(F32)).astype(BF16))
    return jnp.concatenate(heads, axis=1)


def _forget_cum_kernel(f_ref, b_ref, cum_ref):
    x = f_ref[...] + b_ref[...]
    c = jnp.minimum(x, 0.0) - jnp.log1p(jnp.exp(-jnp.abs(x)))
    n = c.shape[0]
    row = lax.broadcasted_iota(jnp.int32, c.shape, 0)
    shift = 1
    while shift < n:
        c = c + jnp.where(row >= shift, pltpu.roll(c, shift, 0), 0.0)
        shift *= 2
    cum_ref[...] = c.T[:cum_ref.shape[0], :]


def _forget_cum(f_logits, f_bias, bsz, seq):
    bias = jnp.pad(f_bias.astype(F32), (0, LANES - C_HEADS)).reshape(1, LANES)
    blocks = seq * LANES * 4 + CUM_ROWS * seq * 4
    return pl.pallas_call(
        _forget_cum_kernel,
        grid=(bsz,),
        in_specs=[pl.BlockSpec((seq, LANES), lambda b: (b, 0)),
                  pl.BlockSpec((1, LANES), lambda b: (0, 0))],
        out_specs=pl.BlockSpec((CUM_ROWS, seq), lambda b: (b, 0)),
        out_shape=jax.ShapeDtypeStruct((bsz * CUM_ROWS, seq), F32),
        compiler_params=pltpu.CompilerParams(
            dimension_semantics=("parallel",),
            vmem_limit_bytes=_vmem_limit(blocks, 0, 6 * seq * LANES * 4)),
        name="forget_cum",
    )(f_logits, bias)


def _fox_kernel(q_ref, k_ref, v_ref, z_ref, cum_ref, o_ref, ka_ref, vt_ref, *, group):
    t = ATT_T
    hw = C_DIM
    seq = k_ref.shape[0]

    for g in range(group):
        cs = slice(g * hw, (g + 1) * hw)
        ka_ref[g, :, :hw] = k_ref[:, cs]
        ka_ref[g, :, hw:] = _bias_lanes(-LOG2E * cum_ref[g], True)
        vt_ref[g, :hw] = v_ref[:, cs].astype(F32).T.astype(BF16)
        vt_ref[g, hw:] = jnp.ones((ONES_ROWS, seq), BF16)

    def make_chains(c):
        rows = slice(c * t, (c + 1) * t)
        chains = []
        for g in range(group):
            qb = _bias_lanes(LOG2E * cum_ref[g, :, rows], False)
            qa = jnp.concatenate([q_ref[rows, g * hw:(g + 1) * hw], qb], axis=1)
            chains.append((qa, ka_ref.at[g], vt_ref.at[g]))
        return chains

    def finish(c, outs):
        rows = slice(c * t, (c + 1) * t)
        for g, (acc, l) in enumerate(outs):
            cs = slice(g * hw, (g + 1) * hw)
            o_ref[rows, cs] = ((acc / l).T * z_ref[rows, cs].astype(F32)).astype(o_ref.dtype)

    _attend_all_tiles(make_chains, finish, seq // t, t)


def _fox_attention(proj, cum, bsz, seq):
    hw = C_DIM
    grp = C_GROUP
    gw = grp * hw
    ng = C_HEADS // grp
    blocks = 5 * seq * gw * 2 + grp * 8 * seq * 4
    scratch = grp * seq * 3 * hw * 2
    cum3 = cum.reshape(bsz * CUM_ROWS, 1, seq)
    cols = lambda first: pl.BlockSpec((seq, gw), lambda b, h: (b, first * ng + h))
    return pl.pallas_call(
        functools.partial(_fox_kernel, group=grp),
        grid=(bsz, ng),
        in_specs=[
            cols(0), cols(1), cols(2), cols(3),
            pl.BlockSpec((grp, 1, seq), lambda b, h: (b * (CUM_ROWS // grp) + h, 0, 0)),
        ],
        out_specs=cols(0),
        out_shape=jax.ShapeDtypeStruct((bsz * seq, C_WIDTH), BF16),
        scratch_shapes=[pltpu.VMEM((grp, seq, 2 * hw), BF16), pltpu.VMEM((grp, hw + ONES_ROWS, seq), BF16)],
        compiler_params=pltpu.CompilerParams(
            dimension_semantics=("parallel", "parallel"),
            vmem_limit_bytes=_vmem_limit(blocks, scratch, ATT_TEMP_BYTES)),
        name="fox_attn",
    )(proj, proj, proj, proj, cum3)


def _mix_out_kernel(*refs, has_conv, tiles_per_seq):
    refs = list(refs)
    attn_ref = refs.pop(0)
    conv_refs = [refs.pop(0) for _ in range(8)] if has_conv else None
    xq_ref, xz_ref, kv_ref, w_ref, h_ref, o_ref = refs

    if has_conv:
        starts_sequence = pl.program_id(0) % tiles_per_seq == 0
        yb = _conv_gated(*conv_refs, starts_sequence)
    scores = _xattn_scores(xq_ref, kv_ref)
    width = attn_ref.shape[1]
    acc = h_ref[...] + jnp.dot(attn_ref[...], w_ref[:width, :], preferred_element_type=F32)
    if has_conv:
        acc = acc + jnp.dot(yb, w_ref[width:width + B_WIDTH, :], preferred_element_type=F32)
        width += B_WIDTH
    yx = _xattn_gated(scores, xz_ref, kv_ref)
    o_ref[...] = acc + jnp.dot(yx, w_ref[width:, :], preferred_element_type=F32)


def _mix_out(y_attn, proj, kv, w_out, h, seq, x_col_block, conv=None):
    t, d = h.shape
    tm = OUT_TM
    tiles_per_seq = seq // tm
    aw = y_attn.shape[1]
    rows = lambda width, col: pl.BlockSpec((tm, width), lambda i: (i, col))
    in_specs = [rows(aw, 0)]
    args = [y_attn]
    blocks = tm * aw * 2 + 2 * tm * X_WIDTH * 2 + MEM_LEN * 2 * X_WIDTH * 2 + w_out.size * 2 + 2 * tm * d * 4
    if conv is not None:
        cw, cb, base = conv
        prev = lambda col: pl.BlockSpec(
            (HALO, B_WIDTH), lambda i: (jnp.maximum(i * (tm // HALO) - 1, 0), col))
        in_specs += [rows(B_WIDTH, base), rows(B_WIDTH, base + 1), prev(base), prev(base + 1),
                     rows(B_WIDTH, base + 2), rows(B_WIDTH, base + 3),
                     pl.BlockSpec((CONV_W, B_WIDTH), lambda i: (0, 0)),
                     pl.BlockSpec((1, B_WIDTH), lambda i: (0, 0))]
        args += [proj] * 6 + [cw, cb.reshape(1, B_WIDTH)]
        blocks += 4 * tm * B_WIDTH * 2
    in_specs += [rows(X_WIDTH, x_col_block), rows(X_WIDTH, x_col_block + 1),
                 pl.BlockSpec((MEM_LEN, 2 * X_WIDTH), lambda i: (i // tiles_per_seq, 0)),
                 pl.BlockSpec(w_out.shape, lambda i: (0, 0)),
                 pl.BlockSpec((tm, d), lambda i: (i, 0))]
    args += [proj, proj, kv, w_out, h]
    return pl.pallas_call(
        functools.partial(_mix_out_kernel, has_conv=conv is not None, tiles_per_seq=tiles_per_seq),
        grid=(t // tm,),
        in_specs=in_specs,
        out_specs=pl.BlockSpec((tm, d), lambda i: (i, 0)),
        out_shape=jax.ShapeDtypeStruct((t, d), F32),
        compiler_params=pltpu.CompilerParams(
            dimension_semantics=("parallel",),
            vmem_limit_bytes=_vmem_limit(blocks, 0, 3 * tm * d * 4 + 8 * tm * B_WIDTH * 4)),
        name="mix_out",
    )(*args)


def _even_layer(h, mem2d, layer, bsz, seq, norm_g, w_in, w_out, a_qn, a_kn, a_lam, a_on,
                b_cw, b_cb, x_qn, x_kn, mem_g, w_mem_kv):
    lam_init = 0.8 - 0.6 * math.exp(-0.3 * layer)
    conv0 = 4 * A_WIDTH
    x0 = conv0 + 4 * B_WIDTH
    chunks = [(0, A_WIDTH, A_QK_DIM), (A_WIDTH, A_WIDTH, A_QK_DIM), (x0, X_WIDTH, X_DIM),
              (3 * A_WIDTH, A_WIDTH, "silu"), (conv0 + 3 * B_WIDTH, B_WIDTH, "silu"),
              (x0 + X_WIDTH, X_WIDTH, "silu"), (2 * A_WIDTH, A_WIDTH, "plain"), (conv0, 3 * B_WIDTH, "plain")]
    chunks = [(0, c0, c0, width, kind) for c0, width, kind in chunks]
    gains = jnp.concatenate([
        jnp.tile(a_qn, 2 * A_HEADS) * (LOG2E * A_QK_DIM ** -0.5), jnp.tile(a_kn, 2 * A_HEADS),
        jnp.ones((x0 - 2 * A_WIDTH,), F32), jnp.tile(x_qn, X_HEADS) * (X_DIM ** -0.5),
        jnp.ones((X_WIDTH,), F32)]).reshape(1, -1)
    proj = _norm_proj(h, norm_g, [w_in.astype(BF16)], chunks, gains)
    ya = _diff_attention(proj, a_lam, a_on, lam_init, bsz, seq)
    kv = _mem_kv(mem2d, mem_g, w_mem_kv.astype(BF16), x_kn, bsz)
    return _mix_out(ya, proj, kv, w_out.astype(BF16), h, seq, x_col_block=x0 // X_WIDTH,
                    conv=(b_cw, b_cb, conv0 // B_WIDTH))


def _odd_layer(h, mem2d, bsz, seq, norm_g, w_in, w_out, c_qn, c_kn, c_fb, x_qn, x_kn,
               mem_g, w_mem_kv):
    main = 4 * C_WIDTH
    w_t = w_in.T.astype(BF16)
    w_x = w_t[main + C_HEADS:]
    w_f = jnp.pad(w_t[main:main + C_HEADS], ((0, LANES - C_HEADS), (0, 0)))
    chunks = [(0, 0, 0, C_WIDTH, C_DIM), (0, C_WIDTH, C_WIDTH, C_WIDTH, C_DIM),
              (1, 0, main, X_WIDTH, X_DIM), (0, 3 * C_WIDTH, 3 * C_WIDTH, C_WIDTH, "silu"),
              (1, X_WIDTH, main + X_WIDTH, X_WIDTH, "silu"), (0, 2 * C_WIDTH, 2 * C_WIDTH, C_WIDTH, "plain")]
    gains = jnp.concatenate([
        jnp.tile(c_qn, C_HEADS) * (LOG2E * C_DIM ** -0.5), jnp.tile(c_kn, C_HEADS),
        jnp.ones((main - 2 * C_WIDTH,), F32), jnp.tile(x_qn, X_HEADS) * (X_DIM ** -0.5),
        jnp.ones((X_WIDTH,), F32)]).reshape(1, -1)
    proj, f_logits = _norm_proj(h, norm_g, [w_t, w_x], chunks, gains, w_f, w_rows_are_outputs=True)
    cum = _forget_cum(f_logits, c_fb, bsz, seq)
    yc = _fox_attention(proj, cum, bsz, seq)
    kv = _mem_kv(mem2d, mem_g, w_mem_kv.astype(BF16), x_kn, bsz)
    return _mix_out(yc, proj, kv, w_out.astype(BF16), h, seq, x_col_block=main // X_WIDTH)


def kernel(x, mem, e_norm_g, e_w_in, e_w_out, e_a_q_norm_g, e_a_k_norm_g, e_a_lambda,
           e_a_out_norm_g, e_b_conv_w, e_b_conv_b, e_x_q_norm_g, e_x_k_norm_g, e_mem_norm_g,
           e_w_mem_kv, o_norm_g, o_w_in, o_w_out, o_c_q_norm_g, o_c_k_norm_g, o_c_forget_b,
           o_x_q_norm_g, o_x_k_norm_g, o_mem_norm_g, o_w_mem_kv):
    bsz, seq, d = x.shape
    h = x.reshape(bsz * seq, d)
    mem2d = mem.reshape(bsz * MEM_LEN, d)
    depth = e_w_in.shape[0] + o_w_in.shape[0]
    for layer in range(depth):
        i = layer // 2
        if layer % 2 == 0:
            h = _even_layer(h, mem2d, layer, bsz, seq, e_norm_g[i], e_w_in[i], e_w_out[i],
                            e_a_q_norm_g[i], e_a_k_norm_g[i], e_a_lambda[i], e_a_out_norm_g[i],
                            e_b_conv_w[i], e_b_conv_b[i], e_x_q_norm_g[i], e_x_k_norm_g[i],
                            e_mem_norm_g[i], e_w_mem_kv[i])
        else:
            h = _odd_layer(h, mem2d, bsz, seq, o_norm_g[i], o_w_in[i], o_w_out[i],
                           o_c_q_norm_g[i], o_c_k_norm_g[i], o_c_forget_b[i],
                           o_x_q_norm_g[i], o_x_k_norm_g[i], o_mem_norm_g[i], o_w_mem_kv[i])
    return h.reshape(bsz, seq, d)
```
